```python
import math
import jax, jax.numpy as jnp
from jax import lax
import numpy as np

D_MODEL = 1024
BATCH = 4
SEQ = 8192
DEPTH = 2
DEC_BATCH = 8
DEC_SEQ = 64
PAST_LEN = 4096

CHUNK = 64
N_EVEN = (DEPTH + 1) // 2
N_ODD = DEPTH // 2
EPS = 1e-6
LRU_WIDTH = D_MODEL // 2
LRU_BLOCKS = 8
LRU_BLOCK_DIM = LRU_WIDTH // LRU_BLOCKS
CONV_WIDTH = 4
LRU_C = 8.0
ATT_HEADS = 8
HEAD_DIM = (D_MODEL // 2) // ATT_HEADS
ATT_WIDTH = ATT_HEADS * HEAD_DIM
LEFT_CHUNKS = 8
BAND = (LEFT_CHUNKS + 1) * CHUNK
WINDOW = LEFT_CHUNKS * CHUNK
MAX_REL = 128
IN_AB = 2 * LRU_WIDTH + 3 * ATT_WIDTH
MIX_AB = LRU_WIDTH + ATT_WIDTH
SSM_WIDTH = D_MODEL
SSM_GROUP = 16
SSM_GROUPS = SSM_WIDTH // SSM_GROUP
SSM_STATE = 64
DT_MIN = 1e-3
DT_MAX = 1e-1
D_FF = 2816

kernel_name = "hybrid_streaming_encoder_step"


def _rms_norm(x, g):
    xf = x.astype(jnp.float32)
    y = xf * lax.rsqrt(jnp.mean(xf * xf, axis=-1, keepdims=True) + EPS)
    return (y * g.astype(jnp.float32)).astype(x.dtype)


def _swiglu(x, w_in, w_out):
    gate, up = jnp.split(x @ w_in, 2, axis=-1)
    return (jax.nn.silu(gate) * up) @ w_out


def _causal_dw_conv(x, prev, w, b):
    S = x.shape[1]
    xp = jnp.concatenate([prev.astype(x.dtype), x], axis=1)
    y = b.astype(x.dtype)
    for k in range(CONV_WIDTH):
        y = y + xp[:, k:k + S] * w[k]
    return y, xp[:, xp.shape[1] - (CONV_WIDTH - 1):]


def _rg_lru(xc, h0, wa, ba, wx, bx, lam):
    Bn, S, W = xc.shape
    xb = xc.reshape(Bn, S, LRU_BLOCKS, LRU_BLOCK_DIM)
    r = jax.nn.sigmoid(jnp.einsum('bshi,hij->bshj', xb, wa).reshape(Bn, S, W) + ba)
    i = jax.nn.sigmoid(jnp.einsum('bshi,hij->bshj', xb, wx).reshape(Bn, S, W) + bx)
    log_a = -LRU_C * r.astype(jnp.float32) * jax.nn.softplus(-lam.astype(jnp.float32))
    a = jnp.exp(log_a)
    u = jnp.sqrt(-jnp.expm1(2.0 * log_a)) * (i * xc).astype(jnp.float32)

    def step(h, au):
        a_t, u_t = au
        h = a_t * h + u_t
        return h, h

    h_last, hs = lax.scan(step, h0.astype(jnp.float32), (jnp.swapaxes(a, 0, 1), jnp.swapaxes(u, 0, 1)))
    return jnp.swapaxes(hs, 0, 1).astype(xc.dtype), h_last


def _band_attention(q, k, v, q_pos, k_pos, rel_table):
    rel = q_pos[:, :, None] - k_pos[:, None, :]
    bias = rel_table[jnp.clip(rel, -MAX_REL, MAX_REL) + MAX_REL]
    qc = (q_pos // CHUNK)[:, :, None]
    kc = (k_pos // CHUNK)[:, None, :]
    valid = (k_pos[:, None, :] >= 0) & (kc <= qc) & (kc >= qc - LEFT_CHUNKS)
    s = jnp.einsum('bnqhd,bnkhd->bnhqk', q, k).astype(jnp.float32) / math.sqrt(HEAD_DIM)
    s = s + jnp.transpose(bias, (0, 3, 1, 2)).astype(jnp.float32)[None]
    s = jnp.where(valid[None, :, None], s, -1e30)
    p = jax.nn.softmax(s, axis=-1).astype(v.dtype)
    return jnp.einsum('bnhqk,bnkhd->bnqhd', p, v)


def _prompt_band_attention(q, k, v, rel_table):
    Bn, S = q.shape[:2]
    nc = S // CHUNK
    qch = q.reshape(Bn, nc, CHUNK, ATT_HEADS, HEAD_DIM)
    pad = ((0, 0), (LEFT_CHUNKS * CHUNK, 0), (0, 0), (0, 0))
    kp = jnp.pad(k, pad).reshape(Bn, nc + LEFT_CHUNKS, CHUNK, ATT_HEADS, HEAD_DIM)
    vp = jnp.pad(v, pad).reshape(Bn, nc + LEFT_CHUNKS, CHUNK, ATT_HEADS, HEAD_DIM)
    k_band = jnp.concatenate([kp[:, o:o + nc] for o in range(LEFT_CHUNKS + 1)], axis=2)
    v_band = jnp.concatenate([vp[:, o:o + nc] for o in range(LEFT_CHUNKS + 1)], axis=2)
    q_pos = jnp.arange(S, dtype=jnp.int32).reshape(nc, CHUNK)
    k_pos = (jnp.arange(nc, dtype=jnp.int32)[:, None] - LEFT_CHUNKS) * CHUNK + jnp.arange(BAND, dtype=jnp.int32)[None]
    out = _band_attention(qch, k_band, v_band, q_pos, k_pos, rel_table)
    return out.reshape(Bn, S, ATT_HEADS, HEAD_DIM)


def _sample_band_attention(q, k, v, cache_k, cache_v, rel_table):
    T = q.shape[1]
    wc = cache_k.shape[1]
    kk = jnp.concatenate([cache_k.astype(k.dtype), k], axis=1)[:, None]
    vv = jnp.concatenate([cache_v.astype(v.dtype), v], axis=1)[:, None]
    q_pos = (PAST_LEN + jnp.arange(T, dtype=jnp.int32))[None]
    k_pos = (PAST_LEN - wc + jnp.arange(wc + T, dtype=jnp.int32))[None]
    return _band_attention(q[:, None], kk, vv, q_pos, k_pos, rel_table)[:, 0]


def _even_mixer(h, conv_prev, lru_prev, cache_k, cache_v, w_in, conv_w, conv_b, lru_wa, lru_ba,
                lru_wx, lru_bx, lru_lambda, q_gain, k_gain, rel_bias, w_out, prompt):
    Bn, S, _ = h.shape
    proj = h @ w_in
    xa, ga, q, k, v = jnp.split(proj, [LRU_WIDTH, 2 * LRU_WIDTH, 2 * LRU_WIDTH + ATT_WIDTH,
                                       2 * LRU_WIDTH + 2 * ATT_WIDTH], axis=-1)
    xc, conv_new = _causal_dw_conv(xa, conv_prev, conv_w, conv_b)
    ya, lru_new = _rg_lru(xc, lru_prev, lru_wa, lru_ba, lru_wx, lru_bx, lru_lambda)
    ya = ya * jax.nn.gelu(ga)
    q = _rms_norm(q.reshape(Bn, S, ATT_HEADS, HEAD_DIM), q_gain)
    k = _rms_norm(k.reshape(Bn, S, ATT_HEADS, HEAD_DIM), k_gain)
    v = v.reshape(Bn, S, ATT_HEADS, HEAD_DIM)
    if prompt:
        yb = _prompt_band_attention(q, k, v, rel_bias)
        keep = min(WINDOW, S)
        k_new, v_new = k[:, S - keep:], v[:, S - keep:]
    else:
        yb = _sample_band_attention(q, k, v, cache_k, cache_v, rel_bias)
        k_new, v_new = k, v
    y = jnp.concatenate([ya, yb.reshape(Bn, S, ATT_WIDTH)], axis=-1) @ w_out
    return y, conv_new, lru_new, k_new, v_new


def _s5(u, s0_re, s0_im, A_re, A_im, B_re, B_im, C_re, C_im, D_skip, log_dt):
    Bn, S, W = u.shape
    f32 = jnp.float32
    A = lax.complex(A_re.astype(f32), A_im.astype(f32))
    dt = jnp.exp(log_dt.astype(f32))[:, None]
    A_bar = jnp.exp(A * dt)
    Bc = lax.complex(B_re.astype(f32), B_im.astype(f32))
    B_bar = ((A_bar - 1.0) / A)[..., None] * Bc
    Cc = lax.complex(C_re.astype(f32), C_im.astype(f32))
    ug = u.reshape(Bn, S, SSM_GROUPS, SSM_GROUP).astype(f32)
    bu = jnp.einsum('bsgi,gpi->bsgp', ug.astype(jnp.complex64), B_bar)
    s0 = lax.complex(s0_re.astype(f32), s0_im.astype(f32))
    bu = bu.at[:, 0].add(A_bar * s0)
    a = jnp.broadcast_to(A_bar, (1, S, SSM_GROUPS, SSM_STATE))

    def combine(e1, e2):
        a1, b1 = e1
        a2, b2 = e2
        return a1 * a2, a2 * b1 + b2

    _, s = lax.associative_scan(combine, (a, bu), axis=1)
    y = jnp.einsum('gip,bsgp->bsgi', Cc, s).real.reshape(Bn, S, W) + D_skip.astype(f32) * u.astype(f32)
    s_last = s[:, -1]
    return y.astype(u.dtype), s_last.real, s_last.imag


def _odd_mixer(h, s_re, s_im, A_re, A_im, B_re, B_im, C_re, C_im, D_skip, log_dt, glu_w):
    y, n_re, n_im = _s5(h, s_re, s_im, A_re, A_im, B_re, B_im, C_re, C_im, D_skip, log_dt)
    a, g = jnp.split(y @ glu_w, 2, axis=-1)
    return a * jax.nn.sigmoid(g), n_re, n_im


def _trunk(x, p, conv_st, lru_st, cache_k, cache_v, ssm_re_st, ssm_im_st, prompt):
    Bn = x.shape[0]
    conv_out, lru_out, k_out, v_out, re_out, im_out = [], [], [], [], [], []
    for l in range(DEPTH):
        x = x + 0.5 * _swiglu(_rms_norm(x, p['ffn1_norm'][l]), p['ffn1_w_in'][l], p['ffn1_w_out'][l])
        h = _rms_norm(x, p['mix_norm'][l])
        if l % 2 == 0:
            e = l // 2
            if prompt:
                c_prev = jnp.zeros((Bn, CONV_WIDTH - 1, LRU_WIDTH), x.dtype)
                h_prev = jnp.zeros((Bn, LRU_WIDTH), jnp.float32)
                ck, cv = None, None
            else:
                c_prev, h_prev, ck, cv = conv_st[e], lru_st[e], cache_k[e], cache_v[e]
            y, c_new, h_new, k_new, v_new = _even_mixer(
                h, c_prev, h_prev, ck, cv, p['ab_w_in'][e], p['conv_w'][e], p['conv_b'][e],
                p['lru_wa'][e], p['lru_ba'][e], p['lru_wx'][e], p['lru_bx'][e], p['lru_lambda'][e],
                p['q_norm'][e], p['k_norm'][e], p['rel_bias'][e], p['ab_w_out'][e], prompt)
            conv_out.append(c_new)
            lru_out.append(h_new)
            k_out.append(k_new)
            v_out.append(v_new)
        else:
            o = l // 2
            if prompt:
                s_re = jnp.zeros((Bn, SSM_GROUPS, SSM_STATE), jnp.float32)
                s_im = jnp.zeros((Bn, SSM_GROUPS, SSM_STATE), jnp.float32)
            else:
                s_re, s_im = ssm_re_st[o], ssm_im_st[o]
            y, n_re, n_im = _odd_mixer(
                h, s_re, s_im, p['ssm_A_re'][o], p['ssm_A_im'][o], p['ssm_B_re'][o], p['ssm_B_im'][o],
                p['ssm_C_re'][o], p['ssm_C_im'][o], p['ssm_D'][o], p['ssm_log_dt'][o], p['glu_w'][o])
            re_out.append(n_re)
            im_out.append(n_im)
        x = x + y
        x = x + 0.5 * _swiglu(_rms_norm(x, p['ffn2_norm'][l]), p['ffn2_w_in'][l], p['ffn2_w_out'][l])
    return (x, jnp.stack(conv_out), jnp.stack(lru_out), jnp.stack(k_out), jnp.stack(v_out),
            jnp.stack(re_out), jnp.stack(im_out))


def _normal(k, shape, scale):
    return scale * jax.random.normal(k, shape, jnp.float32)


def setup_inputs(seed: int = 0) -> dict:
    key = jax.random.key(seed)
    k = jax.random.split(key, 40)
    wc = min(WINDOW, PAST_LEN)
    a0 = jax.random.uniform(k[22], (N_EVEN, LRU_WIDTH), jnp.float32, 0.9, 0.999)
    pa = a0 ** (1.0 / LRU_C)
    lam = jnp.log(pa) - jnp.log1p(-pa)
    n_idx = jnp.arange(SSM_STATE, dtype=jnp.float32)
    return {
        'x_prompt': _normal(k[0], (BATCH, SEQ, D_MODEL), 1.0),
        'x_sample': _normal(k[1], (DEC_BATCH, DEC_SEQ, D_MODEL), 1.0),
        'state_rglru_conv': _normal(k[2], (N_EVEN, DEC_BATCH, CONV_WIDTH - 1, LRU_WIDTH), 1.0),
        'state_rglru_h': _normal(k[3], (N_EVEN, DEC_BATCH, LRU_WIDTH), 0.5),
        'cache_band_k': _normal(k[4], (N_EVEN, DEC_BATCH, wc, ATT_HEADS, HEAD_DIM), 1.0),
        'cache_band_v': _normal(k[5], (N_EVEN, DEC_BATCH, wc, ATT_HEADS, HEAD_DIM), 1.0),
        'state_ssm_re': _normal(k[6], (N_ODD, DEC_BATCH, SSM_GROUPS, SSM_STATE), 0.5),
        'state_ssm_im': _normal(k[7], (N_ODD, DEC_BATCH, SSM_GROUPS, SSM_STATE), 0.5),
        'ffn1_norm': 1.0 + _normal(k[8], (DEPTH, D_MODEL), 0.02),
        'ffn1_w_in': _normal(k[9], (DEPTH, D_MODEL, 2 * D_FF), D_MODEL ** -0.5),
        'ffn1_w_out': _normal(k[10], (DEPTH, D_FF, D_MODEL), D_FF ** -0.5),
        'mix_norm': 1.0 + _normal(k[11], (DEPTH, D_MODEL), 0.02),
        'ffn2_norm': 1.0 + _normal(k[12], (DEPTH, D_MODEL), 0.02),
        'ffn2_w_in': _normal(k[13], (DEPTH, D_MODEL, 2 * D_FF), D_MODEL ** -0.5),
        'ffn2_w_out': _normal(k[14], (DEPTH, D_FF, D_MODEL), D_FF ** -0.5),
        'ab_w_in': _normal(k[15], (N_EVEN, D_MODEL, IN_AB), D_MODEL ** -0.5),
        'conv_w': _normal(k[16], (N_EVEN, CONV_WIDTH, LRU_WIDTH), CONV_WIDTH ** -0.5),
        'conv_b': _normal(k[17], (N_EVEN, LRU_WIDTH), 0.01),
        'lru_wa': _normal(k[18], (N_EVEN, LRU_BLOCKS, LRU_BLOCK_DIM, LRU_BLOCK_DIM), LRU_BLOCK_DIM ** -0.5),
        'lru_ba': _normal(k[19], (N_EVEN, LRU_WIDTH), 0.01),
        'lru_wx': _normal(k[20], (N_EVEN, LRU_BLOCKS, LRU_BLOCK_DIM, LRU_BLOCK_DIM), LRU_BLOCK_DIM ** -0.5),
        'lru_bx': _normal(k[21], (N_EVEN, LRU_WIDTH), 0.01),
        'lru_lambda': lam,
        'q_norm': 1.0 + _normal(k[23], (N_EVEN, HEAD_DIM), 0.02),
        'k_norm': 1.0 + _normal(k[24], (N_EVEN, HEAD_DIM), 0.02),
        'rel_bias': _normal(k[25], (N_EVEN, 2 * MAX_REL + 1, ATT_HEADS), 0.2),
        'ab_w_out': _normal(k[26], (N_EVEN, MIX_AB, D_MODEL), MIX_AB ** -0.5),
        'ssm_A_re': -0.5 + _normal(k[27], (N_ODD, SSM_GROUPS, SSM_STATE), 0.01),
        'ssm_A_im': math.pi * n_idx + _normal(k[28], (N_ODD, SSM_GROUPS, SSM_STATE), 0.01),
        'ssm_B_re': _normal(k[29], (N_ODD, SSM_GROUPS, SSM_STATE, SSM_GROUP), (0.5 / SSM_GROUP) ** 0.5),
        'ssm_B_im': _normal(k[30], (N_ODD, SSM_GROUPS, SSM_STATE, SSM_GROUP), (0.5 / SSM_GROUP) ** 0.5),
        'ssm_C_re': _normal(k[31], (N_ODD, SSM_GROUPS, SSM_GROUP, SSM_STATE), (0.5 / SSM_STATE) ** 0.5),
        'ssm_C_im': _normal(k[32], (N_ODD, SSM_GROUPS, SSM_GROUP, SSM_STATE), (0.5 / SSM_STATE) ** 0.5),
        'ssm_D': _normal(k[33], (N_ODD, SSM_WIDTH), 0.5),
        'ssm_log_dt': jax.random.uniform(k[34], (N_ODD, SSM_GROUPS), jnp.float32, math.log(DT_MIN), math.log(DT_MAX)),
        'glu_w': _normal(k[35], (N_ODD, SSM_WIDTH, 2 * D_MODEL), SSM_WIDTH ** -0.5),
    }


def reference(x_prompt, x_sample, state_rglru_conv, state_rglru_h, cache_band_k, cache_band_v,
              state_ssm_re, state_ssm_im, ffn1_norm, ffn1_w_in, ffn1_w_out, mix_norm, ffn2_norm,
              ffn2_w_in, ffn2_w_out, ab_w_in, conv_w, conv_b, lru_wa, lru_ba, lru_wx, lru_bx,
              lru_lambda, q_norm, k_norm, rel_bias, ab_w_out, ssm_A_re, ssm_A_im, ssm_B_re, ssm_B_im,
              ssm_C_re, ssm_C_im, ssm_D, ssm_log_dt, glu_w):
    p = dict(ffn1_norm=ffn1_norm, ffn1_w_in=ffn1_w_in, ffn1_w_out=ffn1_w_out, mix_norm=mix_norm,
             ffn2_norm=ffn2_norm, ffn2_w_in=ffn2_w_in, ffn2_w_out=ffn2_w_out, ab_w_in=ab_w_in,
             conv_w=conv_w, conv_b=conv_b, lru_wa=lru_wa, lru_ba=lru_ba, lru_wx=lru_wx, lru_bx=lru_bx,
             lru_lambda=lru_lambda, q_norm=q_norm, k_norm=k_norm, rel_bias=rel_bias, ab_w_out=ab_w_out,
             ssm_A_re=ssm_A_re, ssm_A_im=ssm_A_im, ssm_B_re=ssm_B_re, ssm_B_im=ssm_B_im,
             ssm_C_re=ssm_C_re, ssm_C_im=ssm_C_im, ssm_D=ssm_D, ssm_log_dt=ssm_log_dt, glu_w=glu_w)
    y_prompt, p_conv, p_h, p_k, p_v, p_re, p_im = _trunk(
        x_prompt, p, None, None, None, None, None, None, True)
    y_sample, s_conv, s_h, s_k, s_v, s_re, s_im = _trunk(
        x_sample, p, state_rglru_conv, state_rglru_h, cache_band_k, cache_band_v,
        state_ssm_re, state_ssm_im, False)
    return (y_prompt, y_sample, p_conv, p_h, p_k, p_v, p_re, p_im, s_conv, s_h, s_k, s_v, s_re, s_im)
```

```python
import functools
import math

import jax
import jax.numpy as jnp
from jax import lax
from jax.experimental import pallas as pl
from jax.experimental.pallas import tpu as pltpu

F32 = jnp.float32
BF16 = jnp.bfloat16

LANES = 128
SUBLANES = 8
VMEM_LIMIT_BYTES = 56 * 1024 * 1024

EPS = 1e-6
CHUNK = 64
LEFT_CHUNKS = 8
WINDOW = LEFT_CHUNKS * CHUNK
BAND = WINDOW + CHUNK
MAX_REL = 128
HEAD_DIM = 64
CONV_WIDTH = 4
LRU_C = 8.0
SSM_GROUP = 16
SSM_STATE = 64
SEGMENTS = SUBLANES
S5_GROUP_BLOCK = 16
FFN_CHUNK = 256
NEG_INF = -1e30


def _params(*sem):
    return pltpu.CompilerParams(dimension_semantics=sem, vmem_limit_bytes=VMEM_LIMIT_BYTES)


def _resident(shape):
    nd = len(shape)
    return pl.BlockSpec(shape, lambda *_: (0,) * nd, pipeline_mode=pl.Buffered(1))


def _rms(x, g):
    return x * lax.rsqrt(jnp.mean(x * x, axis=-1, keepdims=True) + EPS) * g


def _sigmoid(x):
    return 1.0 / (1.0 + jnp.exp(-x))


def _ffn_body(x_ref, g_ref, win_ref, wout_ref, o_ref, acc_ref):
    n_chunks, _, two_fc = win_ref.shape
    fc = two_fc // 2
    x = x_ref[...]
    h = _rms(x, g_ref[...]).astype(BF16)
    acc_ref[...] = jnp.zeros_like(acc_ref)

    def step(c, carry):
        gu = jnp.dot(h, win_ref[c], preferred_element_type=F32)
        gate, up = gu[:, :fc], gu[:, fc:]
        act = (gate * _sigmoid(gate) * up).astype(BF16)
        acc_ref[...] += jnp.dot(act, wout_ref[c], preferred_element_type=F32)
        return carry

    lax.fori_loop(0, n_chunks, step, 0)
    o_ref[...] = x + 0.5 * acc_ref[...]


def _prep_ffn(w_in, w_out):
    d, two_ff = w_in.shape
    d_ff = two_ff // 2
    nc = d_ff // FFN_CHUNK
    gate = w_in[:, :d_ff].reshape(d, nc, FFN_CHUNK)
    up = w_in[:, d_ff:].reshape(d, nc, FFN_CHUNK)
    w_in_c = jnp.concatenate([gate, up], axis=2).transpose(1, 0, 2).astype(BF16)
    w_out_c = w_out.reshape(nc, FFN_CHUNK, d).astype(BF16)
    return w_in_c, w_out_c


def _ffn(x2d, g, w_in_c, w_out_c, tm):
    n, d = x2d.shape
    return pl.pallas_call(
        _ffn_body,
        grid=(n // tm,),
        in_specs=[
            pl.BlockSpec((tm, d), lambda i: (i, 0)),
            _resident((1, d)),
            _resident(w_in_c.shape),
            _resident(w_out_c.shape),
        ],
        out_specs=pl.BlockSpec((tm, d), lambda i: (i, 0)),
        out_shape=jax.ShapeDtypeStruct((n, d), F32),
        scratch_shapes=[pltpu.VMEM((tm, d), F32)],
        compiler_params=_params("parallel"),
        name="ffn",
    )(x2d, g.reshape(1, d), w_in_c, w_out_c)


def _head_norm(t, gain):
    low = lax.broadcasted_iota(jnp.int32, (t.shape[0], LANES), 1) < HEAD_DIM
    outs = []
    for j in range(t.shape[1] // LANES):
        blk = t[:, j * LANES:(j + 1) * LANES]
        sq = blk * blk
        tot = jnp.sum(sq, axis=-1, keepdims=True)
        lo = jnp.sum(jnp.where(low, sq, 0.0), axis=-1, keepdims=True)
        ms = jnp.where(low, lo, tot - lo) * (1.0 / HEAD_DIM)
        outs.append(blk * lax.rsqrt(ms + EPS) * gain[:, j * LANES:(j + 1) * LANES])
    return jnp.concatenate(outs, axis=-1)


def _inproj_body(x_ref, g_ref, w_ref, qg_ref, kg_ref, xa_ref, ga_ref, q_ref, k_ref, v_ref):
    w = xa_ref.shape[1]
    h = _rms(x_ref[...], g_ref[...]).astype(BF16)
    proj = jnp.dot(h, w_ref[...], preferred_element_type=F32)
    xa_ref[...] = proj[:, 0 * w:1 * w]
    ga_ref[...] = proj[:, 1 * w:2 * w]
    q_ref[...] = _head_norm(proj[:, 2 * w:3 * w], qg_ref[...])
    k_ref[...] = _head_norm(proj[:, 3 * w:4 * w], kg_ref[...])
    v_ref[...] = proj[:, 4 * w:5 * w]


def _inproj(x2d, g, w_in, q_gain, k_gain, tm):
    n, d = x2d.shape
    w = w_in.shape[1] // 5
    heads = w // HEAD_DIM
    row = pl.BlockSpec((tm, w), lambda i: (i, 0))
    return pl.pallas_call(
        _inproj_body,
        grid=(n // tm,),
        in_specs=[
            pl.BlockSpec((tm, d), lambda i: (i, 0)),
            _resident((1, d)),
            _resident(w_in.shape),
            _resident((1, w)),
            _resident((1, w)),
        ],
        out_specs=[row] * 5,
        out_shape=[jax.ShapeDtypeStruct((n, w), F32)] * 5,
        compiler_params=_params("parallel"),
        name="inproj",
    )(x2d, g.reshape(1, d), w_in.astype(BF16),
      jnp.tile(q_gain, heads).reshape(1, w), jnp.tile(k_gain, heads).reshape(1, w))


def _lru_body(xa_ref, ga_ref, c0_ref, h0_ref, cw_ref, cb_ref, wa_ref, ba_ref, wx_ref, bx_ref,
              lam_ref, ya_ref, hlast_ref, ext_ref, hc_ref, a_ref, u_ref):
    t_len = xa_ref.shape[1]
    w = xa_ref.shape[2]

    @pl.when(pl.program_id(1) == 0)
    def _():
        ext_ref[0:SUBLANES, :] = c0_ref[0]
        hc_ref[...] = jnp.broadcast_to(h0_ref[0], hc_ref.shape)

    x = xa_ref[0]
    ext_ref[SUBLANES:SUBLANES + t_len, :] = x
    xc = cb_ref[...] + cw_ref[3:4, :] * x
    for k in range(CONV_WIDTH - 1):
        off = SUBLANES - (CONV_WIDTH - 1) + k
        xc = xc + cw_ref[k:k + 1, :] * ext_ref[off:off + t_len, :]
    ext_ref[0:SUBLANES, :] = x[t_len - SUBLANES:, :]

    xcb = xc.astype(BF16)
    r = _sigmoid(jnp.dot(xcb, wa_ref[...], preferred_element_type=F32) + ba_ref[...])
    i = _sigmoid(jnp.dot(xcb, wx_ref[...], preferred_element_type=F32) + bx_ref[...])
    neg_lam = -lam_ref[...]
    softplus = jnp.maximum(neg_lam, 0.0) + jnp.log1p(jnp.exp(-jnp.abs(neg_lam)))
    log_a = -LRU_C * r * softplus
    a = jnp.exp(log_a)
    a_ref[...] = a
    u_ref[...] = jnp.sqrt(-jnp.tanh(log_a) * (a * a + 1.0)) * (i * xc)

    row = lax.broadcasted_iota(jnp.int32, (SUBLANES, w), 0)

    def block(b, h):
        r0 = pl.multiple_of(b * SUBLANES, SUBLANES)
        a = a_ref[pl.ds(r0, SUBLANES), :]
        u = u_ref[pl.ds(r0, SUBLANES), :]
        for d in (1, 2, 4):
            keep = row >= d
            a_sh = jnp.where(keep, pltpu.roll(a, d, 0), 1.0)
            u_sh = jnp.where(keep, pltpu.roll(u, d, 0), 0.0)
            u = u + a * u_sh
            a = a * a_sh
        hs = a * h + u
        u_ref[pl.ds(r0, SUBLANES), :] = hs
        return jnp.broadcast_to(hs[SUBLANES - 1:SUBLANES, :], (SUBLANES, w))

    h_end = lax.fori_loop(0, t_len // SUBLANES, block, hc_ref[...])
    hc_ref[...] = h_end
    hlast_ref[0] = h_end
    ya_ref[0] = u_ref[...] * jax.nn.gelu(ga_ref[0])


def _block_diag(w):
    nb, n, _ = w.shape
    eye = jnp.eye(nb, dtype=w.dtype)
    return jnp.einsum('hij,hg->higj', w, eye).reshape(nb * n, nb * n)


def _lru(xa, ga, conv0, h0, conv_w, conv_b, wa, ba, wx, bx, lam, t_len):
    b, s, w = xa.shape
    pad = jnp.zeros((b, SUBLANES - (CONV_WIDTH - 1), w), F32)
    c0 = jnp.concatenate([pad, conv0.astype(F32)], axis=1)
    seq = pl.BlockSpec((1, t_len, w), lambda i, t: (i, t, 0))
    vec = _resident((1, w))
    ya, hlast = pl.pallas_call(
        _lru_body,
        grid=(b, s // t_len),
        in_specs=[
            seq, seq,
            pl.BlockSpec((1, SUBLANES, w), lambda i, t: (i, 0, 0)),
            pl.BlockSpec((1, 1, w), lambda i, t: (i, 0, 0)),
            _resident((CONV_WIDTH, w)), vec,
            _resident((w, w)), vec, _resident((w, w)), vec, vec,
        ],
        out_specs=[seq, pl.BlockSpec((1, SUBLANES, w), lambda i, t: (i, 0, 0))],
        out_shape=[jax.ShapeDtypeStruct((b, s, w), F32),
                   jax.ShapeDtypeStruct((b, SUBLANES, w), F32)],
        scratch_shapes=[
            pltpu.VMEM((SUBLANES + t_len, w), F32),
            pltpu.VMEM((SUBLANES, w), F32),
            pltpu.VMEM((t_len, w), F32),
            pltpu.VMEM((t_len, w), F32),
        ],
        compiler_params=_params("parallel", "arbitrary"),
        name="lru",
    )(xa, ga, c0, h0.astype(F32).reshape(b, 1, w), conv_w, conv_b.reshape(1, w),
      _block_diag(wa).astype(BF16), ba.reshape(1, w), _block_diag(wx).astype(BF16),
      bx.reshape(1, w), lam.reshape(1, w))
    return ya, hlast[:, 0, :]


def _attn_body(q_ref, kp_ref, ko_ref, vp_ref, vo_ref, bias_ref, o_ref, kbuf, vbuf, *, mask_first):
    tq = q_ref.shape[1]
    w = q_ref.shape[2]
    kbuf[0:WINDOW, :] = kp_ref[0].astype(BF16)
    kbuf[WINDOW:WINDOW + tq, :] = ko_ref[0].astype(BF16)
    vbuf[0:WINDOW, :] = vp_ref[0].astype(BF16)
    vbuf[WINDOW:WINDOW + tq, :] = vo_ref[0].astype(BF16)
    low = lax.broadcasted_iota(jnp.int32, (CHUNK, LANES), 1) < HEAD_DIM
    col = lax.broadcasted_iota(jnp.int32, (2 * CHUNK, BAND), 1)
    first = pl.program_id(1) == 0
    scale = 1.0 / math.sqrt(HEAD_DIM)

    def chunk(j, carry):
        r0 = pl.multiple_of(j * CHUNK, CHUNK)
        for pr in range(w // LANES):
            lanes = slice(pr * LANES, (pr + 1) * LANES)
            qp = q_ref[0, pl.ds(r0, CHUNK), lanes]
            qs = jnp.concatenate([jnp.where(low, qp, 0.0), jnp.where(low, 0.0, qp)], axis=0)
            kw = kbuf[pl.ds(r0, BAND), lanes]
            s = lax.dot_general(qs.astype(BF16), kw, (((1,), (1,)), ((), ())),
                                preferred_element_type=F32)
            s = s * scale + bias_ref[pr]
            if mask_first:
                s = jnp.where(col >= jnp.where(first, WINDOW - r0, 0), s, NEG_INF)
            p = jnp.exp(s - jnp.max(s, axis=-1, keepdims=True))
            denom = jnp.sum(p, axis=-1, keepdims=True)
            o = jnp.dot(p.astype(BF16), vbuf[pl.ds(r0, BAND), lanes],
                        preferred_element_type=F32) / denom
            o_ref[0, pl.ds(r0, CHUNK), lanes] = jnp.where(low, o[:CHUNK], o[CHUNK:])
        return carry

    lax.fori_loop(0, tq // CHUNK, chunk, 0)


def _band_bias(rel_bias):
    heads = rel_bias.shape[1]
    rel = WINDOW + jnp.arange(CHUNK)[:, None] - jnp.arange(BAND)[None, :]
    tab = rel_bias[jnp.clip(rel, -MAX_REL, MAX_REL) + MAX_REL]
    return jnp.transpose(tab, (2, 0, 1)).reshape(heads // 2, 2 * CHUNK, BAND).astype(F32)


def _attn(q, k_prev, k_own, v_prev, v_own, bias, tq, shift_prev, mask_first):
    b, s, w = q.shape
    own = pl.BlockSpec((1, tq, w), lambda i, t: (i, t, 0))
    prev = pl.BlockSpec((1, WINDOW, w), lambda i, t: (i, jnp.maximum(t - shift_prev, 0), 0))
    return pl.pallas_call(
        functools.partial(_attn_body, mask_first=mask_first),
        grid=(b, s // tq),
        in_specs=[own, prev, own, prev, own, _resident(bias.shape)],
        out_specs=own,
        out_shape=jax.ShapeDtypeStruct((b, s, w), F32),
        scratch_shapes=[pltpu.VMEM((WINDOW + tq, w), BF16), pltpu.VMEM((WINDOW + tq, w), BF16)],
        compiler_params=_params("parallel", "parallel"),
        name="attn",
    )(q, k_prev, k_own, v_prev, v_own, bias)


def _outproj_body(x_ref, ya_ref, yb_ref, wa_ref, wb_ref, o_ref):
    y = jnp.dot(ya_ref[...].astype(BF16), wa_ref[...], preferred_element_type=F32)
    y = y + jnp.dot(yb_ref[...].astype(BF16), wb_ref[...], preferred_element_type=F32)
    o_ref[...] = x_ref[...] + y


def _outproj(x2d, ya, yb, w_out, tm):
    n, d = x2d.shape
    w = ya.shape[1]
    wo = w_out.astype(BF16)
    return pl.pallas_call(
        _outproj_body,
        grid=(n // tm,),
        in_specs=[
            pl.BlockSpec((tm, d), lambda i: (i, 0)),
            pl.BlockSpec((tm, w), lambda i: (i, 0)),
            pl.BlockSpec((tm, w), lambda i: (i, 0)),
            _resident((w, d)), _resident((w, d)),
        ],
        out_specs=pl.BlockSpec((tm, d), lambda i: (i, 0)),
        out_shape=jax.ShapeDtypeStruct((n, d), F32),
        compiler_params=_params("parallel"),
        name="outproj",
    )(x2d, ya, yb, wo[:w], wo[w:])


def _s5_load_h(x_ref, g_ref, hnat, hperm):
    tm = x_ref.shape[2]
    n_lane_blocks = hnat.shape[0]
    for r in range(SEGMENTS):
        h = _rms(x_ref[0, r], g_ref[...])
        for c in range(n_lane_blocks):
            hnat[c, r * tm:(r + 1) * tm, :] = h[:, c * LANES:(c + 1) * LANES]

    def perm(m, carry):
        rows = pl.ds(pl.multiple_of(m * SEGMENTS, SEGMENTS), SEGMENTS)
        for c in range(n_lane_blocks):
            hperm[rows, c * LANES:(c + 1) * LANES] = hnat[c, pl.ds(m, SEGMENTS, stride=tm), :]
        return carry

    lax.fori_loop(0, tm, perm, 0)


def _s5_scan(bu, sbuf, st, are_ref, aim_ref, jb, tm):
    half = bu.shape[1] // 2
    base = jb * bu.shape[1]
    step = 4 * LANES
    for c0 in range(0, half, step):
        ar = are_ref[jb, :, c0:c0 + step]
        ai = aim_ref[jb, :, c0:c0 + step]

        def body(m, carry, c0=c0, ar=ar, ai=ai):
            sr, si = carry
            rows = pl.ds(pl.multiple_of(m * SEGMENTS, SEGMENTS), SEGMENTS)
            nr = ar * sr - ai * si + bu[rows, c0:c0 + step]
            ni = ar * si + ai * sr + bu[rows, half + c0:half + c0 + step]
            if sbuf is not None:
                sbuf[rows, c0:c0 + step] = nr
                sbuf[rows, half + c0:half + c0 + step] = ni
            return nr, ni

        sr, si = lax.fori_loop(
            0, tm, body,
            (st[:, base + c0:base + c0 + step], st[:, base + half + c0:base + half + c0 + step]))
        st[:, base + c0:base + c0 + step] = sr
        st[:, base + half + c0:base + half + c0 + step] = si


def _s5_ends_body(x_ref, g_ref, bblk_ref, are_ref, aim_ref, ends_ref, hnat, hperm, bu, st):
    tm = x_ref.shape[2]
    kb = bblk_ref.shape[1]

    @pl.when(pl.program_id(1) == 0)
    def _():
        st[...] = jnp.zeros_like(st)

    _s5_load_h(x_ref, g_ref, hnat, hperm)
    hb = hperm[...].astype(BF16)
    for jb in range(bblk_ref.shape[0]):
        bu[...] = jnp.dot(hb[:, jb * kb:(jb + 1) * kb], bblk_ref[jb], preferred_element_type=F32)
        _s5_scan(bu, None, st, are_ref, aim_ref, jb, tm)

    @pl.when(pl.program_id(1) == pl.num_programs(1) - 1)
    def _():
        ends_ref[0] = st[...]


def _s5_main_body(x_ref, g_ref, bblk_ref, cblk_ref, are_ref, aim_ref, apr_ref, api_ref, d_ref,
                  glu_ref, ends_ref, s0_ref, o_ref, slast_ref, hnat, hperm, bu, sbuf, ybuf, st):
    tm = x_ref.shape[2]
    d = x_ref.shape[3]
    nb, kb, two_half = bblk_ref.shape
    half = two_half // 2

    @pl.when(pl.program_id(1) == 0)
    def _():
        for jb in range(nb):
            re_cols = slice(jb * two_half, jb * two_half + half)
            im_cols = slice(jb * two_half + half, (jb + 1) * two_half)
            pr = apr_ref[jb, 0:1, :]
            pi = api_ref[jb, 0:1, :]
            er = s0_ref[0, :, re_cols]
            ei = s0_ref[0, :, im_cols]
            for r in range(SEGMENTS):
                st[r:r + 1, re_cols] = er
                st[r:r + 1, im_cols] = ei
                nr = pr * er - pi * ei + ends_ref[0, r:r + 1, re_cols]
                ni = pr * ei + pi * er + ends_ref[0, r:r + 1, im_cols]
                er, ei = nr, ni
            slast_ref[0, :, re_cols] = er
            slast_ref[0, :, im_cols] = ei

    _s5_load_h(x_ref, g_ref, hnat, hperm)
    h = hperm[...]
    hb = h.astype(BF16)
    for jb in range(nb):
        bu[...] = jnp.dot(hb[:, jb * kb:(jb + 1) * kb], bblk_ref[jb], preferred_element_type=F32)
        _s5_scan(bu, sbuf, st, are_ref, aim_ref, jb, tm)
        ybuf[:, jb * kb:(jb + 1) * kb] = jnp.dot(sbuf[...].astype(BF16), cblk_ref[jb],
                                                 preferred_element_type=F32)
    y = ybuf[...] + d_ref[...] * h
    z = jnp.dot(y.astype(BF16), glu_ref[...], preferred_element_type=F32)
    o = z[:, :d] * _sigmoid(z[:, d:])
    for c in range(hnat.shape[0]):
        hnat[c] = o[:, c * LANES:(c + 1) * LANES]
    for r in range(SEGMENTS):
        for c in range(hnat.shape[0]):
            lanes = slice(c * LANES, (c + 1) * LANES)
            o_ref[0, r, :, lanes] = x_ref[0, r, :, lanes] + hnat[c, pl.ds(r, tm, stride=SEGMENTS), :]


def _prep_s5(a_re, a_im, b_re, b_im, c_re, c_im, log_dt, seg_len):
    g, p = a_re.shape
    gb = S5_GROUP_BLOCK
    nb = g // gb
    a = lax.complex(a_re.astype(F32), a_im.astype(F32))
    dt = jnp.exp(log_dt.astype(F32))[:, None]
    a_bar = jnp.exp(a * dt)
    b_bar = ((a_bar - 1.0) / a)[..., None] * lax.complex(b_re.astype(F32), b_im.astype(F32))
    a_pow = a_bar
    for _ in range(int(math.log2(seg_len))):
        a_pow = a_pow * a_pow
    eye = jnp.eye(gb, dtype=F32)

    def in_blocks(m):
        return jnp.einsum('bgpi,gh->bgihp', m.reshape(nb, gb, p, SSM_GROUP), eye).reshape(
            nb, gb * SSM_GROUP, gb * p)

    def out_blocks(m):
        return jnp.einsum('bgip,gh->bhpgi', m.reshape(nb, gb, SSM_GROUP, p), eye).reshape(
            nb, gb * p, gb * SSM_GROUP)

    def lanes(v):
        return jnp.broadcast_to(v.reshape(nb, 1, gb * p), (nb, SUBLANES, gb * p))

    bblk = jnp.concatenate([in_blocks(b_bar.real), in_blocks(b_bar.imag)], axis=2).astype(BF16)
    cblk = jnp.concatenate([out_blocks(c_re.astype(F32)), out_blocks(-c_im.astype(F32))],
                           axis=1).astype(BF16)
    return (bblk, cblk, lanes(a_bar.real), lanes(a_bar.imag), lanes(a_pow.real), lanes(a_pow.imag))


def _s5_state_to_lanes(s_re, s_im):
    b, g, p = s_re.shape
    nb = g // S5_GROUP_BLOCK
    both = jnp.stack([s_re.reshape(b, nb, S5_GROUP_BLOCK * p), s_im.reshape(b, nb, S5_GROUP_BLOCK * p)],
                     axis=2)
    return both.reshape(b, 1, 2 * g * p).astype(F32)


def _s5_state_from_lanes(s, g, p):
    b = s.shape[0]
    both = s.reshape(b, g // S5_GROUP_BLOCK, 2, S5_GROUP_BLOCK, p)
    return both[:, :, 0].reshape(b, g, p), both[:, :, 1].reshape(b, g, p)


def _odd_mixer(x, norm_g, s_re, s_im, a_re, a_im, b_re, b_im, c_re, c_im, d_skip, log_dt, glu_w, tm):
    b, s, d = x.shape
    g, p = a_re.shape
    seg_len = s // SEGMENTS
    bblk, cblk, are, aim, apr, api = _prep_s5(a_re, a_im, b_re, b_im, c_re, c_im, log_dt, seg_len)
    nb, kb, two_half = bblk.shape
    n_state = nb * two_half
    xv = x.reshape(b, SEGMENTS, seg_len, d)
    rows = SEGMENTS * tm
    grid = (b, seg_len // tm)
    x_spec = pl.BlockSpec((1, SEGMENTS, tm, d), lambda i, t: (i, 0, t, 0))
    lane_spec = _resident(are.shape)
    g2 = norm_g.reshape(1, d)

    ends = pl.pallas_call(
        _s5_ends_body,
        grid=grid,
        in_specs=[x_spec, _resident((1, d)), _resident(bblk.shape), lane_spec, lane_spec],
        out_specs=pl.BlockSpec((1, SEGMENTS, n_state), lambda i, t: (i, 0, 0)),
        out_shape=jax.ShapeDtypeStruct((b, SEGMENTS, n_state), F32),
        scratch_shapes=[
            pltpu.VMEM((d // LANES, rows, LANES), F32), pltpu.VMEM((rows, d), F32),
            pltpu.VMEM((rows, two_half), F32), pltpu.VMEM((SEGMENTS, n_state), F32),
        ],
        compiler_params=_params("parallel", "arbitrary"),
        name="s5_ends",
    )(xv, g2, bblk, are, aim)

    out, s_last = pl.pallas_call(
        _s5_main_body,
        grid=grid,
        in_specs=[
            x_spec, _resident((1, d)), _resident(bblk.shape), _resident(cblk.shape),
            lane_spec, lane_spec, lane_spec, lane_spec, _resident((1, d)),
            _resident(glu_w.shape),
            pl.BlockSpec((1, SEGMENTS, n_state), lambda i, t: (i, 0, 0)),
            pl.BlockSpec((1, 1, n_state), lambda i, t: (i, 0, 0)),
        ],
        out_specs=[x_spec, pl.BlockSpec((1, 1, n_state), lambda i, t: (i, 0, 0))],
        out_shape=[jax.ShapeDtypeStruct(xv.shape, F32),
                   jax.ShapeDtypeStruct((b, 1, n_state), F32)],
        scratch_shapes=[
            pltpu.VMEM((d // LANES, rows, LANES), F32), pltpu.VMEM((rows, d), F32),
            pltpu.VMEM((rows, two_half), F32), pltpu.VMEM((rows, two_half), F32),
            pltpu.VMEM((rows, d), F32), pltpu.VMEM((SEGMENTS, n_state), F32),
        ],
        compiler_params=_params("parallel", "arbitrary"),
        name="s5_main",
    )(xv, g2, bblk, cblk, are, aim, apr, api, d_skip.reshape(1, d).astype(F32),
      glu_w.astype(BF16), ends, _s5_state_to_lanes(s_re, s_im))
    n_re, n_im = _s5_state_from_lanes(s_last, g, p)
    return out.reshape(b, s, d), n_re, n_im


def _even_mixer(x, p, e, conv0, h0, cache_k, cache_v, tm, prompt):
    b, s, d = x.shape
    n = b * s
    xa, ga, q, k, v = _inproj(x.reshape(n, d), p['mix_norm_l'], p['ab_w_in'][e], p['q_norm'][e],
                              p['k_norm'][e], tm)
    w = xa.shape[1]
    heads = w // HEAD_DIM
    xa, ga, q, k, v = (t.reshape(b, s, w) for t in (xa, ga, q, k, v))
    ya, h_new = _lru(xa, ga, conv0, h0, p['conv_w'][e], p['conv_b'][e], p['lru_wa'][e],
                     p['lru_ba'][e], p['lru_wx'][e], p['lru_bx'][e], p['lru_lambda'][e],
                     min(s, 512))
    bias = _band_bias(p['rel_bias'][e])
    if prompt:
        yb = _attn(q, k, k, v, v, bias, WINDOW, 1, True)
        k_new, v_new = k[:, s - WINDOW:], v[:, s - WINDOW:]
    else:
        wc = cache_k.shape[1]
        yb = _attn(q, cache_k.reshape(b, wc, w), k, cache_v.reshape(b, wc, w), v, bias, s, 0, False)
        k_new, v_new = k, v
    conv_new = jnp.concatenate([conv0.astype(F32), xa], axis=1)[:, -(CONV_WIDTH - 1):]
    y = _outproj(x.reshape(n, d), ya.reshape(n, w), yb.reshape(n, w), p['ab_w_out'][e], tm)
    return (y.reshape(b, s, d), conv_new, h_new,
            k_new.reshape(b, -1, heads, HEAD_DIM), v_new.reshape(b, -1, heads, HEAD_DIM))


def _trunk(x, p, conv_st, lru_st, cache_k, cache_v, ssm_re_st, ssm_im_st, prompt):
    b, s, d = x.shape
    n = b * s
    tm = min(n, 512)
    depth = p['ffn1_norm'].shape[0]
    conv_out, lru_out, k_out, v_out, re_out, im_out = [], [], [], [], [], []
    for l in range(depth):
        x = _ffn(x.reshape(n, d), p['ffn1_norm'][l], *p['ffn1_w'][l], tm).reshape(b, s, d)
        if l % 2 == 0:
            e = l // 2
            w = p['conv_w'].shape[-1]
            if prompt:
                c_prev = jnp.zeros((b, CONV_WIDTH - 1, w), F32)
                h_prev = jnp.zeros((b, w), F32)
                ck = cv = None
            else:
                c_prev, h_prev, ck, cv = conv_st[e], lru_st[e], cache_k[e], cache_v[e]
            pe = dict(p, mix_norm_l=p['mix_norm'][l])
            x, c_new, h_new, k_new, v_new = _even_mixer(x, pe, e, c_prev, h_prev, ck, cv, tm, prompt)
            conv_out.append(c_new)
            lru_out.append(h_new)
            k_out.append(k_new)
            v_out.append(v_new)
        else:
            o = l // 2
            g, st = p['ssm_A_re'].shape[1:]
            if prompt:
                s_re = jnp.zeros((b, g, st), F32)
                s_im = jnp.zeros((b, g, st), F32)
            else:
                s_re, s_im = ssm_re_st[o], ssm_im_st[o]
            x, n_re, n_im = _odd_mixer(
                x, p['mix_norm'][l], s_re, s_im, p['ssm_A_re'][o], p['ssm_A_im'][o], p['ssm_B_re'][o],
                p['ssm_B_im'][o], p['ssm_C_re'][o], p['ssm_C_im'][o], p['ssm_D'][o],
                p['ssm_log_dt'][o], p['glu_w'][o], min(s // SEGMENTS, 64))
            re_out.append(n_re)
            im_out.append(n_im)
        x = _ffn(x.reshape(n, d), p['ffn2_norm'][l], *p['ffn2_w'][l], tm).reshape(b, s, d)
    return (x, jnp.stack(conv_out), jnp.stack(lru_out), jnp.stack(k_out), jnp.stack(v_out),
            jnp.stack(re_out), jnp.stack(im_out))


def kernel(x_prompt, x_sample, state_rglru_conv, state_rglru_h, cache_band_k, cache_band_v,
           state_ssm_re, state_ssm_im, ffn1_norm, ffn1_w_in, ffn1_w_out, mix_norm, ffn2_norm,
           ffn2_w_in, ffn2_w_out, ab_w_in, conv_w, conv_b, lru_wa, lru_ba, lru_wx, lru_bx,
           lru_lambda, q_norm, k_norm, rel_bias, ab_w_out, ssm_A_re, ssm_A_im, ssm_B_re, ssm_B_im,
           ssm_C_re, ssm_C_im, ssm_D, ssm_log_dt, glu_w):
    depth = ffn1_norm.shape[0]
    p = dict(ffn1_norm=ffn1_norm, mix_norm=mix_norm, ffn2_norm=ffn2_norm, ab_w_in=ab_w_in,
             conv_w=conv_w, conv_b=conv_b, lru_wa=lru_wa, lru_ba=lru_ba, lru_wx=lru_wx, lru_bx=lru_bx,
             lru_lambda=lru_lambda, q_norm=q_norm, k_norm=k_norm, rel_bias=rel_bias, ab_w_out=ab_w_out,
             ssm_A_re=ssm_A_re, ssm_A_im=ssm_A_im, ssm_B_re=ssm_B_re, ssm_B_im=ssm_B_im,
             ssm_C_re=ssm_C_re, ssm_C_im=ssm_C_im, ssm_D=ssm_D, ssm_log_dt=ssm_log_dt, glu_w=glu_w)
    p['ffn1_w'] = [_prep_ffn(ffn1_w_in[l], ffn1_w_out[l]) for l in range(depth)]
    p['ffn2_w'] = [_prep_ffn(ffn2_w_in[l], ffn2_w_out[l]) for l in range(depth)]
    y_prompt, p_conv, p_h, p_k, p_v, p_re, p_im = _trunk(
        x_prompt, p, None, None, None, None, None, None, True)
    y_sample, s_conv, s_h, s_k, s_v, s_re, s_im = _trunk(
        x_sample, p, state_rglru_conv, state_rglru_h, cache_band_k, cache_band_v,
        state_ssm_re, state_ssm_im, False)
    return (y_prompt, y_sample, p_conv, p_h, p_k, p_v, p_re, p_im, s_conv, s_h, s_k, s_v, s_re, s_im)
```

```python
import functools
import math

import jax
import jax.numpy as jnp
from jax import lax
from jax.experimental import pallas as pl
from jax.experimental.pallas import tpu as pltpu

F32 = jnp.float32
BF16 = jnp.bfloat16

LANES = 128
SUBLANES = 8
VMEM_LIMIT_BYTES = 56 * 1024 * 1024

EPS = 1e-6
CHUNK = 64
LEFT_CHUNKS = 8
WINDOW = LEFT_CHUNKS * CHUNK
BAND = WINDOW + CHUNK
MAX_REL = 128
HEAD_DIM = 64
CONV_WIDTH = 4
LRU_C = 8.0
SSM_GROUP = 16
SSM_STATE = 64
SEGMENTS = SUBLANES
S5_GROUP_BLOCK = 16
FFN_CHUNK = 256
NEG_INF = -1e30


def _params(*sem):
    return pltpu.CompilerParams(dimension_semantics=sem, vmem_limit_bytes=VMEM_LIMIT_BYTES)


def _resident(shape):
    nd = len(shape)
    return pl.BlockSpec(shape, lambda *_: (0,) * nd, pipeline_mode=pl.Buffered(1))


def _rms(x, g):
    return x * lax.rsqrt(jnp.mean(x * x, axis=-1, keepdims=True) + EPS) * g


def _sigmoid(x):
    return 1.0 / (1.0 + jnp.exp(-x))


def _ffn_body(x_ref, g_ref, win_ref, wout_ref, o_ref, acc_ref):
    n_chunks, _, two_fc = win_ref.shape
    fc = two_fc // 2
    x = x_ref[...]
    h = _rms(x, g_ref[...]).astype(BF16)
    for c in range(n_chunks):
        gu = jnp.dot(h, win_ref[c], preferred_element_type=F32)
        gate, up = gu[:, :fc], gu[:, fc:]
        act = (gate * _sigmoid(gate) * up).astype(BF16)
        y = jnp.dot(act, wout_ref[c], preferred_element_type=F32)
        if c == 0:
            acc_ref[...] = y
        else:
            acc_ref[...] += y
    o_ref[...] = x + 0.5 * acc_ref[...]


def _prep_ffn(w_in, w_out):
    d, two_ff = w_in.shape
    d_ff = two_ff // 2
    nc = d_ff // FFN_CHUNK
    gate = w_in[:, :d_ff].reshape(d, nc, FFN_CHUNK)
    up = w_in[:, d_ff:].reshape(d, nc, FFN_CHUNK)
    w_in_c = jnp.concatenate([gate, up], axis=2).transpose(1, 0, 2).astype(BF16)
    w_out_c = w_out.reshape(nc, FFN_CHUNK, d).astype(BF16)
    return w_in_c, w_out_c


def _ffn(x2d, g, w_in_c, w_out_c, tm):
    n, d = x2d.shape
    return pl.pallas_call(
        _ffn_body,
        grid=(n // tm,),
        in_specs=[
            pl.BlockSpec((tm, d), lambda i: (i, 0)),
            _resident((1, d)),
            _resident(w_in_c.shape),
            _resident(w_out_c.shape),
        ],
        out_specs=pl.BlockSpec((tm, d), lambda i: (i, 0)),
        out_shape=jax.ShapeDtypeStruct((n, d), F32),
        scratch_shapes=[pltpu.VMEM((tm, d), F32)],
        compiler_params=_params("parallel"),
        name="ffn",
    )(x2d, g.reshape(1, d), w_in_c, w_out_c)


def _head_norm(t, gain):
    low = lax.broadcasted_iota(jnp.int32, (t.shape[0], LANES), 1) < HEAD_DIM
    outs = []
    for j in range(t.shape[1] // LANES):
        blk = t[:, j * LANES:(j + 1) * LANES]
        sq = blk * blk
        tot = jnp.sum(sq, axis=-1, keepdims=True)
        lo = jnp.sum(jnp.where(low, sq, 0.0), axis=-1, keepdims=True)
        ms = jnp.where(low, lo, tot - lo) * (1.0 / HEAD_DIM)
        outs.append(blk * lax.rsqrt(ms + EPS) * gain[:, j * LANES:(j + 1) * LANES])
    return jnp.concatenate(outs, axis=-1)


def _inproj_body(x_ref, g_ref, w_ref, qg_ref, kg_ref, xa_ref, ga_ref, q_ref, k_ref, v_ref):
    w = xa_ref.shape[1]
    h = _rms(x_ref[...], g_ref[...]).astype(BF16)
    proj = jnp.dot(h, w_ref[...], preferred_element_type=F32)
    xa_ref[...] = proj[:, 0 * w:1 * w]
    ga_ref[...] = proj[:, 1 * w:2 * w]
    q_ref[...] = _head_norm(proj[:, 2 * w:3 * w], qg_ref[...])
    k_ref[...] = _head_norm(proj[:, 3 * w:4 * w], kg_ref[...])
    v_ref[...] = proj[:, 4 * w:5 * w]


def _inproj(x2d, g, w_in, q_gain, k_gain, tm):
    n, d = x2d.shape
    w = w_in.shape[1] // 5
    heads = w // HEAD_DIM
    row = pl.BlockSpec((tm, w), lambda i: (i, 0))
    return pl.pallas_call(
        _inproj_body,
        grid=(n // tm,),
        in_specs=[
            pl.BlockSpec((tm, d), lambda i: (i, 0)),
            _resident((1, d)),
            _resident(w_in.shape),
            _resident((1, w)),
            _resident((1, w)),
        ],
        out_specs=[row] * 5,
        out_shape=[jax.ShapeDtypeStruct((n, w), F32)] * 5,
        compiler_params=_params("parallel"),
        name="inproj",
    )(x2d, g.reshape(1, d), w_in.astype(BF16),
      jnp.tile(q_gain, heads).reshape(1, w), jnp.tile(k_gain, heads).reshape(1, w))


def _lru_body(xa_ref, ga_ref, c0_ref, h0_ref, cw_ref, cb_ref, wa_ref, ba_ref, wx_ref, bx_ref,
              lam_ref, ya_ref, hlast_ref, ext_ref, hc_ref, a_ref, u_ref):
    t_len = xa_ref.shape[1]
    w = xa_ref.shape[2]

    @pl.when(pl.program_id(1) == 0)
    def _():
        ext_ref[0:SUBLANES, :] = c0_ref[0]
        hc_ref[...] = jnp.broadcast_to(h0_ref[0], hc_ref.shape)

    x = xa_ref[0]
    ext_ref[SUBLANES:SUBLANES + t_len, :] = x
    xc = cb_ref[...] + cw_ref[3:4, :] * x
    for k in range(CONV_WIDTH - 1):
        off = SUBLANES - (CONV_WIDTH - 1) + k
        xc = xc + cw_ref[k:k + 1, :] * ext_ref[off:off + t_len, :]
    ext_ref[0:SUBLANES, :] = x[t_len - SUBLANES:, :]

    xcb = xc.astype(BF16)
    r = _sigmoid(jnp.dot(xcb, wa_ref[...], preferred_element_type=F32) + ba_ref[...])
    i = _sigmoid(jnp.dot(xcb, wx_ref[...], preferred_element_type=F32) + bx_ref[...])
    neg_lam = -lam_ref[...]
    softplus = jnp.maximum(neg_lam, 0.0) + jnp.log1p(jnp.exp(-jnp.abs(neg_lam)))
    log_a = -LRU_C * r * softplus
    a = jnp.exp(log_a)
    a_ref[...] = a
    u_ref[...] = jnp.sqrt(-jnp.tanh(log_a) * (a * a + 1.0)) * (i * xc)

    row = lax.broadcasted_iota(jnp.int32, (SUBLANES, w), 0)

    def block(b, h):
        r0 = pl.multiple_of(b * SUBLANES, SUBLANES)
        a = a_ref[pl.ds(r0, SUBLANES), :]
        u = u_ref[pl.ds(r0, SUBLANES), :]
        for d in (1, 2, 4):
            keep = row >= d
            a_sh = jnp.where(keep, pltpu.roll(a, d, 0), 1.0)
            u_sh = jnp.where(keep, pltpu.roll(u, d, 0), 0.0)
            u = u + a * u_sh
            a = a * a_sh
        hs = a * h + u
        u_ref[pl.ds(r0, SUBLANES), :] = hs
        return jnp.broadcast_to(hs[SUBLANES - 1:SUBLANES, :], (SUBLANES, w))

    h_end = lax.fori_loop(0, t_len // SUBLANES, block, hc_ref[...])
    hc_ref[...] = h_end
    hlast_ref[0] = h_end
    ya_ref[0] = u_ref[...] * jax.nn.gelu(ga_ref[0])


def _block_diag(w):
    nb, n, _ = w.shape
    eye = jnp.eye(nb, dtype=w.dtype)
    return jnp.einsum('hij,hg->higj', w, eye).reshape(nb * n, nb * n)


def _lru(xa, ga, conv0, h0, conv_w, conv_b, wa, ba, wx, bx, lam, t_len):
    b, s, w = xa.shape
    pad = jnp.zeros((b, SUBLANES - (CONV_WIDTH - 1), w), F32)
    c0 = jnp.concatenate([pad, conv0.astype(F32)], axis=1)
    seq = pl.BlockSpec((1, t_len, w), lambda i, t: (i, t, 0))
    vec = _resident((1, w))
    ya, hlast = pl.pallas_call(
        _lru_body,
        grid=(b, s // t_len),
        in_specs=[
            seq, seq,
            pl.BlockSpec((1, SUBLANES, w), lambda i, t: (i, 0, 0)),
            pl.BlockSpec((1, 1, w), lambda i, t: (i, 0, 0)),
            _resident((CONV_WIDTH, w)), vec,
            _resident((w, w)), vec, _resident((w, w)), vec, vec,
        ],
        out_specs=[seq, pl.BlockSpec((1, SUBLANES, w), lambda i, t: (i, 0, 0))],
        out_shape=[jax.ShapeDtypeStruct((b, s, w), F32),
                   jax.ShapeDtypeStruct((b, SUBLANES, w), F32)],
        scratch_shapes=[
            pltpu.VMEM((SUBLANES + t_len, w), F32),
            pltpu.VMEM((SUBLANES, w), F32),
            pltpu.VMEM((t_len, w), F32),
            pltpu.VMEM((t_len, w), F32),
        ],
        compiler_params=_params("parallel", "arbitrary"),
        name="lru",
    )(xa, ga, c0, h0.astype(F32).reshape(b, 1, w), conv_w, conv_b.reshape(1, w),
      _block_diag(wa).astype(BF16), ba.reshape(1, w), _block_diag(wx).astype(BF16),
      bx.reshape(1, w), lam.reshape(1, w))
    return ya, hlast[:, 0, :]


def _attn_body(q_ref, kp_ref, ko_ref, vp_ref, vo_ref, bias_ref, o_ref, kbuf, vbuf, *, mask_first):
    tq = q_ref.shape[1]
    w = q_ref.shape[2]
    kbuf[0:WINDOW, :] = kp_ref[0].astype(BF16)
    kbuf[WINDOW:WINDOW + tq, :] = ko_ref[0].astype(BF16)
    vbuf[0:WINDOW, :] = vp_ref[0].astype(BF16)
    vbuf[WINDOW:WINDOW + tq, :] = vo_ref[0].astype(BF16)
    low = lax.broadcasted_iota(jnp.int32, (CHUNK, LANES), 1) < HEAD_DIM
    col = lax.broadcasted_iota(jnp.int32, (2 * CHUNK, BAND), 1)
    first = pl.program_id(1) == 0
    scale = 1.0 / math.sqrt(HEAD_DIM)

    def chunk(j, carry):
        r0 = pl.multiple_of(j * CHUNK, CHUNK)
        for pr in range(w // LANES):
            lanes = slice(pr * LANES, (pr + 1) * LANES)
            qp = q_ref[0, pl.ds(r0, CHUNK), lanes]
            qs = jnp.concatenate([jnp.where(low, qp, 0.0), jnp.where(low, 0.0, qp)], axis=0)
            kw = kbuf[pl.ds(r0, BAND), lanes]
            s = lax.dot_general(qs.astype(BF16), kw, (((1,), (1,)), ((), ())),
                                preferred_element_type=F32)
            s = s * scale + bias_ref[pr]
            if mask_first:
                s = jnp.where(col >= jnp.where(first, WINDOW - r0, 0), s, NEG_INF)
            p = jnp.exp(s - jnp.max(s, axis=-1, keepdims=True))
            denom = jnp.sum(p, axis=-1, keepdims=True)
            o = jnp.dot(p.astype(BF16), vbuf[pl.ds(r0, BAND), lanes],
                        preferred_element_type=F32) / denom
            o_ref[0, pl.ds(r0, CHUNK), lanes] = jnp.where(low, o[:CHUNK], o[CHUNK:])
        return carry

    lax.fori_loop(0, tq // CHUNK, chunk, 0)


def _band_bias(rel_bias):
    heads = rel_bias.shape[1]
    rel = WINDOW + jnp.arange(CHUNK)[:, None] - jnp.arange(BAND)[None, :]
    tab = rel_bias[jnp.clip(rel, -MAX_REL, MAX_REL) + MAX_REL]
    return jnp.transpose(tab, (2, 0, 1)).reshape(heads // 2, 2 * CHUNK, BAND).astype(F32)


def _attn(q, k_prev, k_own, v_prev, v_own, bias, tq, shift_prev, mask_first):
    b, s, w = q.shape
    own = pl.BlockSpec((1, tq, w), lambda i, t: (i, t, 0))
    prev = pl.BlockSpec((1, WINDOW, w), lambda i, t: (i, jnp.maximum(t - shift_prev, 0), 0))
    return pl.pallas_call(
        functools.partial(_attn_body, mask_first=mask_first),
        grid=(b, s // tq),
        in_specs=[own, prev, own, prev, own, _resident(bias.shape)],
        out_specs=own,
        out_shape=jax.ShapeDtypeStruct((b, s, w), F32),
        scratch_shapes=[pltpu.VMEM((WINDOW + tq, w), BF16), pltpu.VMEM((WINDOW + tq, w), BF16)],
        compiler_params=_params("parallel", "parallel"),
        name="attn",
    )(q, k_prev, k_own, v_prev, v_own, bias)


def _outproj_body(x_ref, ya_ref, yb_ref, wa_ref, wb_ref, o_ref):
    y = jnp.dot(ya_ref[...].astype(BF16), wa_ref[...], preferred_element_type=F32)
    y = y + jnp.dot(yb_ref[...].astype(BF16), wb_ref[...], preferred_element_type=F32)
    o_ref[...] = x_ref[...] + y


def _outproj(x2d, ya, yb, w_out, tm):
    n, d = x2d.shape
    w = ya.shape[1]
    wo = w_out.astype(BF16)
    return pl.pallas_call(
        _outproj_body,
        grid=(n // tm,),
        in_specs=[
            pl.BlockSpec((tm, d), lambda i: (i, 0)),
            pl.BlockSpec((tm, w), lambda i: (i, 0)),
            pl.BlockSpec((tm, w), lambda i: (i, 0)),
            _resident((w, d)), _resident((w, d)),
        ],
        out_specs=pl.BlockSpec((tm, d), lambda i: (i, 0)),
        out_shape=jax.ShapeDtypeStruct((n, d), F32),
        compiler_params=_params("parallel"),
        name="outproj",
    )(x2d, ya, yb, wo[:w], wo[w:])


def _s5_load_h(x_ref, g_ref, hnat, hperm):
    tm = x_ref.shape[2]
    pitch = hnat.shape[1] // SEGMENTS
    n_lane_blocks = hnat.shape[0]
    for r in range(SEGMENTS):
        h = _rms(x_ref[0, r], g_ref[...])
        for c in range(n_lane_blocks):
            hnat[c, r * pitch:r * pitch + tm, :] = h[:, c * LANES:(c + 1) * LANES]
    for m in range(tm):
        for c in range(n_lane_blocks):
            hperm[m * SEGMENTS:(m + 1) * SEGMENTS, c * LANES:(c + 1) * LANES] = (
                hnat[c, pl.ds(m, SEGMENTS, stride=pitch), :])


def _s5_pitch(tm):
    assert tm % SUBLANES == 0
    return tm + SUBLANES // 2


def _s5_scan(bu, sbuf, st, are_ref, aim_ref, jb, tm):
    half = bu.shape[1] // 2
    base = jb * bu.shape[1]
    step = 4 * LANES
    for c0 in range(0, half, step):
        ar = are_ref[jb, :, c0:c0 + step]
        ai = aim_ref[jb, :, c0:c0 + step]
        sr = st[:, base + c0:base + c0 + step]
        si = st[:, base + half + c0:base + half + c0 + step]
        for m in range(tm):
            rows = slice(m * SEGMENTS, (m + 1) * SEGMENTS)
            nr = ar * sr - ai * si + bu[rows, c0:c0 + step]
            ni = ar * si + ai * sr + bu[rows, half + c0:half + c0 + step]
            if sbuf is not None:
                sbuf[rows, c0:c0 + step] = nr
                sbuf[rows, half + c0:half + c0 + step] = ni
            sr, si = nr, ni
        st[:, base + c0:base + c0 + step] = sr
        st[:, base + half + c0:base + half + c0 + step] = si


def _s5_ends_body(x_ref, g_ref, bblk_ref, are_ref, aim_ref, ends_ref, hnat, hperm, bu0, bu1, st):
    tm = x_ref.shape[2]
    nb, kb, _ = bblk_ref.shape
    bus = (bu0, bu1)

    @pl.when(pl.program_id(1) == 0)
    def _():
        st[...] = jnp.zeros_like(st)

    _s5_load_h(x_ref, g_ref, hnat, hperm)
    hb = hperm[...].astype(BF16)

    def b_u(jb):
        return jnp.dot(hb[:, jb * kb:(jb + 1) * kb], bblk_ref[jb], preferred_element_type=F32)

    bus[0][...] = b_u(0)
    for jb in range(nb):
        if jb + 1 < nb:
            bus[(jb + 1) % 2][...] = b_u(jb + 1)
        _s5_scan(bus[jb % 2], None, st, are_ref, aim_ref, jb, tm)

    @pl.when(pl.program_id(1) == pl.num_programs(1) - 1)
    def _():
        ends_ref[0] = st[...]


def _s5_main_body(x_ref, g_ref, bblk_ref, cblk_ref, are_ref, aim_ref, apr_ref, api_ref, d_ref,
                  glu_ref, ends_ref, s0_ref, o_ref, slast_ref, hnat, hperm, bu0, bu1, sb0, sb1,
                  ybuf, st):
    tm = x_ref.shape[2]
    bus = (bu0, bu1)
    sbs = (sb0, sb1)
    d = x_ref.shape[3]
    nb, kb, two_half = bblk_ref.shape
    half = two_half // 2

    @pl.when(pl.program_id(1) == 0)
    def _():
        for jb in range(nb):
            re_cols = slice(jb * two_half, jb * two_half + half)
            im_cols = slice(jb * two_half + half, (jb + 1) * two_half)
            pr = apr_ref[jb, 0:1, :]
            pi = api_ref[jb, 0:1, :]
            er = s0_ref[0, :, re_cols]
            ei = s0_ref[0, :, im_cols]
            for r in range(SEGMENTS):
                st[r:r + 1, re_cols] = er
                st[r:r + 1, im_cols] = ei
                nr = pr * er - pi * ei + ends_ref[0, r:r + 1, re_cols]
                ni = pr * ei + pi * er + ends_ref[0, r:r + 1, im_cols]
                er, ei = nr, ni
            slast_ref[0, :, re_cols] = er
            slast_ref[0, :, im_cols] = ei

    _s5_load_h(x_ref, g_ref, hnat, hperm)
    h = hperm[...]
    hb = h.astype(BF16)

    def b_u(jb):
        return jnp.dot(hb[:, jb * kb:(jb + 1) * kb], bblk_ref[jb], preferred_element_type=F32)

    bus[0][...] = b_u(0)
    for jb in range(nb):
        if jb + 1 < nb:
            bus[(jb + 1) % 2][...] = b_u(jb + 1)
        _s5_scan(bus[jb % 2], sbs[jb % 2], st, are_ref, aim_ref, jb, tm)
        ybuf[:, jb * kb:(jb + 1) * kb] = jnp.dot(sbs[jb % 2][...].astype(BF16), cblk_ref[jb],
                                                 preferred_element_type=F32)
    y = ybuf[...] + d_ref[...] * h
    z = jnp.dot(y.astype(BF16), glu_ref[...], preferred_element_type=F32)
    o = z[:, :d] * _sigmoid(z[:, d:])
    for c in range(hnat.shape[0]):
        hnat[c, 0:SEGMENTS * tm, :] = o[:, c * LANES:(c + 1) * LANES]
    for r in range(SEGMENTS):
        for c in range(hnat.shape[0]):
            lanes = slice(c * LANES, (c + 1) * LANES)
            o_ref[0, r, :, lanes] = x_ref[0, r, :, lanes] + hnat[c, pl.ds(r, tm, stride=SEGMENTS), :]


def _prep_s5(a_re, a_im, b_re, b_im, c_re, c_im, log_dt, seg_len):
    g, p = a_re.shape
    gb = S5_GROUP_BLOCK
    nb = g // gb
    a = lax.complex(a_re.astype(F32), a_im.astype(F32))
    dt = jnp.exp(log_dt.astype(F32))[:, None]
    a_bar = jnp.exp(a * dt)
    b_bar = ((a_bar - 1.0) / a)[..., None] * lax.complex(b_re.astype(F32), b_im.astype(F32))
    a_pow = a_bar
    for _ in range(int(math.log2(seg_len))):
        a_pow = a_pow * a_pow
    eye = jnp.eye(gb, dtype=F32)

    def in_blocks(m):
        return jnp.einsum('bgpi,gh->bgihp', m.reshape(nb, gb, p, SSM_GROUP), eye).reshape(
            nb, gb * SSM_GROUP, gb * p)

    def out_blocks(m):
        return jnp.einsum('bgip,gh->bhpgi', m.reshape(nb, gb, SSM_GROUP, p), eye).reshape(
            nb, gb * p, gb * SSM_GROUP)

    def lanes(v):
        return jnp.broadcast_to(v.reshape(nb, 1, gb * p), (nb, SUBLANES, gb * p))

    bblk = jnp.concatenate([in_blocks(b_bar.real), in_blocks(b_bar.imag)], axis=2).astype(BF16)
    cblk = jnp.concatenate([out_blocks(c_re.astype(F32)), out_blocks(-c_im.astype(F32))],
                           axis=1).astype(BF16)
    return (bblk, cblk, lanes(a_bar.real), lanes(a_bar.imag), lanes(a_pow.real), lanes(a_pow.imag))


def _s5_state_to_lanes(s_re, s_im):
    b, g, p = s_re.shape
    nb = g // S5_GROUP_BLOCK
    both = jnp.stack([s_re.reshape(b, nb, S5_GROUP_BLOCK * p), s_im.reshape(b, nb, S5_GROUP_BLOCK * p)],
                     axis=2)
    return both.reshape(b, 1, 2 * g * p).astype(F32)


def _s5_state_from_lanes(s, g, p):
    b = s.shape[0]
    both = s.reshape(b, g // S5_GROUP_BLOCK, 2, S5_GROUP_BLOCK, p)
    return both[:, :, 0].reshape(b, g, p), both[:, :, 1].reshape(b, g, p)


def _odd_mixer(x, norm_g, s_re, s_im, a_re, a_im, b_re, b_im, c_re, c_im, d_skip, log_dt, glu_w, tm):
    b, s, d = x.shape
    g, p = a_re.shape
    seg_len = s // SEGMENTS
    bblk, cblk, are, aim, apr, api = _prep_s5(a_re, a_im, b_re, b_im, c_re, c_im, log_dt, seg_len)
    nb, kb, two_half = bblk.shape
    n_state = nb * two_half
    xv = x.reshape(b, SEGMENTS, seg_len, d)
    rows = SEGMENTS * tm
    grid = (b, seg_len // tm)
    x_spec = pl.BlockSpec((1, SEGMENTS, tm, d), lambda i, t: (i, 0, t, 0))
    lane_spec = _resident(are.shape)
    g2 = norm_g.reshape(1, d)

    ends = pl.pallas_call(
        _s5_ends_body,
        grid=grid,
        in_specs=[x_spec, _resident((1, d)), _resident(bblk.shape), lane_spec, lane_spec],
        out_specs=pl.BlockSpec((1, SEGMENTS, n_state), lambda i, t: (i, 0, 0)),
        out_shape=jax.ShapeDtypeStruct((b, SEGMENTS, n_state), F32),
        scratch_shapes=[
            pltpu.VMEM((d // LANES, SEGMENTS * _s5_pitch(tm), LANES), F32), pltpu.VMEM((rows, d), F32),
            pltpu.VMEM((rows, two_half), F32), pltpu.VMEM((rows, two_half), F32),
            pltpu.VMEM((SEGMENTS, n_state), F32),
        ],
        compiler_params=_params("parallel", "arbitrary"),
        name="s5_ends",
    )(xv, g2, bblk, are, aim)

    out, s_last = pl.pallas_call(
        _s5_main_body,
        grid=grid,
        in_specs=[
            x_spec, _resident((1, d)), _resident(bblk.shape), _resident(cblk.shape),
            lane_spec, lane_spec, lane_spec, lane_spec, _resident((1, d)),
            _resident(glu_w.shape),
            pl.BlockSpec((1, SEGMENTS, n_state), lambda i, t: (i, 0, 0)),
            pl.BlockSpec((1, 1, n_state), lambda i, t: (i, 0, 0)),
        ],
        out_specs=[x_spec, pl.BlockSpec((1, 1, n_state), lambda i, t: (i, 0, 0))],
        out_shape=[jax.ShapeDtypeStruct(xv.shape, F32),
                   jax.ShapeDtypeStruct((b, 1, n_state), F32)],
        scratch_shapes=[
            pltpu.VMEM((d // LANES, SEGMENTS * _s5_pitch(tm), LANES), F32), pltpu.VMEM((rows, d), F32),
            pltpu.VMEM((rows, two_half), F32), pltpu.VMEM((rows, two_half), F32),
            pltpu.VMEM((rows, two_half), F32), pltpu.VMEM((rows, two_half), F32),
            pltpu.VMEM((rows, d), F32), pltpu.VMEM((SEGMENTS, n_state), F32),
        ],
        compiler_params=_params("parallel", "arbitrary"),
        name="s5_main",
    )(xv, g2, bblk, cblk, are, aim, apr, api, d_skip.reshape(1, d).astype(F32),
      glu_w.astype(BF16), ends, _s5_state_to_lanes(s_re, s_im))
    n_re, n_im = _s5_state_from_lanes(s_last, g, p)
    return out.reshape(b, s, d), n_re, n_im


def _even_mixer(x, p, e, conv0, h0, cache_k, cache_v, tm, prompt):
    b, s, d = x.shape
    n = b * s
    xa, ga, q, k, v = _inproj(x.reshape(n, d), p['mix_norm_l'], p['ab_w_in'][e], p['q_norm'][e],
                              p['k_norm'][e], tm)
    w = xa.shape[1]
    heads = w // HEAD_DIM
    xa, ga, q, k, v = (t.reshape(b, s, w) for t in (xa, ga, q, k, v))
    ya, h_new = _lru(xa, ga, conv0, h0, p['conv_w'][e], p['conv_b'][e], p['lru_wa'][e],
                     p['lru_ba'][e], p['lru_wx'][e], p['lru_bx'][e], p['lru_lambda'][e],
                     min(s, 512))
    bias = _band_bias(p['rel_bias'][e])
    if prompt:
        yb = _attn(q, k, k, v, v, bias, WINDOW, 1, True)
        k_new, v_new = k[:, s - WINDOW:], v[:, s - WINDOW:]
    else:
        wc = cache_k.shape[1]
        yb = _attn(q, cache_k.reshape(b, wc, w), k, cache_v.reshape(b, wc, w), v, bias, s, 0, False)
        k_new, v_new = k, v
    conv_new = jnp.concatenate([conv0.astype(F32), xa], axis=1)[:, -(CONV_WIDTH - 1):]
    y = _outproj(x.reshape(n, d), ya.reshape(n, w), yb.reshape(n, w), p['ab_w_out'][e], tm)
    return (y.reshape(b, s, d), conv_new, h_new,
            k_new.reshape(b, -1, heads, HEAD_DIM), v_new.reshape(b, -1, heads, HEAD_DIM))


def _trunk(x, p, conv_st, lru_st, cache_k, cache_v, ssm_re_st, ssm_im_st, prompt):
    b, s, d = x.shape
    n = b * s
    tm = min(n, 512)
    depth = p['ffn1_norm'].shape[0]
    conv_out, lru_out, k_out, v_out, re_out, im_out = [], [], [], [], [], []
    for l in range(depth):
        x = _ffn(x.reshape(n, d), p['ffn1_norm'][l], *p['ffn1_w'][l], tm).reshape(b, s, d)
        if l % 2 == 0:
            e = l // 2
            w = p['conv_w'].shape[-1]
            if prompt:
                c_prev = jnp.zeros((b, CONV_WIDTH - 1, w), F32)
                h_prev = jnp.zeros((b, w), F32)
                ck = cv = None
            else:
                c_prev, h_prev, ck, cv = conv_st[e], lru_st[e], cache_k[e], cache_v[e]
            pe = dict(p, mix_norm_l=p['mix_norm'][l])
            x, c_new, h_new, k_new, v_new = _even_mixer(x, pe, e, c_prev, h_prev, ck, cv, tm, prompt)
            conv_out.append(c_new)
            lru_out.append(h_new)
            k_out.append(k_new)
            v_out.append(v_new)
        else:
            o = l // 2
            g, st = p['ssm_A_re'].shape[1:]
            if prompt:
                s_re = jnp.zeros((b, g, st), F32)
                s_im = jnp.zeros((b, g, st), F32)
            else:
                s_re, s_im = ssm_re_st[o], ssm_im_st[o]
            x, n_re, n_im = _odd_mixer(
                x, p['mix_norm'][l], s_re, s_im, p['ssm_A_re'][o], p['ssm_A_im'][o], p['ssm_B_re'][o],
                p['ssm_B_im'][o], p['ssm_C_re'][o], p['ssm_C_im'][o], p['ssm_D'][o],
                p['ssm_log_dt'][o], p['glu_w'][o], min(s // SEGMENTS, 64))
            re_out.append(n_re)
            im_out.append(n_im)
        x = _ffn(x.reshape(n, d), p['ffn2_norm'][l], *p['ffn2_w'][l], tm).reshape(b, s, d)
    return (x, jnp.stack(conv_out), jnp.stack(lru_out), jnp.stack(k_out), jnp.stack(v_out),
            jnp.stack(re_out), jnp.stack(im_out))


def kernel(x_prompt, x_sample, state_rglru_conv, state_rglru_h, cache_band_k, cache_band_v,
           state_ssm_re, state_ssm_im, ffn1_norm, ffn1_w_in, ffn1_w_out, mix_norm, ffn2_norm,
           ffn2_w_in, ffn2_w_out, ab_w_in, conv_w, conv_b, lru_wa, lru_ba, lru_wx, lru_bx,
           lru_lambda, q_norm, k_norm, rel_bias, ab_w_out, ssm_A_re, ssm_A_im, ssm_B_re, ssm_B_im,
           ssm_C_re, ssm_C_im, ssm_D, ssm_log_dt, glu_w):
    depth = ffn1_norm.shape[0]
    p = dict(ffn1_norm=ffn1_norm, mix_norm=mix_norm, ffn2_norm=ffn2_norm, ab_w_in=ab_w_in,
             conv_w=conv_w, conv_b=conv_b, lru_wa=lru_wa, lru_ba=lru_ba, lru_wx=lru_wx, lru_bx=lru_bx,
             lru_lambda=lru_lambda, q_norm=q_norm, k_norm=k_norm, rel_bias=rel_bias, ab_w_out=ab_w_out,
             ssm_A_re=ssm_A_re, ssm_A_im=ssm_A_im, ssm_B_re=ssm_B_re, ssm_B_im=ssm_B_im,
             ssm_C_re=ssm_C_re, ssm_C_im=ssm_C_im, ssm_D=ssm_D, ssm_log_dt=ssm_log_dt, glu_w=glu_w)
    p['ffn1_w'] = [_prep_ffn(ffn1_w_in[l], ffn1_w_out[l]) for l in range(depth)]
    p['ffn2_w'] = [_prep_ffn(ffn2_w_in[l], ffn2_w_out[l]) for l in range(depth)]
    y_prompt, p_conv, p_h, p_k, p_v, p_re, p_im = _trunk(
        x_prompt, p, None, None, None, None, None, None, True)
    y_sample, s_conv, s_h, s_k, s_v, s_re, s_im = _trunk(
        x_sample, p, state_rglru_conv, state_rglru_h, cache_band_k, cache_band_v,
        state_ssm_re, state_ssm_im, False)
    return (y_prompt, y_sample, p_conv, p_h, p_k, p_v, p_re, p_im, s_conv, s_h, s_k, s_v, s_re, s_im)
```

```python
import functools
import math

import jax
import jax.numpy as jnp
from jax import lax
from jax.experimental import pallas as pl
from jax.experimental.pallas import tpu as pltpu

F32 = jnp.float32
BF16 = jnp.bfloat16

LANES = 128
SUBLANES = 8
VMEM_LIMIT_BYTES = 56 * 1024 * 1024

EPS = 1e-6
CHUNK = 64
LEFT_CHUNKS = 8
WINDOW = LEFT_CHUNKS * CHUNK
BAND = WINDOW + CHUNK
MAX_REL = 128
PAST_LEN = 4096
HEAD_DIM = 64
CONV_WIDTH = 4
LRU_C = 8.0
SSM_GROUP = 16
SSM_STATE = 64
SEGMENTS = SUBLANES
S5_GROUP_BLOCK = 16
FFN_CHUNK = 256
ATTN_BLOCK = 256
NEG_INF = -1e30


def _params(*sem):
    return pltpu.CompilerParams(dimension_semantics=sem, vmem_limit_bytes=VMEM_LIMIT_BYTES)


def _resident(shape):
    nd = len(shape)
    return pl.BlockSpec(shape, lambda *_: (0,) * nd, pipeline_mode=pl.Buffered(1))


def _rms(x, g):
    return x * lax.rsqrt(jnp.mean(x * x, axis=-1, keepdims=True) + EPS) * g


def _sigmoid(x):
    return 1.0 / (1.0 + jnp.exp(-x))


def _ffn_body(x_ref, g_ref, win_ref, wout_ref, o_ref, acc_ref):
    d_ff = wout_ref.shape[0]
    x = x_ref[...]
    h = _rms(x, g_ref[...]).astype(BF16)
    for c in range(d_ff // FFN_CHUNK):
        cols = slice(c * FFN_CHUNK, (c + 1) * FFN_CHUNK)
        up_cols = slice(d_ff + c * FFN_CHUNK, d_ff + (c + 1) * FFN_CHUNK)
        gate = jnp.dot(h, win_ref[:, cols], preferred_element_type=F32)
        up = jnp.dot(h, win_ref[:, up_cols], preferred_element_type=F32)
        act = (gate * _sigmoid(gate) * up).astype(BF16)
        y = jnp.dot(act, wout_ref[cols, :], preferred_element_type=F32)
        if c == 0:
            acc_ref[...] = y
        else:
            acc_ref[...] += y
    o_ref[...] = x + 0.5 * acc_ref[...]


def _prep_ffn(w_in, w_out):
    assert w_out.shape[0] % FFN_CHUNK == 0
    return w_in.astype(BF16), w_out.astype(BF16)


def _ffn(x2d, g, w_in_c, w_out_c, tm):
    n, d = x2d.shape
    return pl.pallas_call(
        _ffn_body,
        grid=(n // tm,),
        in_specs=[
            pl.BlockSpec((tm, d), lambda i: (i, 0)),
            _resident((1, d)),
            _resident(w_in_c.shape),
            _resident(w_out_c.shape),
        ],
        out_specs=pl.BlockSpec((tm, d), lambda i: (i, 0)),
        out_shape=jax.ShapeDtypeStruct((n, d), F32),
        scratch_shapes=[pltpu.VMEM((tm, d), F32)],
        compiler_params=_params("parallel"),
        name="ffn",
    )(x2d, g.reshape(1, d), w_in_c, w_out_c)


def _head_norm(t, gain):
    low = lax.broadcasted_iota(jnp.int32, (t.shape[0], LANES), 1) < HEAD_DIM
    outs = []
    for j in range(t.shape[1] // LANES):
        blk = t[:, j * LANES:(j + 1) * LANES]
        sq = blk * blk
        tot = jnp.sum(sq, axis=-1, keepdims=True)
        lo = jnp.sum(jnp.where(low, sq, 0.0), axis=-1, keepdims=True)
        ms = jnp.where(low, lo, tot - lo) * (1.0 / HEAD_DIM)
        outs.append(blk * lax.rsqrt(ms + EPS) * gain[:, j * LANES:(j + 1) * LANES])
    return jnp.concatenate(outs, axis=-1)


def _inproj_body(x_ref, g_ref, w_ref, qg_ref, kg_ref, xa_ref, ga_ref, q_ref, k_ref, v_ref,
                 kt_ref, vt_ref):
    w = xa_ref.shape[2]
    h = _rms(x_ref[0], g_ref[...]).astype(BF16)
    proj = jnp.dot(h, w_ref[...], preferred_element_type=F32)
    xa_ref[0] = proj[:, 0 * w:1 * w]
    ga_ref[0] = proj[:, 1 * w:2 * w]
    q = _head_norm(proj[:, 2 * w:3 * w], qg_ref[...]) * (1.0 / math.sqrt(HEAD_DIM))
    k = _head_norm(proj[:, 3 * w:4 * w], kg_ref[...])
    v = proj[:, 4 * w:5 * w]
    kt_ref[0] = k
    vt_ref[0] = v
    for pr in range(w // LANES):
        lanes = slice(pr * LANES, (pr + 1) * LANES)
        q_ref[0, pr] = q[:, lanes].astype(BF16)
        k_ref[0, pr] = k[:, lanes].astype(BF16)
        v_ref[0, pr] = v[:, lanes].astype(BF16)


def _inproj(x, g, w_in, q_gain, k_gain, tm):
    b, s, d = x.shape
    w = w_in.shape[1] // 5
    heads = w // HEAD_DIM
    npair = w // LANES
    row = pl.BlockSpec((1, tm, w), lambda i, t: (i, t, 0))
    pair = pl.BlockSpec((1, npair, tm, LANES), lambda i, t: (i, 0, t, 0))
    tail = pl.BlockSpec((1, tm, w), lambda i, t: (i, 0, 0))
    return pl.pallas_call(
        _inproj_body,
        grid=(b, s // tm),
        in_specs=[
            pl.BlockSpec((1, tm, d), lambda i, t: (i, t, 0)),
            _resident((1, d)),
            _resident(w_in.shape),
            _resident((1, w)),
            _resident((1, w)),
        ],
        out_specs=[row, row, pair, pair, pair, tail, tail],
        out_shape=[jax.ShapeDtypeStruct((b, s, w), F32)] * 2
        + [jax.ShapeDtypeStruct((b, npair, s, LANES), BF16)] * 3
        + [jax.ShapeDtypeStruct((b, tm, w), F32)] * 2,
        compiler_params=_params("parallel", "arbitrary"),
        name="inproj",
    )(x, g.reshape(1, d), w_in.astype(BF16),
      jnp.tile(q_gain, heads).reshape(1, w), jnp.tile(k_gain, heads).reshape(1, w))


def _lru_body(xa_ref, ga_ref, c0_ref, h0_ref, cw_ref, cb_ref, wa_ref, ba_ref, wx_ref, bx_ref,
              lam_ref, ya_ref, hlast_ref, ext_ref, hc_ref, a_ref, u_ref):
    t_len = xa_ref.shape[1]
    w = xa_ref.shape[2]

    @pl.when(pl.program_id(1) == 0)
    def _():
        ext_ref[0:SUBLANES, :] = c0_ref[0]
        hc_ref[...] = jnp.broadcast_to(h0_ref[0], hc_ref.shape)

    x = xa_ref[0]
    ext_ref[SUBLANES:SUBLANES + t_len, :] = x
    xc = cb_ref[...] + cw_ref[3:4, :] * x
    for k in range(CONV_WIDTH - 1):
        off = SUBLANES - (CONV_WIDTH - 1) + k
        xc = xc + cw_ref[k:k + 1, :] * ext_ref[off:off + t_len, :]
    ext_ref[0:SUBLANES, :] = x[t_len - SUBLANES:, :]

    xcb = xc.astype(BF16)
    r = _sigmoid(jnp.dot(xcb, wa_ref[...], preferred_element_type=F32) + ba_ref[...])
    i = _sigmoid(jnp.dot(xcb, wx_ref[...], preferred_element_type=F32) + bx_ref[...])
    neg_lam = -lam_ref[...]
    softplus = jnp.maximum(neg_lam, 0.0) + jnp.log1p(jnp.exp(-jnp.abs(neg_lam)))
    log_a = -LRU_C * r * softplus
    a = jnp.exp(log_a)
    a_ref[...] = a
    u_ref[...] = jnp.sqrt(-jnp.tanh(log_a) * (a * a + 1.0)) * (i * xc)

    row = lax.broadcasted_iota(jnp.int32, (SUBLANES, w), 0)

    def block(b, h):
        r0 = pl.multiple_of(b * SUBLANES, SUBLANES)
        a = a_ref[pl.ds(r0, SUBLANES), :]
        u = u_ref[pl.ds(r0, SUBLANES), :]
        for d in (1, 2, 4):
            keep = row >= d
            a_sh = jnp.where(keep, pltpu.roll(a, d, 0), 1.0)
            u_sh = jnp.where(keep, pltpu.roll(u, d, 0), 0.0)
            u = u + a * u_sh
            a = a * a_sh
        hs = a * h + u
        u_ref[pl.ds(r0, SUBLANES), :] = hs
        return jnp.broadcast_to(hs[SUBLANES - 1:SUBLANES, :], (SUBLANES, w))

    h_end = lax.fori_loop(0, t_len // SUBLANES, block, hc_ref[...])
    hc_ref[...] = h_end
    hlast_ref[0] = h_end
    ya_ref[0] = u_ref[...] * jax.nn.gelu(ga_ref[0])


def _block_diag(w):
    nb, n, _ = w.shape
    eye = jnp.eye(nb, dtype=w.dtype)
    return jnp.einsum('hij,hg->higj', w, eye).reshape(nb * n, nb * n)


def _lru(xa, ga, conv0, h0, conv_w, conv_b, wa, ba, wx, bx, lam, t_len):
    b, s, w = xa.shape
    pad = jnp.zeros((b, SUBLANES - (CONV_WIDTH - 1), w), F32)
    c0 = jnp.concatenate([pad, conv0.astype(F32)], axis=1)
    seq = pl.BlockSpec((1, t_len, w), lambda i, t: (i, t, 0))
    vec = _resident((1, w))
    ya, hlast = pl.pallas_call(
        _lru_body,
        grid=(b, s // t_len),
        in_specs=[
            seq, seq,
            pl.BlockSpec((1, SUBLANES, w), lambda i, t: (i, 0, 0)),
            pl.BlockSpec((1, 1, w), lambda i, t: (i, 0, 0)),
            _resident((CONV_WIDTH, w)), vec,
            _resident((w, w)), vec, _resident((w, w)), vec, vec,
        ],
        out_specs=[seq, pl.BlockSpec((1, SUBLANES, w), lambda i, t: (i, 0, 0))],
        out_shape=[jax.ShapeDtypeStruct((b, s, w), F32),
                   jax.ShapeDtypeStruct((b, SUBLANES, w), F32)],
        scratch_shapes=[
            pltpu.VMEM((SUBLANES + t_len, w), F32),
            pltpu.VMEM((SUBLANES, w), F32),
            pltpu.VMEM((t_len, w), F32),
            pltpu.VMEM((t_len, w), F32),
        ],
        compiler_params=_params("parallel", "arbitrary"),
        name="lru",
    )(xa, ga, c0, h0.astype(F32).reshape(b, 1, w), conv_w, conv_b.reshape(1, w),
      _block_diag(wa).astype(BF16), ba.reshape(1, w), _block_diag(wx).astype(BF16),
      bx.reshape(1, w), lam.reshape(1, w))
    return ya, hlast[:, 0, :]


def _attn_body(*refs, n_prev):
    q_ref = refs[0]
    k_parts = refs[1:2 + n_prev]
    v_parts = refs[2 + n_prev:3 + 2 * n_prev]
    bias_ref, o_ref = refs[3 + 2 * n_prev:]
    npair, nq = q_ref.shape[1], q_ref.shape[2]
    low = lax.broadcasted_iota(jnp.int32, (nq, LANES), 1) < HEAD_DIM
    for pr in range(npair):
        qp = q_ref[0, pr]
        zero = jnp.zeros_like(qp)
        qs = jnp.concatenate([jnp.where(low, qp, zero), jnp.where(low, zero, qp)], axis=0)
        kw = jnp.concatenate([part[0, pr] for part in k_parts], axis=0)
        vw = jnp.concatenate([part[0, pr] for part in v_parts], axis=0)
        s = lax.dot_general(qs, kw, (((1,), (1,)), ((), ())), preferred_element_type=F32)
        s = s + bias_ref[0, pr]
        p = jnp.exp(s - jnp.max(s, axis=-1, keepdims=True))
        denom = jnp.sum(p, axis=-1, keepdims=True)
        o = jnp.dot(p.astype(BF16), vw, preferred_element_type=F32) / denom
        o_ref[0, pr] = jnp.where(low, o[:nq], o[nq:]).astype(o_ref.dtype)


def _band_bias(rel_bias, nq, invalid_cols):
    heads = rel_bias.shape[1]
    win = WINDOW + nq
    period = win + nq
    n_hi = WINDOW - MAX_REL
    n_lo = max(win - (WINDOW + MAX_REL + 1), 0)
    tab = rel_bias.astype(F32)
    mid = jnp.flip(tab, axis=0)[:win - n_hi - n_lo]
    vec = jnp.concatenate([jnp.broadcast_to(tab[-1:], (n_hi, heads)), mid,
                           jnp.broadcast_to(tab[:1], (n_lo, heads)),
                           jnp.broadcast_to(tab[-1:], (nq, heads))], axis=0)
    toep = jnp.tile(vec.T, (1, nq))[:, :nq * (period - 1)].reshape(heads, nq, period - 1)[:, :, :win]
    qc = jnp.arange(nq)[:, None] // CHUNK
    kc = jnp.arange(win)[None, :] // CHUNK
    band = (kc >= qc) & (kc <= qc + LEFT_CHUNKS)
    col = jnp.arange(win)[None, :]
    out = [jnp.where(band & (col >= c), toep, NEG_INF) for c in invalid_cols]
    return jnp.stack(out).reshape(len(invalid_cols), heads // 2, 2 * nq, win)


def _attn(q, k_own, v_own, k_prev, v_prev, bias, nq, bias_map):
    b, npair, s, _ = q.shape

    def spec(rows, index_map):
        return pl.BlockSpec((1, npair, rows, LANES), index_map)

    own = spec(nq, lambda i, t: (i, 0, t, 0))
    return pl.pallas_call(
        functools.partial(_attn_body, n_prev=len(k_prev)),
        grid=(b, s // nq),
        in_specs=([own] + [spec(r, m) for _, r, m in k_prev] + [own]
                  + [spec(r, m) for _, r, m in v_prev] + [own]
                  + [pl.BlockSpec((1,) + bias.shape[1:], bias_map)]),
        out_specs=own,
        out_shape=jax.ShapeDtypeStruct((b, npair, s, LANES), BF16),
        compiler_params=_params("parallel", "arbitrary"),
        name="attn",
    )(q, *[a for a, _, _ in k_prev], k_own, *[a for a, _, _ in v_prev], v_own, bias)


def _outproj_body(x_ref, ya_ref, yb_ref, w_ref, o_ref):
    w = ya_ref.shape[2]
    yb = jnp.concatenate([yb_ref[0, pr] for pr in range(yb_ref.shape[1])], axis=-1)
    y = jnp.dot(ya_ref[0].astype(BF16), w_ref[:w, :], preferred_element_type=F32)
    y = y + jnp.dot(yb, w_ref[w:, :], preferred_element_type=F32)
    o_ref[0] = x_ref[0] + y


def _outproj(x, ya, yb, w_out, tm):
    b, s, d = x.shape
    w = ya.shape[2]
    npair = yb.shape[1]
    return pl.pallas_call(
        _outproj_body,
        grid=(b, s // tm),
        in_specs=[
            pl.BlockSpec((1, tm, d), lambda i, t: (i, t, 0)),
            pl.BlockSpec((1, tm, w), lambda i, t: (i, t, 0)),
            pl.BlockSpec((1, npair, tm, LANES), lambda i, t: (i, 0, t, 0)),
            _resident(w_out.shape),
        ],
        out_specs=pl.BlockSpec((1, tm, d), lambda i, t: (i, t, 0)),
        out_shape=jax.ShapeDtypeStruct((b, s, d), F32),
        compiler_params=_params("parallel", "parallel"),
        name="outproj",
    )(x, ya, yb, w_out.astype(BF16))


def _s5_load_h(x_ref, g_ref, hnat, hperm):
    tm = x_ref.shape[2]
    pitch = hnat.shape[1] // SEGMENTS
    n_lane_blocks = hnat.shape[0]
    for r in range(SEGMENTS):
        h = _rms(x_ref[0, r], g_ref[...])
        for c in range(n_lane_blocks):
            hnat[c, r * pitch:r * pitch + tm, :] = h[:, c * LANES:(c + 1) * LANES]
    for m in range(tm):
        for c in range(n_lane_blocks):
            hperm[m * SEGMENTS:(m + 1) * SEGMENTS, c * LANES:(c + 1) * LANES] = (
                hnat[c, pl.ds(m, SEGMENTS, stride=pitch), :])


def _s5_pitch(tm):
    assert tm % SUBLANES == 0
    return tm + SUBLANES // 2


def _s5_scan(bu, sbuf, st, are_ref, aim_ref, jb, tm):
    half = bu.shape[1] // 2
    base = jb * bu.shape[1]
    step = 4 * LANES
    for c0 in range(0, half, step):
        ar = are_ref[jb, :, c0:c0 + step]
        ai = aim_ref[jb, :, c0:c0 + step]
        sr = st[:, base + c0:base + c0 + step]
        si = st[:, base + half + c0:base + half + c0 + step]
        for m in range(tm):
            rows = slice(m * SEGMENTS, (m + 1) * SEGMENTS)
            nr = ar * sr - ai * si + bu[rows, c0:c0 + step]
            ni = ar * si + ai * sr + bu[rows, half + c0:half + c0 + step]
            if sbuf is not None:
                sbuf[rows, c0:c0 + step] = nr
                sbuf[rows, half + c0:half + c0 + step] = ni
            sr, si = nr, ni
        st[:, base + c0:base + c0 + step] = sr
        st[:, base + half + c0:base + half + c0 + step] = si


def _s5_ends_body(x_ref, g_ref, bblk_ref, are_ref, aim_ref, ends_ref, hnat, hperm, bu0, bu1, st):
    tm = x_ref.shape[2]
    nb, kb, _ = bblk_ref.shape
    bus = (bu0, bu1)

    @pl.when(pl.program_id(1) == 0)
    def _():
        st[...] = jnp.zeros_like(st)

    _s5_load_h(x_ref, g_ref, hnat, hperm)
    hb = hperm[...].astype(BF16)

    def b_u(jb):
        return jnp.dot(hb[:, jb * kb:(jb + 1) * kb], bblk_ref[jb], preferred_element_type=F32)

    bus[0][...] = b_u(0)
    for jb in range(nb):
        if jb + 1 < nb:
            bus[(jb + 1) % 2][...] = b_u(jb + 1)
        _s5_scan(bus[jb % 2], None, st, are_ref, aim_ref, jb, tm)

    @pl.when(pl.program_id(1) == pl.num_programs(1) - 1)
    def _():
        ends_ref[0] = st[...]


def _s5_main_body(x_ref, g_ref, bblk_ref, cblk_ref, are_ref, aim_ref, apr_ref, api_ref, d_ref,
                  glu_ref, ends_ref, s0_ref, o_ref, slast_ref, hnat, hperm, bu0, bu1, sb0, sb1,
                  ybuf, st):
    tm = x_ref.shape[2]
    bus = (bu0, bu1)
    sbs = (sb0, sb1)
    d = x_ref.shape[3]
    nb, kb, two_half = bblk_ref.shape
    half = two_half // 2

    @pl.when(pl.program_id(1) == 0)
    def _():
        for jb in range(nb):
            re_cols = slice(jb * two_half, jb * two_half + half)
            im_cols = slice(jb * two_half + half, (jb + 1) * two_half)
            pr = apr_ref[jb, 0:1, :]
            pi = api_ref[jb, 0:1, :]
            er = s0_ref[0, :, re_cols]
            ei = s0_ref[0, :, im_cols]
            for r in range(SEGMENTS):
                st[r:r + 1, re_cols] = er
                st[r:r + 1, im_cols] = ei
                nr = pr * er - pi * ei + ends_ref[0, r:r + 1, re_cols]
                ni = pr * ei + pi * er + ends_ref[0, r:r + 1, im_cols]
                er, ei = nr, ni
            slast_ref[0, :, re_cols] = er
            slast_ref[0, :, im_cols] = ei

    _s5_load_h(x_ref, g_ref, hnat, hperm)
    h = hperm[...]
    hb = h.astype(BF16)

    def b_u(jb):
        return jnp.dot(hb[:, jb * kb:(jb + 1) * kb], bblk_ref[jb], preferred_element_type=F32)

    bus[0][...] = b_u(0)
    for jb in range(nb):
        if jb + 1 < nb:
            bus[(jb + 1) % 2][...] = b_u(jb + 1)
        _s5_scan(bus[jb % 2], sbs[jb % 2], st, are_ref, aim_ref, jb, tm)
        ybuf[:, jb * kb:(jb + 1) * kb] = jnp.dot(sbs[jb % 2][...].astype(BF16), cblk_ref[jb],
                                                 preferred_element_type=F32)
    y = ybuf[...] + d_ref[...] * h
    z = jnp.dot(y.astype(BF16), glu_ref[...], preferred_element_type=F32)
    o = z[:, :d] * _sigmoid(z[:, d:])
    for c in range(hnat.shape[0]):
        hnat[c, 0:SEGMENTS * tm, :] = o[:, c * LANES:(c + 1) * LANES]
    for r in range(SEGMENTS):
        for c in range(hnat.shape[0]):
            lanes = slice(c * LANES, (c + 1) * LANES)
            o_ref[0, r, :, lanes] = x_ref[0, r, :, lanes] + hnat[c, pl.ds(r, tm, stride=SEGMENTS), :]


def _prep_s5(a_re, a_im, b_re, b_im, c_re, c_im, log_dt, seg_len):
    g, p = a_re.shape
    gb = S5_GROUP_BLOCK
    nb = g // gb
    a = lax.complex(a_re.astype(F32), a_im.astype(F32))
    dt = jnp.exp(log_dt.astype(F32))[:, None]
    a_bar = jnp.exp(a * dt)
    b_bar = ((a_bar - 1.0) / a)[..., None] * lax.complex(b_re.astype(F32), b_im.astype(F32))
    a_pow = a_bar
    for _ in range(int(math.log2(seg_len))):
        a_pow = a_pow * a_pow
    eye = jnp.eye(gb, dtype=F32)

    def in_blocks(m):
        return jnp.einsum('bgpi,gh->bgihp', m.reshape(nb, gb, p, SSM_GROUP), eye).reshape(
            nb, gb * SSM_GROUP, gb * p)

    def out_blocks(m):
        return jnp.einsum('bgip,gh->bhpgi', m.reshape(nb, gb, SSM_GROUP, p), eye).reshape(
            nb, gb * p, gb * SSM_GROUP)

    def lanes(v):
        return jnp.broadcast_to(v.reshape(nb, 1, gb * p), (nb, SUBLANES, gb * p))

    bblk = jnp.concatenate([in_blocks(b_bar.real), in_blocks(b_bar.imag)], axis=2).astype(BF16)
    cblk = jnp.concatenate([out_blocks(c_re.astype(F32)), out_blocks(-c_im.astype(F32))],
                           axis=1).astype(BF16)
    return (bblk, cblk, lanes(a_bar.real), lanes(a_bar.imag), lanes(a_pow.real), lanes(a_pow.imag))


def _s5_state_to_lanes(s_re, s_im):
    b, g, p = s_re.shape
    nb = g // S5_GROUP_BLOCK
    both = jnp.stack([s_re.reshape(b, nb, S5_GROUP_BLOCK * p), s_im.reshape(b, nb, S5_GROUP_BLOCK * p)],
                     axis=2)
    return both.reshape(b, 1, 2 * g * p).astype(F32)


def _s5_state_from_lanes(s, g, p):
    b = s.shape[0]
    both = s.reshape(b, g // S5_GROUP_BLOCK, 2, S5_GROUP_BLOCK, p)
    return both[:, :, 0].reshape(b, g, p), both[:, :, 1].reshape(b, g, p)


def _odd_mixer(x, norm_g, s_re, s_im, a_re, a_im, b_re, b_im, c_re, c_im, d_skip, log_dt, glu_w, tm):
    b, s, d = x.shape
    g, p = a_re.shape
    seg_len = s // SEGMENTS
    bblk, cblk, are, aim, apr, api = _prep_s5(a_re, a_im, b_re, b_im, c_re, c_im, log_dt, seg_len)
    nb, kb, two_half = bblk.shape
    n_state = nb * two_half
    xv = x.reshape(b, SEGMENTS, seg_len, d)
    rows = SEGMENTS * tm
    grid = (b, seg_len // tm)
    x_spec = pl.BlockSpec((1, SEGMENTS, tm, d), lambda i, t: (i, 0, t, 0))
    lane_spec = _resident(are.shape)
    g2 = norm_g.reshape(1, d)

    ends = pl.pallas_call(
        _s5_ends_body,
        grid=grid,
        in_specs=[x_spec, _resident((1, d)), _resident(bblk.shape), lane_spec, lane_spec],
        out_specs=pl.BlockSpec((1, SEGMENTS, n_state), lambda i, t: (i, 0, 0)),
        out_shape=jax.ShapeDtypeStruct((b, SEGMENTS, n_state), F32),
        scratch_shapes=[
            pltpu.VMEM((d // LANES, SEGMENTS * _s5_pitch(tm), LANES), F32), pltpu.VMEM((rows, d), F32),
            pltpu.VMEM((rows, two_half), F32), pltpu.VMEM((rows, two_half), F32),
            pltpu.VMEM((SEGMENTS, n_state), F32),
        ],
        compiler_params=_params("parallel", "arbitrary"),
        name="s5_ends",
    )(xv, g2, bblk, are, aim)

    out, s_last = pl.pallas_call(
        _s5_main_body,
        grid=grid,
        in_specs=[
            x_spec, _resident((1, d)), _resident(bblk.shape), _resident(cblk.shape),
            lane_spec, lane_spec, lane_spec, lane_spec, _resident((1, d)),
            _resident(glu_w.shape),
            pl.BlockSpec((1, SEGMENTS, n_state), lambda i, t: (i, 0, 0)),
            pl.BlockSpec((1, 1, n_state), lambda i, t: (i, 0, 0)),
        ],
        out_specs=[x_spec, pl.BlockSpec((1, 1, n_state), lambda i, t: (i, 0, 0))],
        out_shape=[jax.ShapeDtypeStruct(xv.shape, F32),
                   jax.ShapeDtypeStruct((b, 1, n_state), F32)],
        scratch_shapes=[
            pltpu.VMEM((d // LANES, SEGMENTS * _s5_pitch(tm), LANES), F32), pltpu.VMEM((rows, d), F32),
            pltpu.VMEM((rows, two_half), F32), pltpu.VMEM((rows, two_half), F32),
            pltpu.VMEM((rows, two_half), F32), pltpu.VMEM((rows, two_half), F32),
            pltpu.VMEM((rows, d), F32), pltpu.VMEM((SEGMENTS, n_state), F32),
        ],
        compiler_params=_params("parallel", "arbitrary"),
        name="s5_main",
    )(xv, g2, bblk, cblk, are, aim, apr, api, d_skip.reshape(1, d).astype(F32),
      glu_w.astype(BF16), ends, _s5_state_to_lanes(s_re, s_im))
    n_re, n_im = _s5_state_from_lanes(s_last, g, p)
    return out.reshape(b, s, d), n_re, n_im


def _to_pairs(t):
    b, rows, heads, hd = t.shape
    return t.reshape(b, rows, heads // 2, 2 * hd).transpose(0, 2, 1, 3).astype(BF16)


def _even_mixer(x, p, e, conv0, h0, cache_k, cache_v, prompt):
    b, s, d = x.shape
    tm = min(s, WINDOW)
    xa, ga, q, k, v, k_tail, v_tail = _inproj(x, p['mix_norm_l'], p['ab_w_in'][e], p['q_norm'][e],
                                              p['k_norm'][e], tm)
    w = xa.shape[2]
    heads = w // HEAD_DIM
    ya, h_new = _lru(xa, ga, conv0, h0, p['conv_w'][e], p['conv_b'][e], p['lru_wa'][e],
                     p['lru_ba'][e], p['lru_wx'][e], p['lru_bx'][e], p['lru_lambda'][e], tm)
    if prompt:
        nq = ATTN_BLOCK
        n_parts = WINDOW // nq
        bias = _band_bias(p['rel_bias'][e], nq, [WINDOW - nq * v for v in range(n_parts + 1)])
        maps = [functools.partial(lambda i, t, back: (i, 0, jnp.maximum(t - back, 0), 0), back=back)
                for back in range(n_parts, 0, -1)]
        yb = _attn(q, k, v, [(k, nq, m) for m in maps], [(v, nq, m) for m in maps], bias, nq,
                   lambda i, t: (jnp.minimum(t, n_parts), 0, 0, 0))
    else:
        wc = cache_k.shape[1]
        assert wc == WINDOW and s == CHUNK and PAST_LEN % CHUNK == 0
        bias = _band_bias(p['rel_bias'][e], s, [0])
        first = lambda i, t: (i, 0, 0, 0)
        yb = _attn(q, k, v, [(_to_pairs(cache_k), wc, first)], [(_to_pairs(cache_v), wc, first)],
                   bias, s, lambda i, t: (0, 0, 0, 0))
    conv_new = jnp.concatenate([conv0.astype(F32), xa], axis=1)[:, -(CONV_WIDTH - 1):]
    y = _outproj(x, ya, yb, p['ab_w_out'][e], tm)
    return (y, conv_new, h_new, k_tail.reshape(b, -1, heads, HEAD_DIM),
            v_tail.reshape(b, -1, heads, HEAD_DIM))


def _trunk(x, p, conv_st, lru_st, cache_k, cache_v, ssm_re_st, ssm_im_st, prompt):
    b, s, d = x.shape
    n = b * s
    tm = min(n, 512)
    depth = p['ffn1_norm'].shape[0]
    conv_out, lru_out, k_out, v_out, re_out, im_out = [], [], [], [], [], []
    for l in range(depth):
        x = _ffn(x.reshape(n, d), p['ffn1_norm'][l], *p['ffn1_w'][l], tm).reshape(b, s, d)
        if l % 2 == 0:
            e = l // 2
            w = p['conv_w'].shape[-1]
            if prompt:
                c_prev = jnp.zeros((b, CONV_WIDTH - 1, w), F32)
                h_prev = jnp.zeros((b, w), F32)
                ck = cv = None
            else:
                c_prev, h_prev, ck, cv = conv_st[e], lru_st[e], cache_k[e], cache_v[e]
            pe = dict(p, mix_norm_l=p['mix_norm'][l])
            x, c_new, h_new, k_new, v_new = _even_mixer(x, pe, e, c_prev, h_prev, ck, cv, prompt)
            conv_out.append(c_new)
            lru_out.append(h_new)
            k_out.append(k_new)
            v_out.append(v_new)
        else:
            o = l // 2
            g, st = p['ssm_A_re'].shape[1:]
            if prompt:
                s_re = jnp.zeros((b, g, st), F32)
                s_im = jnp.zeros((b, g, st), F32)
            else:
                s_re, s_im = ssm_re_st[o], ssm_im_st[o]
            x, n_re, n_im = _odd_mixer(
                x, p['mix_norm'][l], s_re, s_im, p['ssm_A_re'][o], p['ssm_A_im'][o], p['ssm_B_re'][o],
                p['ssm_B_im'][o], p['ssm_C_re'][o], p['ssm_C_im'][o], p['ssm_D'][o],
                p['ssm_log_dt'][o], p['glu_w'][o], min(s // SEGMENTS, 64))
            re_out.append(n_re)
            im_out.append(n_im)
        x = _ffn(x.reshape(n, d), p['ffn2_norm'][l], *p['ffn2_w'][l], tm).reshape(b, s, d)
    return (x, jnp.stack(conv_out), jnp.stack(lru_out), jnp.stack(k_out), jnp.stack(v_out),
            jnp.stack(re_out), jnp.stack(im_out))


def kernel(x_prompt, x_sample, state_rglru_conv, state_rglru_h, cache_band_k, cache_band_v,
           state_ssm_re, state_ssm_im, ffn1_norm, ffn1_w_in, ffn1_w_out, mix_norm, ffn2_norm,
           ffn2_w_in, ffn2_w_out, ab_w_in, conv_w, conv_b, lru_wa, lru_ba, lru_wx, lru_bx,
           lru_lambda, q_norm, k_norm, rel_bias, ab_w_out, ssm_A_re, ssm_A_im, ssm_B_re, ssm_B_im,
           ssm_C_re, ssm_C_im, ssm_D, ssm_log_dt, glu_w):
    depth = ffn1_norm.shape[0]
    p = dict(ffn1_norm=ffn1_norm, mix_norm=mix_norm, ffn2_norm=ffn2_norm, ab_w_in=ab_w_in,
             conv_w=conv_w, conv_b=conv_b, lru_wa=lru_wa, lru_ba=lru_ba, lru_wx=lru_wx, lru_bx=lru_bx,
             lru_lambda=lru_lambda, q_norm=q_norm, k_norm=k_norm, rel_bias=rel_bias, ab_w_out=ab_w_out,
             ssm_A_re=ssm_A_re, ssm_A_im=ssm_A_im, ssm_B_re=ssm_B_re, ssm_B_im=ssm_B_im,
             ssm_C_re=ssm_C_re, ssm_C_im=ssm_C_im, ssm_D=ssm_D, ssm_log_dt=ssm_log_dt, glu_w=glu_w)
    p['ffn1_w'] = [_prep_ffn(ffn1_w_in[l], ffn1_w_out[l]) for l in range(depth)]
    p['ffn2_w'] = [_prep_ffn(ffn2_w_in[l], ffn2_w_out[l]) for l in range(depth)]
    y_prompt, p_conv, p_h, p_k, p_v, p_re, p_im = _trunk(
        x_prompt, p, None, None, None, None, None, None, True)
    y_sample, s_conv, s_h, s_k, s_v, s_re, s_im = _trunk(
        x_sample, p, state_rglru_conv, state_rglru_h, cache_band_k, cache_band_v,
        state_ssm_re, state_ssm_im, False)
    return (y_prompt, y_sample, p_conv, p_h, p_k, p_v, p_re, p_im, s_conv, s_h, s_k, s_v, s_re, s_im)
```

```python
import functools
import math

import jax
import jax.numpy as jnp
from jax import lax
from jax.experimental import pallas as pl
from jax.experimental.pallas import tpu as pltpu

F32 = jnp.float32
BF16 = jnp.bfloat16

LANES = 128
SUBLANES = 8
VMEM_LIMIT_BYTES = 56 * 1024 * 1024

EPS = 1e-6
CHUNK = 64
LEFT_CHUNKS = 8
WINDOW = LEFT_CHUNKS * CHUNK
BAND = WINDOW + CHUNK
MAX_REL = 128
PAST_LEN = 4096
HEAD_DIM = 64
CONV_WIDTH = 4
LRU_C = 8.0
SSM_GROUP = 16
SSM_STATE = 64
SEGMENTS = SUBLANES
S5_GROUP_BLOCK = 16
FFN_CHUNK = 256
ATTN_BLOCK = 256
FFN_ROWS = 512
NEG_INF = -1e30


def _params(*sem):
    return pltpu.CompilerParams(dimension_semantics=sem, vmem_limit_bytes=VMEM_LIMIT_BYTES)


def _resident(shape):
    nd = len(shape)
    return pl.BlockSpec(shape, lambda *_: (0,) * nd, pipeline_mode=pl.Buffered(1))


def _rms(x, g):
    return x * lax.rsqrt(jnp.mean(x * x, axis=-1, keepdims=True) + EPS) * g


def _sigmoid(x):
    return 1.0 / (1.0 + jnp.exp(-x))


def _ffn_body(x_ref, g_ref, win_ref, wout_ref, o_ref, acc_ref):
    d_ff = wout_ref.shape[0]
    x = x_ref[...]
    h = _rms(x, g_ref[...]).astype(BF16)
    for c in range(d_ff // FFN_CHUNK):
        cols = slice(c * FFN_CHUNK, (c + 1) * FFN_CHUNK)
        up_cols = slice(d_ff + c * FFN_CHUNK, d_ff + (c + 1) * FFN_CHUNK)
        gate = jnp.dot(h, win_ref[:, cols], preferred_element_type=F32)
        up = jnp.dot(h, win_ref[:, up_cols], preferred_element_type=F32)
        act = (gate * _sigmoid(gate) * up).astype(BF16)
        y = jnp.dot(act, wout_ref[cols, :], preferred_element_type=F32)
        if c == 0:
            acc_ref[...] = y
        else:
            acc_ref[...] += y
    o_ref[...] = x + 0.5 * acc_ref[...]


def _ffn(x2d, g, w_in_all, w_out_all, layer, tm):
    n, d = x2d.shape
    _, d_ff, _ = w_out_all.shape
    assert d_ff % FFN_CHUNK == 0 and n % tm == 0
    return pl.pallas_call(
        _ffn_body,
        grid=(n // tm,),
        in_specs=[
            pl.BlockSpec((tm, d), lambda i: (i, 0)),
            _resident((1, d)),
            pl.BlockSpec((None, d, 2 * d_ff), lambda i: (layer, 0, 0), pipeline_mode=pl.Buffered(1)),
            pl.BlockSpec((None, d_ff, d), lambda i: (layer, 0, 0), pipeline_mode=pl.Buffered(1)),
        ],
        out_specs=pl.BlockSpec((tm, d), lambda i: (i, 0)),
        out_shape=jax.ShapeDtypeStruct((n, d), F32),
        scratch_shapes=[pltpu.VMEM((tm, d), F32)],
        compiler_params=_params("parallel"),
        name="ffn",
    )(x2d, g.reshape(1, d), w_in_all, w_out_all)


def _head_norm(t, gain):
    low = lax.broadcasted_iota(jnp.int32, (t.shape[0], LANES), 1) < HEAD_DIM
    outs = []
    for j in range(t.shape[1] // LANES):
        blk = t[:, j * LANES:(j + 1) * LANES]
        sq = blk * blk
        tot = jnp.sum(sq, axis=-1, keepdims=True)
        lo = jnp.sum(jnp.where(low, sq, 0.0), axis=-1, keepdims=True)
        ms = jnp.where(low, lo, tot - lo) * (1.0 / HEAD_DIM)
        outs.append(blk * lax.rsqrt(ms + EPS) * gain[:, j * LANES:(j + 1) * LANES])
    return jnp.concatenate(outs, axis=-1)


def _block_diag(w):
    nb, n, _ = w.shape
    eye = jnp.eye(nb, dtype=w.dtype)
    return jnp.einsum('hij,hg->higj', w, eye).reshape(nb * n, nb * n)


def _rg_lru(xa, ga, c0_ref, h0_ref, cw_ref, cb_ref, wa_ref, ba_ref, wx_ref, bx_ref, lam_ref,
            ya_ref, hlast_ref, ctail_ref, ext_ref, hc_ref, a_ref, u_ref):
    t_len, w = xa.shape

    @pl.when(pl.program_id(1) == 0)
    def _():
        ext_ref[0:SUBLANES, :] = c0_ref[0]
        hc_ref[...] = jnp.broadcast_to(h0_ref[0], hc_ref.shape)

    ext_ref[SUBLANES:SUBLANES + t_len, :] = xa
    xc = cb_ref[...] + cw_ref[3:4, :] * xa
    for k in range(CONV_WIDTH - 1):
        off = SUBLANES - (CONV_WIDTH - 1) + k
        xc = xc + cw_ref[k:k + 1, :] * ext_ref[off:off + t_len, :]
    tail = xa[t_len - SUBLANES:, :]
    ext_ref[0:SUBLANES, :] = tail
    ctail_ref[0] = tail

    xcb = xc.astype(BF16)
    r = _sigmoid(jnp.dot(xcb, wa_ref[...], preferred_element_type=F32) + ba_ref[...])
    i = _sigmoid(jnp.dot(xcb, wx_ref[...], preferred_element_type=F32) + bx_ref[...])
    neg_lam = -lam_ref[...]
    softplus = jnp.maximum(neg_lam, 0.0) + jnp.log1p(jnp.exp(-jnp.abs(neg_lam)))
    log_a = -LRU_C * r * softplus
    a = jnp.exp(log_a)
    a_ref[...] = a
    u_ref[...] = jnp.sqrt(-jnp.tanh(log_a) * (a * a + 1.0)) * (i * xc)

    row = lax.broadcasted_iota(jnp.int32, (SUBLANES, w), 0)
    h = hc_ref[...]
    for blk in range(t_len // SUBLANES):
        rows = slice(blk * SUBLANES, (blk + 1) * SUBLANES)
        a = a_ref[rows, :]
        u = u_ref[rows, :]
        for d in (1, 2, 4):
            keep = row >= d
            a_sh = jnp.where(keep, pltpu.roll(a, d, 0), 1.0)
            u_sh = jnp.where(keep, pltpu.roll(u, d, 0), 0.0)
            u = u + a * u_sh
            a = a * a_sh
        hs = a * h + u
        ya_ref[0, rows, :] = (hs * jax.nn.gelu(ga[rows, :])).astype(ya_ref.dtype)
        h = jnp.broadcast_to(hs[SUBLANES - 1:SUBLANES, :], (SUBLANES, w))
    hc_ref[...] = h
    hlast_ref[0] = h


def _inproj_body(x_ref, g_ref, w_ref, qg_ref, kg_ref, c0_ref, h0_ref, cw_ref, cb_ref, wa_ref,
                 ba_ref, wx_ref, bx_ref, lam_ref, ya_ref, q_ref, k_ref, v_ref, kt_ref, vt_ref,
                 hlast_ref, ctail_ref, ext_ref, hc_ref, a_ref, u_ref):
    w = kt_ref.shape[2]
    h = _rms(x_ref[0], g_ref[...]).astype(BF16)
    pa = jnp.dot(h, w_ref[:, :2 * w], preferred_element_type=F32)
    _rg_lru(pa[:, :w], pa[:, w:], c0_ref, h0_ref, cw_ref, cb_ref, wa_ref, ba_ref, wx_ref, bx_ref,
            lam_ref, ya_ref, hlast_ref, ctail_ref, ext_ref, hc_ref, a_ref, u_ref)
    pb = jnp.dot(h, w_ref[:, 2 * w:], preferred_element_type=F32)
    q = _head_norm(pb[:, 0 * w:1 * w], qg_ref[...]) * (1.0 / math.sqrt(HEAD_DIM))
    k = _head_norm(pb[:, 1 * w:2 * w], kg_ref[...])
    v = pb[:, 2 * w:3 * w]
    kt_ref[0] = k
    vt_ref[0] = v
    for pr in range(w // LANES):
        lanes = slice(pr * LANES, (pr + 1) * LANES)
        q_ref[0, pr] = q[:, lanes].astype(BF16)
        k_ref[0, pr] = k[:, lanes].astype(BF16)
        v_ref[0, pr] = v[:, lanes].astype(BF16)


def _inproj(x, g, w_in, q_gain, k_gain, conv0, h0, conv_w, conv_b, wa, ba, wx, bx, lam, tm):
    b, s, d = x.shape
    w = w_in.shape[1] // 5
    heads = w // HEAD_DIM
    npair = w // LANES
    assert s % tm == 0 and tm % SUBLANES == 0
    pad = jnp.zeros((b, SUBLANES - (CONV_WIDTH - 1), w), F32)
    c0 = jnp.concatenate([pad, conv0.astype(F32)], axis=1)
    row = pl.BlockSpec((1, tm, w), lambda i, t: (i, t, 0))
    pair = pl.BlockSpec((1, npair, tm, LANES), lambda i, t: (i, 0, t, 0))
    tail = pl.BlockSpec((1, tm, w), lambda i, t: (i, 0, 0))
    state = pl.BlockSpec((1, SUBLANES, w), lambda i, t: (i, 0, 0))
    vec = _resident((1, w))
    return pl.pallas_call(
        _inproj_body,
        grid=(b, s // tm),
        in_specs=[
            pl.BlockSpec((1, tm, d), lambda i, t: (i, t, 0)),
            _resident((1, d)), _resident(w_in.shape), vec, vec,
            state, pl.BlockSpec((1, 1, w), lambda i, t: (i, 0, 0)),
            _resident((CONV_WIDTH, w)), vec, _resident((w, w)), vec, _resident((w, w)), vec, vec,
        ],
        out_specs=[row, pair, pair, pair, tail, tail, state, state],
        out_shape=[jax.ShapeDtypeStruct((b, s, w), BF16)]
        + [jax.ShapeDtypeStruct((b, npair, s, LANES), BF16)] * 3
        + [jax.ShapeDtypeStruct((b, tm, w), F32)] * 2
        + [jax.ShapeDtypeStruct((b, SUBLANES, w), F32)] * 2,
        scratch_shapes=[
            pltpu.VMEM((SUBLANES + tm, w), F32),
            pltpu.VMEM((SUBLANES, w), F32),
            pltpu.VMEM((tm, w), F32),
            pltpu.VMEM((tm, w), F32),
        ],
        compiler_params=_params("parallel", "arbitrary"),
        name="inproj",
    )(x, g.reshape(1, d), w_in.astype(BF16),
      jnp.tile(q_gain, heads).reshape(1, w), jnp.tile(k_gain, heads).reshape(1, w),
      c0, h0.astype(F32).reshape(b, 1, w), conv_w, conv_b.reshape(1, w),
      _block_diag(wa).astype(BF16), ba.reshape(1, w), _block_diag(wx).astype(BF16),
      bx.reshape(1, w), lam.reshape(1, w))


def _attn_body(*refs, n_prev):
    q_ref = refs[0]
    k_parts = refs[1:2 + n_prev]
    v_parts = refs[2 + n_prev:3 + 2 * n_prev]
    bias_ref, o_ref = refs[3 + 2 * n_prev:]
    npair, nq = q_ref.shape[1], q_ref.shape[2]
    low = lax.broadcasted_iota(jnp.int32, (nq, LANES), 1) < HEAD_DIM
    for pr in range(npair):
        qp = q_ref[0, pr]
        zero = jnp.zeros_like(qp)
        qs = jnp.concatenate([jnp.where(low, qp, zero), jnp.where(low, zero, qp)], axis=0)
        kw = jnp.concatenate([part[0, pr] for part in k_parts], axis=0)
        vw = jnp.concatenate([part[0, pr] for part in v_parts], axis=0)
        s = lax.dot_general(qs, kw, (((1,), (1,)), ((), ())), preferred_element_type=F32)
        s = s + bias_ref[0, pr]
        p = jnp.exp(s - jnp.max(s, axis=-1, keepdims=True))
        denom = jnp.sum(p, axis=-1, keepdims=True)
        o = jnp.dot(p.astype(BF16), vw, preferred_element_type=F32) / denom
        o_ref[0, pr] = jnp.where(low, o[:nq], o[nq:]).astype(o_ref.dtype)


def _band_bias(rel_bias, nq, invalid_cols):
    heads = rel_bias.shape[1]
    win = WINDOW + nq
    period = win + nq
    n_hi = WINDOW - MAX_REL
    n_lo = max(win - (WINDOW + MAX_REL + 1), 0)
    tab = rel_bias.astype(F32)
    mid = jnp.flip(tab, axis=0)[:win - n_hi - n_lo]
    vec = jnp.concatenate([jnp.broadcast_to(tab[-1:], (n_hi, heads)), mid,
                           jnp.broadcast_to(tab[:1], (n_lo, heads)),
                           jnp.broadcast_to(tab[-1:], (nq, heads))], axis=0)
    toep = jnp.tile(vec.T, (1, nq))[:, :nq * (period - 1)].reshape(heads, nq, period - 1)[:, :, :win]
    qc = jnp.arange(nq)[:, None] // CHUNK
    kc = jnp.arange(win)[None, :] // CHUNK
    band = (kc >= qc) & (kc <= qc + LEFT_CHUNKS)
    col = jnp.arange(win)[None, :]
    out = [jnp.where(band & (col >= c), toep, NEG_INF) for c in invalid_cols]
    return jnp.stack(out).reshape(len(invalid_cols), heads // 2, 2 * nq, win)


def _attn(q, k_own, v_own, k_prev, v_prev, bias, nq, bias_map):
    b, npair, s, _ = q.shape

    def spec(rows, index_map):
        return pl.BlockSpec((1, npair, rows, LANES), index_map)

    own = spec(nq, lambda i, t: (i, 0, t, 0))
    return pl.pallas_call(
        functools.partial(_attn_body, n_prev=len(k_prev)),
        grid=(b, s // nq),
        in_specs=([own] + [spec(r, m) for _, r, m in k_prev] + [own]
                  + [spec(r, m) for _, r, m in v_prev] + [own]
                  + [pl.BlockSpec((1,) + bias.shape[1:], bias_map)]),
        out_specs=own,
        out_shape=jax.ShapeDtypeStruct((b, npair, s, LANES), BF16),
        compiler_params=_params("parallel", "arbitrary"),
        name="attn",
    )(q, *[a for a, _, _ in k_prev], k_own, *[a for a, _, _ in v_prev], v_own, bias)


def _outproj_body(x_ref, ya_ref, yb_ref, w_ref, o_ref):
    w = ya_ref.shape[2]
    yb = jnp.concatenate([yb_ref[0, pr] for pr in range(yb_ref.shape[1])], axis=-1)
    y = jnp.dot(ya_ref[0], w_ref[:w, :], preferred_element_type=F32)
    y = y + jnp.dot(yb, w_ref[w:, :], preferred_element_type=F32)
    o_ref[0] = x_ref[0] + y


def _outproj(x, ya, yb, w_out, tm):
    b, s, d = x.shape
    w = ya.shape[2]
    npair = yb.shape[1]
    return pl.pallas_call(
        _outproj_body,
        grid=(b, s // tm),
        in_specs=[
            pl.BlockSpec((1, tm, d), lambda i, t: (i, t, 0)),
            pl.BlockSpec((1, tm, w), lambda i, t: (i, t, 0)),
            pl.BlockSpec((1, npair, tm, LANES), lambda i, t: (i, 0, t, 0)),
            _resident(w_out.shape),
        ],
        out_specs=pl.BlockSpec((1, tm, d), lambda i, t: (i, t, 0)),
        out_shape=jax.ShapeDtypeStruct((b, s, d), F32),
        compiler_params=_params("parallel", "parallel"),
        name="outproj",
    )(x, ya, yb, w_out.astype(BF16))


def _s5_load_h(x_ref, g_ref, hnat, hperm):
    tm = x_ref.shape[2]
    pitch = hnat.shape[1] // SEGMENTS
    n_lane_blocks = hnat.shape[0]
    for r in range(SEGMENTS):
        h = _rms(x_ref[0, r], g_ref[...])
        for c in range(n_lane_blocks):
            hnat[c, r * pitch:r * pitch + tm, :] = h[:, c * LANES:(c + 1) * LANES]
    for m in range(tm):
        for c in range(n_lane_blocks):
            hperm[m * SEGMENTS:(m + 1) * SEGMENTS, c * LANES:(c + 1) * LANES] = (
                hnat[c, pl.ds(m, SEGMENTS, stride=pitch), :])


def _s5_pitch(tm):
    assert tm % SUBLANES == 0
    return tm + SUBLANES // 2


def _s5_scan(bu, sbuf, st, are_ref, aim_ref, jb, tm):
    half = bu.shape[1] // 2
    base = jb * bu.shape[1]
    step = 4 * LANES
    for c0 in range(0, half, step):
        ar = are_ref[jb, :, c0:c0 + step]
        ai = aim_ref[jb, :, c0:c0 + step]
        sr = st[:, base + c0:base + c0 + step]
        si = st[:, base + half + c0:base + half + c0 + step]
        for m in range(tm):
            rows = slice(m * SEGMENTS, (m + 1) * SEGMENTS)
            nr = ar * sr - ai * si + bu[rows, c0:c0 + step]
            ni = ar * si + ai * sr + bu[rows, half + c0:half + c0 + step]
            if sbuf is not None:
                sbuf[rows, c0:c0 + step] = nr
                sbuf[rows, half + c0:half + c0 + step] = ni
            sr, si = nr, ni
        st[:, base + c0:base + c0 + step] = sr
        st[:, base + half + c0:base + half + c0 + step] = si


def _s5_ends_body(x_ref, g_ref, bblk_ref, are_ref, aim_ref, ends_ref, hnat, hperm, bu0, bu1, st):
    tm = x_ref.shape[2]
    nb, kb, _ = bblk_ref.shape
    bus = (bu0, bu1)

    @pl.when(pl.program_id(1) == 0)
    def _():
        st[...] = jnp.zeros_like(st)

    _s5_load_h(x_ref, g_ref, hnat, hperm)
    hb = hperm[...].astype(BF16)

    def b_u(jb):
        return jnp.dot(hb[:, jb * kb:(jb + 1) * kb], bblk_ref[jb], preferred_element_type=F32)

    bus[0][...] = b_u(0)
    for jb in range(nb):
        if jb + 1 < nb:
            bus[(jb + 1) % 2][...] = b_u(jb + 1)
        _s5_scan(bus[jb % 2], None, st, are_ref, aim_ref, jb, tm)

    @pl.when(pl.program_id(1) == pl.num_programs(1) - 1)
    def _():
        ends_ref[0] = st[...]


def _s5_main_body(x_ref, g_ref, bblk_ref, cblk_ref, are_ref, aim_ref, apr_ref, api_ref, d_ref,
                  glu_ref, ends_ref, s0_ref, o_ref, slast_ref, hnat, hperm, bu0, bu1, sb0, sb1,
                  ybuf, st):
    tm = x_ref.shape[2]
    bus = (bu0, bu1)
    sbs = (sb0, sb1)
    d = x_ref.shape[3]
    nb, kb, two_half = bblk_ref.shape
    half = two_half // 2

    @pl.when(pl.program_id(1) == 0)
    def _():
        for jb in range(nb):
            re_cols = slice(jb * two_half, jb * two_half + half)
            im_cols = slice(jb * two_half + half, (jb + 1) * two_half)
            pr = apr_ref[jb, 0:1, :]
            pi = api_ref[jb, 0:1, :]
            er = s0_ref[0, :, re_cols]
            ei = s0_ref[0, :, im_cols]
            for r in range(SEGMENTS):
                st[r:r + 1, re_cols] = er
                st[r:r + 1, im_cols] = ei
                nr = pr * er - pi * ei + ends_ref[0, r:r + 1, re_cols]
                ni = pr * ei + pi * er + ends_ref[0, r:r + 1, im_cols]
                er, ei = nr, ni
            slast_ref[0, :, re_cols] = er
            slast_ref[0, :, im_cols] = ei

    _s5_load_h(x_ref, g_ref, hnat, hperm)
    h = hperm[...]
    hb = h.astype(BF16)

    def b_u(jb):
        return jnp.dot(hb[:, jb * kb:(jb + 1) * kb], bblk_ref[jb], preferred_element_type=F32)

    bus[0][...] = b_u(0)
    for jb in range(nb):
        if jb + 1 < nb:
            bus[(jb + 1) % 2][...] = b_u(jb + 1)
        _s5_scan(bus[jb % 2], sbs[jb % 2], st, are_ref, aim_ref, jb, tm)
        ybuf[:, jb * kb:(jb + 1) * kb] = jnp.dot(sbs[jb % 2][...].astype(BF16), cblk_ref[jb],
                                                 preferred_element_type=F32)
    y = ybuf[...] + d_ref[...] * h
    z = jnp.dot(y.astype(BF16), glu_ref[...], preferred_element_type=F32)
    o = z[:, :d] * _sigmoid(z[:, d:])
    for c in range(hnat.shape[0]):
        hnat[c, 0:SEGMENTS * tm, :] = o[:, c * LANES:(c + 1) * LANES]
    for r in range(SEGMENTS):
        for c in range(hnat.shape[0]):
            lanes = slice(c * LANES, (c + 1) * LANES)
            o_ref[0, r, :, lanes] = x_ref[0, r, :, lanes] + hnat[c, pl.ds(r, tm, stride=SEGMENTS), :]


def _prep_s5(a_re, a_im, b_re, b_im, c_re, c_im, log_dt, seg_len):
    g, p = a_re.shape
    gb = S5_GROUP_BLOCK
    nb = g // gb
    a = lax.complex(a_re.astype(F32), a_im.astype(F32))
    dt = jnp.exp(log_dt.astype(F32))[:, None]
    a_bar = jnp.exp(a * dt)
    b_bar = ((a_bar - 1.0) / a)[..., None] * lax.complex(b_re.astype(F32), b_im.astype(F32))
    a_pow = a_bar
    for _ in range(int(math.log2(seg_len))):
        a_pow = a_pow * a_pow
    eye = jnp.eye(gb, dtype=F32)

    def in_blocks(m):
        return jnp.einsum('bgpi,gh->bgihp', m.reshape(nb, gb, p, SSM_GROUP), eye).reshape(
            nb, gb * SSM_GROUP, gb * p)

    def out_blocks(m):
        return jnp.einsum('bgip,gh->bhpgi', m.reshape(nb, gb, SSM_GROUP, p), eye).reshape(
            nb, gb * p, gb * SSM_GROUP)

    def lanes(v):
        return jnp.broadcast_to(v.reshape(nb, 1, gb * p), (nb, SUBLANES, gb * p))

    bblk = jnp.concatenate([in_blocks(b_bar.real), in_blocks(b_bar.imag)], axis=2).astype(BF16)
    cblk = jnp.concatenate([out_blocks(c_re.astype(F32)), out_blocks(-c_im.astype(F32))],
                           axis=1).astype(BF16)
    return (bblk, cblk, lanes(a_bar.real), lanes(a_bar.imag), lanes(a_pow.real), lanes(a_pow.imag))


def _s5_state_to_lanes(s_re, s_im):
    b, g, p = s_re.shape
    nb = g // S5_GROUP_BLOCK
    both = jnp.stack([s_re.reshape(b, nb, S5_GROUP_BLOCK * p), s_im.reshape(b, nb, S5_GROUP_BLOCK * p)],
                     axis=2)
    return both.reshape(b, 1, 2 * g * p).astype(F32)


def _s5_state_from_lanes(s, g, p):
    b = s.shape[0]
    both = s.reshape(b, g // S5_GROUP_BLOCK, 2, S5_GROUP_BLOCK, p)
    return both[:, :, 0].reshape(b, g, p), both[:, :, 1].reshape(b, g, p)


def _odd_mixer(x, norm_g, s_re, s_im, a_re, a_im, b_re, b_im, c_re, c_im, d_skip, log_dt, glu_w, tm):
    b, s, d = x.shape
    g, p = a_re.shape
    seg_len = s // SEGMENTS
    bblk, cblk, are, aim, apr, api = _prep_s5(a_re, a_im, b_re, b_im, c_re, c_im, log_dt, seg_len)
    nb, kb, two_half = bblk.shape
    n_state = nb * two_half
    xv = x.reshape(b, SEGMENTS, seg_len, d)
    rows = SEGMENTS * tm
    grid = (b, seg_len // tm)
    x_spec = pl.BlockSpec((1, SEGMENTS, tm, d), lambda i, t: (i, 0, t, 0))
    lane_spec = _resident(are.shape)
    g2 = norm_g.reshape(1, d)

    ends = pl.pallas_call(
        _s5_ends_body,
        grid=grid,
        in_specs=[x_spec, _resident((1, d)), _resident(bblk.shape), lane_spec, lane_spec],
        out_specs=pl.BlockSpec((1, SEGMENTS, n_state), lambda i, t: (i, 0, 0)),
        out_shape=jax.ShapeDtypeStruct((b, SEGMENTS, n_state), F32),
        scratch_shapes=[
            pltpu.VMEM((d // LANES, SEGMENTS * _s5_pitch(tm), LANES), F32), pltpu.VMEM((rows, d), F32),
            pltpu.VMEM((rows, two_half), F32), pltpu.VMEM((rows, two_half), F32),
            pltpu.VMEM((SEGMENTS, n_state), F32),
        ],
        compiler_params=_params("parallel", "arbitrary"),
        name="s5_ends",
    )(xv, g2, bblk, are, aim)

    out, s_last = pl.pallas_call(
        _s5_main_body,
        grid=grid,
        in_specs=[
            x_spec, _resident((1, d)), _resident(bblk.shape), _resident(cblk.shape),
            lane_spec, lane_spec, lane_spec, lane_spec, _resident((1, d)),
            _resident(glu_w.shape),
            pl.BlockSpec((1, SEGMENTS, n_state), lambda i, t: (i, 0, 0)),
            pl.BlockSpec((1, 1, n_state), lambda i, t: (i, 0, 0)),
        ],
        out_specs=[x_spec, pl.BlockSpec((1, 1, n_state), lambda i, t: (i, 0, 0))],
        out_shape=[jax.ShapeDtypeStruct(xv.shape, F32),
                   jax.ShapeDtypeStruct((b, 1, n_state), F32)],
        scratch_shapes=[
            pltpu.VMEM((d // LANES, SEGMENTS * _s5_pitch(tm), LANES), F32), pltpu.VMEM((rows, d), F32),
            pltpu.VMEM((rows, two_half), F32), pltpu.VMEM((rows, two_half), F32),
            pltpu.VMEM((rows, two_half), F32), pltpu.VMEM((rows, two_half), F32),
            pltpu.VMEM((rows, d), F32), pltpu.VMEM((SEGMENTS, n_state), F32),
        ],
        compiler_params=_params("parallel", "arbitrary"),
        name="s5_main",
    )(xv, g2, bblk, cblk, are, aim, apr, api, d_skip.reshape(1, d).astype(F32),
      glu_w.astype(BF16), ends, _s5_state_to_lanes(s_re, s_im))
    n_re, n_im = _s5_state_from_lanes(s_last, g, p)
    return out.reshape(b, s, d), n_re, n_im


def _to_pairs(t):
    b, rows, heads, hd = t.shape
    return t.reshape(b, rows, heads // 2, 2 * hd).transpose(0, 2, 1, 3).astype(BF16)


def _even_mixer(x, p, e, conv0, h0, cache_k, cache_v, prompt):
    b, s, d = x.shape
    tm = min(s, WINDOW)
    ya, q, k, v, k_tail, v_tail, h_last, c_tail = _inproj(
        x, p['mix_norm_l'], p['ab_w_in'][e], p['q_norm'][e], p['k_norm'][e], conv0, h0,
        p['conv_w'][e], p['conv_b'][e], p['lru_wa'][e], p['lru_ba'][e], p['lru_wx'][e],
        p['lru_bx'][e], p['lru_lambda'][e], tm)
    w = ya.shape[2]
    heads = w // HEAD_DIM
    h_new = h_last[:, 0, :]
    conv_new = c_tail[:, SUBLANES - (CONV_WIDTH - 1):, :]
    if prompt:
        nq = ATTN_BLOCK
        n_parts = WINDOW // nq
        bias = _band_bias(p['rel_bias'][e], nq, [WINDOW - nq * v for v in range(n_parts + 1)])
        maps = [functools.partial(lambda i, t, back: (i, 0, jnp.maximum(t - back, 0), 0), back=back)
                for back in range(n_parts, 0, -1)]
        yb = _attn(q, k, v, [(k, nq, m) for m in maps], [(v, nq, m) for m in maps], bias, nq,
                   lambda i, t: (jnp.minimum(t, n_parts), 0, 0, 0))
    else:
        wc = cache_k.shape[1]
        assert wc == WINDOW and s == CHUNK and PAST_LEN % CHUNK == 0
        bias = _band_bias(p['rel_bias'][e], s, [0])
        first = lambda i, t: (i, 0, 0, 0)
        yb = _attn(q, k, v, [(_to_pairs(cache_k), wc, first)], [(_to_pairs(cache_v), wc, first)],
                   bias, s, lambda i, t: (0, 0, 0, 0))
    y = _outproj(x, ya, yb, p['ab_w_out'][e], tm)
    return (y, conv_new, h_new, k_tail.reshape(b, -1, heads, HEAD_DIM),
            v_tail.reshape(b, -1, heads, HEAD_DIM))


def _trunk(x, p, conv_st, lru_st, cache_k, cache_v, ssm_re_st, ssm_im_st, prompt):
    b, s, d = x.shape
    n = b * s
    tm = min(n, FFN_ROWS)
    depth = p['ffn1_norm'].shape[0]
    conv_out, lru_out, k_out, v_out, re_out, im_out = [], [], [], [], [], []
    for l in range(depth):
        x = _ffn(x.reshape(n, d), p['ffn1_norm'][l], *p['ffn1_w'], l, tm).reshape(b, s, d)
        if l % 2 == 0:
            e = l // 2
            w = p['conv_w'].shape[-1]
            if prompt:
                c_prev = jnp.zeros((b, CONV_WIDTH - 1, w), F32)
                h_prev = jnp.zeros((b, w), F32)
                ck = cv = None
            else:
                c_prev, h_prev, ck, cv = conv_st[e], lru_st[e], cache_k[e], cache_v[e]
            pe = dict(p, mix_norm_l=p['mix_norm'][l])
            x, c_new, h_new, k_new, v_new = _even_mixer(x, pe, e, c_prev, h_prev, ck, cv, prompt)
            conv_out.append(c_new)
            lru_out.append(h_new)
            k_out.append(k_new)
            v_out.append(v_new)
        else:
            o = l // 2
            g, st = p['ssm_A_re'].shape[1:]
            if prompt:
                s_re = jnp.zeros((b, g, st), F32)
                s_im = jnp.zeros((b, g, st), F32)
            else:
                s_re, s_im = ssm_re_st[o], ssm_im_st[o]
            x, n_re, n_im = _odd_mixer(
                x, p['mix_norm'][l], s_re, s_im, p['ssm_A_re'][o], p['ssm_A_im'][o], p['ssm_B_re'][o],
                p['ssm_B_im'][o], p['ssm_C_re'][o], p['ssm_C_im'][o], p['ssm_D'][o],
                p['ssm_log_dt'][o], p['glu_w'][o], min(s // SEGMENTS, 64))
            re_out.append(n_re)
            im_out.append(n_im)
        x = _ffn(x.reshape(n, d), p['ffn2_norm'][l], *p['ffn2_w'], l, tm).reshape(b, s, d)
    return (x, jnp.stack(conv_out), jnp.stack(lru_out), jnp.stack(k_out), jnp.stack(v_out),
            jnp.stack(re_out), jnp.stack(im_out))


def kernel(x_prompt, x_sample, state_rglru_conv, state_rglru_h, cache_band_k, cache_band_v,
           state_ssm_re, state_ssm_im, ffn1_norm, ffn1_w_in, ffn1_w_out, mix_norm, ffn2_norm,
           ffn2_w_in, ffn2_w_out, ab_w_in, conv_w, conv_b, lru_wa, lru_ba, lru_wx, lru_bx,
           lru_lambda, q_norm, k_norm, rel_bias, ab_w_out, ssm_A_re, ssm_A_im, ssm_B_re, ssm_B_im,
           ssm_C_re, ssm_C_im, ssm_D, ssm_log_dt, glu_w):
    depth = ffn1_norm.shape[0]
    p = dict(ffn1_norm=ffn1_norm, mix_norm=mix_norm, ffn2_norm=ffn2_norm, ab_w_in=ab_w_in,
             conv_w=conv_w, conv_b=conv_b, lru_wa=lru_wa, lru_ba=lru_ba, lru_wx=lru_wx, lru_bx=lru_bx,
             lru_lambda=lru_lambda, q_norm=q_norm, k_norm=k_norm, rel_bias=rel_bias, ab_w_out=ab_w_out,
             ssm_A_re=ssm_A_re, ssm_A_im=ssm_A_im, ssm_B_re=ssm_B_re, ssm_B_im=ssm_B_im,
             ssm_C_re=ssm_C_re, ssm_C_im=ssm_C_im, ssm_D=ssm_D, ssm_log_dt=ssm_log_dt, glu_w=glu_w)
    p['ffn1_w'] = (ffn1_w_in.astype(BF16), ffn1_w_out.astype(BF16))
    p['ffn2_w'] = (ffn2_w_in.astype(BF16), ffn2_w_out.astype(BF16))
    y_prompt, p_conv, p_h, p_k, p_v, p_re, p_im = _trunk(
        x_prompt, p, None, None, None, None, None, None, True)
    y_sample, s_conv, s_h, s_k, s_v, s_re, s_im = _trunk(
        x_sample, p, state_rglru_conv, state_rglru_h, cache_band_k, cache_band_v,
        state_ssm_re, state_ssm_im, False)
    return (y_prompt, y_sample, p_conv, p_h, p_k, p_v, p_re, p_im, s_conv, s_h, s_k, s_v, s_re, s_im)
```

```python
import functools
import math

import jax
import jax.numpy as jnp
from jax import lax
from jax.experimental import pallas as pl
from jax.experimental.pallas import tpu as pltpu

F32 = jnp.float32
BF16 = jnp.bfloat16

LANES = 128
SUBLANES = 8
VMEM_LIMIT_BYTES = 56 * 1024 * 1024

EPS = 1e-6
CHUNK = 64
LEFT_CHUNKS = 8
WINDOW = LEFT_CHUNKS * CHUNK
BAND = WINDOW + CHUNK
MAX_REL = 128
PAST_LEN = 4096
HEAD_DIM = 64
CONV_WIDTH = 4
LRU_C = 8.0
SSM_GROUP = 16
SSM_STATE = 64
SEGMENTS = SUBLANES
S5_GROUP_BLOCK = 16
S5_ENDS_STEPS = 4
S5_ENDS_ROWS = 1024
FFN_CHUNK = 256
ATTN_BLOCK = 256
ATTN_BLOCKS_PER_STEP = 2
FFN_ROWS = 512
NEG_INF = -1e30


def _params(*sem):
    return pltpu.CompilerParams(dimension_semantics=sem, vmem_limit_bytes=VMEM_LIMIT_BYTES)


def _resident(shape):
    nd = len(shape)
    return pl.BlockSpec(shape, lambda *_: (0,) * nd, pipeline_mode=pl.Buffered(1))


def _rms(x, g):
    return x * lax.rsqrt(jnp.mean(x * x, axis=-1, keepdims=True) + EPS) * g


def _sigmoid(x):
    return 1.0 / (1.0 + jnp.exp(-x))


def _ffn_body(x_ref, g_ref, win_ref, wout_ref, o_ref, acc_ref):
    d_ff = wout_ref.shape[0]
    x = x_ref[...]
    h = _rms(x, g_ref[...]).astype(BF16)
    for c in range(d_ff // FFN_CHUNK):
        cols = slice(c * FFN_CHUNK, (c + 1) * FFN_CHUNK)
        up_cols = slice(d_ff + c * FFN_CHUNK, d_ff + (c + 1) * FFN_CHUNK)
        gate = jnp.dot(h, win_ref[:, cols], preferred_element_type=F32)
        up = jnp.dot(h, win_ref[:, up_cols], preferred_element_type=F32)
        act = (gate * _sigmoid(gate) * up).astype(BF16)
        y = jnp.dot(act, wout_ref[cols, :], preferred_element_type=F32)
        if c == 0:
            acc_ref[...] = y
        else:
            acc_ref[...] += y
    o_ref[...] = x + 0.5 * acc_ref[...]


def _ffn(x2d, g, w_in_all, w_out_all, layer, tm):
    n, d = x2d.shape
    _, d_ff, _ = w_out_all.shape
    assert d_ff % FFN_CHUNK == 0 and n % tm == 0
    return pl.pallas_call(
        _ffn_body,
        grid=(n // tm,),
        in_specs=[
            pl.BlockSpec((tm, d), lambda i: (i, 0)),
            _resident((1, d)),
            pl.BlockSpec((None, d, 2 * d_ff), lambda i: (layer, 0, 0), pipeline_mode=pl.Buffered(1)),
            pl.BlockSpec((None, d_ff, d), lambda i: (layer, 0, 0), pipeline_mode=pl.Buffered(1)),
        ],
        out_specs=pl.BlockSpec((tm, d), lambda i: (i, 0)),
        out_shape=jax.ShapeDtypeStruct((n, d), F32),
        scratch_shapes=[pltpu.VMEM((tm, d), F32)],
        compiler_params=_params("parallel"),
        name="ffn",
    )(x2d, g.reshape(1, d), w_in_all, w_out_all)


def _head_norm(t, gain):
    low = lax.broadcasted_iota(jnp.int32, (t.shape[0], LANES), 1) < HEAD_DIM
    outs = []
    for j in range(t.shape[1] // LANES):
        blk = t[:, j * LANES:(j + 1) * LANES]
        sq = blk * blk
        tot = jnp.sum(sq, axis=-1, keepdims=True)
        lo = jnp.sum(jnp.where(low, sq, 0.0), axis=-1, keepdims=True)
        ms = jnp.where(low, lo, tot - lo) * (1.0 / HEAD_DIM)
        outs.append(blk * lax.rsqrt(ms + EPS) * gain[:, j * LANES:(j + 1) * LANES])
    return jnp.concatenate(outs, axis=-1)


def _block_diag(w):
    nb, n, _ = w.shape
    eye = jnp.eye(nb, dtype=w.dtype)
    return jnp.einsum('hij,hg->higj', w, eye).reshape(nb * n, nb * n)


def _rg_lru(xa, ga, c0_ref, h0_ref, cw_ref, cb_ref, wa_ref, ba_ref, wx_ref, bx_ref, lam_ref,
            ya_ref, hlast_ref, ctail_ref, ext_ref, hc_ref, a_ref, u_ref):
    t_len, w = xa.shape

    @pl.when(pl.program_id(1) == 0)
    def _():
        ext_ref[0:SUBLANES, :] = c0_ref[0]
        hc_ref[...] = jnp.broadcast_to(h0_ref[0], hc_ref.shape)

    ext_ref[SUBLANES:SUBLANES + t_len, :] = xa
    xc = cb_ref[...] + cw_ref[3:4, :] * xa
    for k in range(CONV_WIDTH - 1):
        off = SUBLANES - (CONV_WIDTH - 1) + k
        xc = xc + cw_ref[k:k + 1, :] * ext_ref[off:off + t_len, :]
    tail = xa[t_len - SUBLANES:, :]
    ext_ref[0:SUBLANES, :] = tail
    ctail_ref[0] = tail

    xcb = xc.astype(BF16)
    r = _sigmoid(jnp.dot(xcb, wa_ref[...], preferred_element_type=F32) + ba_ref[...])
    i = _sigmoid(jnp.dot(xcb, wx_ref[...], preferred_element_type=F32) + bx_ref[...])
    neg_lam = -lam_ref[...]
    softplus = jnp.maximum(neg_lam, 0.0) + jnp.log1p(jnp.exp(-jnp.abs(neg_lam)))
    log_a = -LRU_C * r * softplus
    a = jnp.exp(log_a)
    a_ref[...] = a
    u_ref[...] = jnp.sqrt(-jnp.tanh(log_a) * (a * a + 1.0)) * (i * xc)

    row = lax.broadcasted_iota(jnp.int32, (SUBLANES, w), 0)
    h = hc_ref[...]
    for blk in range(t_len // SUBLANES):
        rows = slice(blk * SUBLANES, (blk + 1) * SUBLANES)
        a = a_ref[rows, :]
        u = u_ref[rows, :]
        for d in (1, 2, 4):
            keep = row >= d
            a_sh = jnp.where(keep, pltpu.roll(a, d, 0), 1.0)
            u_sh = jnp.where(keep, pltpu.roll(u, d, 0), 0.0)
            u = u + a * u_sh
            a = a * a_sh
        hs = a * h + u
        ya_ref[0, rows, :] = (hs * jax.nn.gelu(ga[rows, :])).astype(ya_ref.dtype)
        h = jnp.broadcast_to(hs[SUBLANES - 1:SUBLANES, :], (SUBLANES, w))
    hc_ref[...] = h
    hlast_ref[0] = h


def _inproj_body(x_ref, g_ref, w_ref, qg_ref, kg_ref, c0_ref, h0_ref, cw_ref, cb_ref, wa_ref,
                 ba_ref, wx_ref, bx_ref, lam_ref, ya_ref, q_ref, k_ref, v_ref, kt_ref, vt_ref,
                 hlast_ref, ctail_ref, ext_ref, hc_ref, a_ref, u_ref):
    w = kt_ref.shape[2]
    h = _rms(x_ref[0], g_ref[...]).astype(BF16)
    pa = jnp.dot(h, w_ref[:, :2 * w], preferred_element_type=F32)
    _rg_lru(pa[:, :w], pa[:, w:], c0_ref, h0_ref, cw_ref, cb_ref, wa_ref, ba_ref, wx_ref, bx_ref,
            lam_ref, ya_ref, hlast_ref, ctail_ref, ext_ref, hc_ref, a_ref, u_ref)
    pb = jnp.dot(h, w_ref[:, 2 * w:], preferred_element_type=F32)
    q = _head_norm(pb[:, 0 * w:1 * w], qg_ref[...]) * (1.0 / math.sqrt(HEAD_DIM))
    k = _head_norm(pb[:, 1 * w:2 * w], kg_ref[...])
    v = pb[:, 2 * w:3 * w]
    kt_ref[0] = k
    vt_ref[0] = v
    for pr in range(w // LANES):
        lanes = slice(pr * LANES, (pr + 1) * LANES)
        q_ref[0, pr] = q[:, lanes].astype(BF16)
        k_ref[0, pr] = k[:, lanes].astype(BF16)
        v_ref[0, pr] = v[:, lanes].astype(BF16)


def _inproj(x, g, w_in, q_gain, k_gain, conv0, h0, conv_w, conv_b, wa, ba, wx, bx, lam, tm):
    b, s, d = x.shape
    w = w_in.shape[1] // 5
    heads = w // HEAD_DIM
    npair = w // LANES
    assert s % tm == 0 and tm % SUBLANES == 0
    pad = jnp.zeros((b, SUBLANES - (CONV_WIDTH - 1), w), F32)
    c0 = jnp.concatenate([pad, conv0.astype(F32)], axis=1)
    row = pl.BlockSpec((1, tm, w), lambda i, t: (i, t, 0))
    pair = pl.BlockSpec((1, npair, tm, LANES), lambda i, t: (i, 0, t, 0))
    tail = pl.BlockSpec((1, tm, w), lambda i, t: (i, 0, 0))
    state = pl.BlockSpec((1, SUBLANES, w), lambda i, t: (i, 0, 0))
    vec = _resident((1, w))
    return pl.pallas_call(
        _inproj_body,
        grid=(b, s // tm),
        in_specs=[
            pl.BlockSpec((1, tm, d), lambda i, t: (i, t, 0)),
            _resident((1, d)), _resident(w_in.shape), vec, vec,
            state, pl.BlockSpec((1, 1, w), lambda i, t: (i, 0, 0)),
            _resident((CONV_WIDTH, w)), vec, _resident((w, w)), vec, _resident((w, w)), vec, vec,
        ],
        out_specs=[row, pair, pair, pair, tail, tail, state, state],
        out_shape=[jax.ShapeDtypeStruct((b, s, w), BF16)]
        + [jax.ShapeDtypeStruct((b, npair, s, LANES), BF16)] * 3
        + [jax.ShapeDtypeStruct((b, tm, w), F32)] * 2
        + [jax.ShapeDtypeStruct((b, SUBLANES, w), F32)] * 2,
        scratch_shapes=[
            pltpu.VMEM((SUBLANES + tm, w), F32),
            pltpu.VMEM((SUBLANES, w), F32),
            pltpu.VMEM((tm, w), F32),
            pltpu.VMEM((tm, w), F32),
        ],
        compiler_params=_params("parallel", "arbitrary"),
        name="inproj",
    )(x, g.reshape(1, d), w_in.astype(BF16),
      jnp.tile(q_gain, heads).reshape(1, w), jnp.tile(k_gain, heads).reshape(1, w),
      c0, h0.astype(F32).reshape(b, 1, w), conv_w, conv_b.reshape(1, w),
      _block_diag(wa).astype(BF16), ba.reshape(1, w), _block_diag(wx).astype(BF16),
      bx.reshape(1, w), lam.reshape(1, w))


def _attn_body(*refs, n_prev, n_blocks):
    n_parts = n_prev + n_blocks
    q_ref = refs[0]
    k_parts = refs[1:1 + n_parts]
    v_parts = refs[1 + n_parts:1 + 2 * n_parts]
    bias_refs = refs[1 + 2 * n_parts:1 + 2 * n_parts + n_blocks]
    o_ref = refs[-1]
    npair = q_ref.shape[1]
    nq = q_ref.shape[2] // n_blocks
    low = lax.broadcasted_iota(jnp.int32, (nq, LANES), 1) < HEAD_DIM
    for pr in range(npair):
        for j in range(n_blocks):
            rows = slice(j * nq, (j + 1) * nq)
            qp = q_ref[0, pr, rows, :]
            zero = jnp.zeros_like(qp)
            qs = jnp.concatenate([jnp.where(low, qp, zero), jnp.where(low, zero, qp)], axis=0)
            kw = jnp.concatenate([part[0, pr] for part in k_parts[j:j + n_prev + 1]], axis=0)
            vw = jnp.concatenate([part[0, pr] for part in v_parts[j:j + n_prev + 1]], axis=0)
            s = lax.dot_general(qs, kw, (((1,), (1,)), ((), ())), preferred_element_type=F32)
            s = s + bias_refs[j][0, pr]
            p = jnp.exp(s - jnp.max(s, axis=-1, keepdims=True))
            denom = jnp.sum(p, axis=-1, keepdims=True)
            o = jnp.dot(p.astype(BF16), vw, preferred_element_type=F32) / denom
            o_ref[0, pr, rows, :] = jnp.where(low, o[:nq], o[nq:]).astype(o_ref.dtype)


def _band_bias(rel_bias, nq, invalid_cols):
    heads = rel_bias.shape[1]
    win = WINDOW + nq
    period = win + nq
    n_hi = WINDOW - MAX_REL
    n_lo = max(win - (WINDOW + MAX_REL + 1), 0)
    tab = rel_bias.astype(F32)
    mid = jnp.flip(tab, axis=0)[:win - n_hi - n_lo]
    vec = jnp.concatenate([jnp.broadcast_to(tab[-1:], (n_hi, heads)), mid,
                           jnp.broadcast_to(tab[:1], (n_lo, heads)),
                           jnp.broadcast_to(tab[-1:], (nq, heads))], axis=0)
    toep = jnp.tile(vec.T, (1, nq))[:, :nq * (period - 1)].reshape(heads, nq, period - 1)[:, :, :win]
    qc = jnp.arange(nq)[:, None] // CHUNK
    kc = jnp.arange(win)[None, :] // CHUNK
    band = (kc >= qc) & (kc <= qc + LEFT_CHUNKS)
    col = jnp.arange(win)[None, :]
    out = [jnp.where(band & (col >= c), toep, NEG_INF) for c in invalid_cols]
    return jnp.stack(out).reshape(len(invalid_cols), heads // 2, 2 * nq, win)


def _attn(q, k_parts, v_parts, bias, bias_maps, n_prev):
    b, npair, s, _ = q.shape
    n_blocks = len(bias_maps)
    assert len(k_parts) == len(v_parts) == n_prev + n_blocks
    rows = sum(r for _, r, _ in k_parts[n_prev:])

    def spec(nrows, index_map):
        return pl.BlockSpec((1, npair, nrows, LANES), index_map)

    own = spec(rows, lambda i, t: (i, 0, t, 0))
    return pl.pallas_call(
        functools.partial(_attn_body, n_prev=n_prev, n_blocks=n_blocks),
        grid=(b, s // rows),
        in_specs=([own] + [spec(r, m) for _, r, m in k_parts] + [spec(r, m) for _, r, m in v_parts]
                  + [pl.BlockSpec((1,) + bias.shape[1:], m) for m in bias_maps]),
        out_specs=own,
        out_shape=jax.ShapeDtypeStruct((b, npair, s, LANES), BF16),
        compiler_params=_params("parallel", "arbitrary"),
        name="attn",
    )(q, *[a for a, _, _ in k_parts], *[a for a, _, _ in v_parts], *([bias] * n_blocks))


def _outproj_body(x_ref, ya_ref, yb_ref, w_ref, o_ref):
    w = ya_ref.shape[2]
    yb = jnp.concatenate([yb_ref[0, pr] for pr in range(yb_ref.shape[1])], axis=-1)
    y = jnp.dot(ya_ref[0], w_ref[:w, :], preferred_element_type=F32)
    y = y + jnp.dot(yb, w_ref[w:, :], preferred_element_type=F32)
    o_ref[0] = x_ref[0] + y


def _outproj(x, ya, yb, w_out, tm):
    b, s, d = x.shape
    w = ya.shape[2]
    npair = yb.shape[1]
    return pl.pallas_call(
        _outproj_body,
        grid=(b, s // tm),
        in_specs=[
            pl.BlockSpec((1, tm, d), lambda i, t: (i, t, 0)),
            pl.BlockSpec((1, tm, w), lambda i, t: (i, t, 0)),
            pl.BlockSpec((1, npair, tm, LANES), lambda i, t: (i, 0, t, 0)),
            _resident(w_out.shape),
        ],
        out_specs=pl.BlockSpec((1, tm, d), lambda i, t: (i, t, 0)),
        out_shape=jax.ShapeDtypeStruct((b, s, d), F32),
        compiler_params=_params("parallel", "parallel"),
        name="outproj",
    )(x, ya, yb, w_out.astype(BF16))


def _s5_load_h(x_ref, g_ref, hnat, hperm):
    tm = x_ref.shape[2]
    pitch = hnat.shape[1] // SEGMENTS
    n_lane_blocks = hnat.shape[0]
    for r in range(SEGMENTS):
        h = _rms(x_ref[0, r], g_ref[...])
        for c in range(n_lane_blocks):
            hnat[c, r * pitch:r * pitch + tm, :] = h[:, c * LANES:(c + 1) * LANES]
    for m in range(tm):
        for c in range(n_lane_blocks):
            hperm[m * SEGMENTS:(m + 1) * SEGMENTS, c * LANES:(c + 1) * LANES] = (
                hnat[c, pl.ds(m, SEGMENTS, stride=pitch), :])


def _s5_pitch(tm):
    assert tm % SUBLANES == 0
    return tm + SUBLANES // 2


def _s5_scan(bu, sbuf, st, are_ref, aim_ref, jb, tm):
    half = bu.shape[1] // 2
    base = jb * bu.shape[1]
    step = 4 * LANES
    for c0 in range(0, half, step):
        ar = are_ref[jb, :, c0:c0 + step]
        ai = aim_ref[jb, :, c0:c0 + step]
        sr = st[:, base + c0:base + c0 + step]
        si = st[:, base + half + c0:base + half + c0 + step]
        for m in range(tm):
            rows = slice(m * SEGMENTS, (m + 1) * SEGMENTS)
            nr = ar * sr - ai * si + bu[rows, c0:c0 + step]
            ni = ar * si + ai * sr + bu[rows, half + c0:half + c0 + step]
            if sbuf is not None:
                sbuf[rows, c0:c0 + step] = nr
                sbuf[rows, half + c0:half + c0 + step] = ni
            sr, si = nr, ni
        st[:, base + c0:base + c0 + step] = sr
        st[:, base + half + c0:base + half + c0 + step] = si


def _s5_ends_body(x_ref, g_ref, bblk_ref, ajr_ref, aji_ref, ends_ref, hnat, hperm, bu0, bu1, st):
    tm, d = x_ref.shape[2], x_ref.shape[3]
    nb, kbj, _ = bblk_ref.shape
    kb = d // nb
    steps = kbj // kb
    pitch = hnat.shape[1] // SEGMENTS
    bus = (bu0, bu1)

    @pl.when(pl.program_id(1) == 0)
    def _():
        st[...] = jnp.zeros_like(st)

    for r in range(SEGMENTS):
        h = _rms(x_ref[0, r], g_ref[...])
        for c in range(hnat.shape[0]):
            hnat[c, r * pitch:r * pitch + tm, :] = h[:, c * LANES:(c + 1) * LANES]
    for mb in range(tm // steps):
        for j in range(steps):
            for c in range(hnat.shape[0]):
                jb, within = divmod(c * LANES, kb)
                dst = jb * kbj + j * kb + within
                hperm[mb * SEGMENTS:(mb + 1) * SEGMENTS, dst:dst + LANES] = (
                    hnat[c, pl.ds(mb * steps + j, SEGMENTS, stride=pitch), :])
    hb = hperm[...].astype(BF16)

    def b_u(jb):
        return jnp.dot(hb[:, jb * kbj:(jb + 1) * kbj], bblk_ref[jb], preferred_element_type=F32)

    bus[0][...] = b_u(0)
    for jb in range(nb):
        if jb + 1 < nb:
            bus[(jb + 1) % 2][...] = b_u(jb + 1)
        _s5_scan(bus[jb % 2], None, st, ajr_ref, aji_ref, jb, tm // steps)

    @pl.when(pl.program_id(1) == pl.num_programs(1) - 1)
    def _():
        ends_ref[0] = st[...]


def _s5_main_body(x_ref, g_ref, bblk_ref, cblk_ref, are_ref, aim_ref, apr_ref, api_ref, d_ref,
                  glu_ref, ends_ref, s0_ref, o_ref, slast_ref, hnat, hperm, bu0, bu1, sb0, sb1,
                  ybuf, st):
    tm = x_ref.shape[2]
    bus = (bu0, bu1)
    sbs = (sb0, sb1)
    d = x_ref.shape[3]
    nb, kb, two_half = bblk_ref.shape
    half = two_half // 2

    @pl.when(pl.program_id(1) == 0)
    def _():
        for jb in range(nb):
            re_cols = slice(jb * two_half, jb * two_half + half)
            im_cols = slice(jb * two_half + half, (jb + 1) * two_half)
            pr = apr_ref[jb, 0:1, :]
            pi = api_ref[jb, 0:1, :]
            er = s0_ref[0, :, re_cols]
            ei = s0_ref[0, :, im_cols]
            for r in range(SEGMENTS):
                st[r:r + 1, re_cols] = er
                st[r:r + 1, im_cols] = ei
                nr = pr * er - pi * ei + ends_ref[0, r:r + 1, re_cols]
                ni = pr * ei + pi * er + ends_ref[0, r:r + 1, im_cols]
                er, ei = nr, ni
            slast_ref[0, :, re_cols] = er
            slast_ref[0, :, im_cols] = ei

    _s5_load_h(x_ref, g_ref, hnat, hperm)
    h = hperm[...]
    hb = h.astype(BF16)

    def b_u(jb):
        return jnp.dot(hb[:, jb * kb:(jb + 1) * kb], bblk_ref[jb], preferred_element_type=F32)

    bus[0][...] = b_u(0)
    for jb in range(nb):
        if jb + 1 < nb:
            bus[(jb + 1) % 2][...] = b_u(jb + 1)
        _s5_scan(bus[jb % 2], sbs[jb % 2], st, are_ref, aim_ref, jb, tm)
        ybuf[:, jb * kb:(jb + 1) * kb] = jnp.dot(sbs[jb % 2][...].astype(BF16), cblk_ref[jb],
                                                 preferred_element_type=F32)
    y = ybuf[...] + d_ref[...] * h
    z = jnp.dot(y.astype(BF16), glu_ref[...], preferred_element_type=F32)
    o = z[:, :d] * _sigmoid(z[:, d:])
    for c in range(hnat.shape[0]):
        hnat[c, 0:SEGMENTS * tm, :] = o[:, c * LANES:(c + 1) * LANES]
    for r in range(SEGMENTS):
        for c in range(hnat.shape[0]):
            lanes = slice(c * LANES, (c + 1) * LANES)
            o_ref[0, r, :, lanes] = x_ref[0, r, :, lanes] + hnat[c, pl.ds(r, tm, stride=SEGMENTS), :]


def _prep_s5(a_re, a_im, b_re, b_im, c_re, c_im, log_dt, seg_len):
    g, p = a_re.shape
    gb = S5_GROUP_BLOCK
    nb = g // gb
    a = lax.complex(a_re.astype(F32), a_im.astype(F32))
    dt = jnp.exp(log_dt.astype(F32))[:, None]
    a_bar = jnp.exp(a * dt)
    b_bar = ((a_bar - 1.0) / a)[..., None] * lax.complex(b_re.astype(F32), b_im.astype(F32))
    a_pow = a_bar
    for _ in range(int(math.log2(seg_len))):
        a_pow = a_pow * a_pow
    a_j = [jnp.ones_like(a_bar)]
    for _ in range(S5_ENDS_STEPS):
        a_j.append(a_j[-1] * a_bar)
    eye = jnp.eye(gb, dtype=F32)

    def in_blocks(m):
        return jnp.einsum('bgpi,gh->bgihp', m.reshape(nb, gb, p, SSM_GROUP), eye).reshape(
            nb, gb * SSM_GROUP, gb * p)

    def out_blocks(m):
        return jnp.einsum('bgip,gh->bhpgi', m.reshape(nb, gb, SSM_GROUP, p), eye).reshape(
            nb, gb * p, gb * SSM_GROUP)

    def lanes(v):
        return jnp.broadcast_to(v.reshape(nb, 1, gb * p), (nb, SUBLANES, gb * p))

    def both(m):
        return jnp.concatenate([in_blocks(m.real), in_blocks(m.imag)], axis=2).astype(BF16)

    bblk = both(b_bar)
    bblk_ends = jnp.concatenate([both(a_j[S5_ENDS_STEPS - 1 - j][..., None] * b_bar)
                                 for j in range(S5_ENDS_STEPS)], axis=1)
    cblk = jnp.concatenate([out_blocks(c_re.astype(F32)), out_blocks(-c_im.astype(F32))],
                           axis=1).astype(BF16)
    a_ends = a_j[S5_ENDS_STEPS]
    return (bblk, cblk, lanes(a_bar.real), lanes(a_bar.imag), lanes(a_pow.real), lanes(a_pow.imag),
            bblk_ends, lanes(a_ends.real), lanes(a_ends.imag))


def _s5_state_to_lanes(s_re, s_im):
    b, g, p = s_re.shape
    nb = g // S5_GROUP_BLOCK
    both = jnp.stack([s_re.reshape(b, nb, S5_GROUP_BLOCK * p), s_im.reshape(b, nb, S5_GROUP_BLOCK * p)],
                     axis=2)
    return both.reshape(b, 1, 2 * g * p).astype(F32)


def _s5_state_from_lanes(s, g, p):
    b = s.shape[0]
    both = s.reshape(b, g // S5_GROUP_BLOCK, 2, S5_GROUP_BLOCK, p)
    return both[:, :, 0].reshape(b, g, p), both[:, :, 1].reshape(b, g, p)


def _odd_mixer(x, norm_g, s_re, s_im, a_re, a_im, b_re, b_im, c_re, c_im, d_skip, log_dt, glu_w, tm):
    b, s, d = x.shape
    g, p = a_re.shape
    seg_len = s // SEGMENTS
    bblk, cblk, are, aim, apr, api, bblk_ends, ajr, aji = _prep_s5(
        a_re, a_im, b_re, b_im, c_re, c_im, log_dt, seg_len)
    nb, kb, two_half = bblk.shape
    n_state = nb * two_half
    xv = x.reshape(b, SEGMENTS, seg_len, d)
    rows = SEGMENTS * tm
    grid = (b, seg_len // tm)
    x_spec = pl.BlockSpec((1, SEGMENTS, tm, d), lambda i, t: (i, 0, t, 0))
    lane_spec = _resident(are.shape)
    g2 = norm_g.reshape(1, d)

    tm_e = min(seg_len, S5_ENDS_ROWS // SEGMENTS)
    assert seg_len % tm_e == 0 and tm_e % S5_ENDS_STEPS == 0
    ends = pl.pallas_call(
        _s5_ends_body,
        grid=(b, seg_len // tm_e),
        in_specs=[pl.BlockSpec((1, SEGMENTS, tm_e, d), lambda i, t: (i, 0, t, 0)), _resident((1, d)),
                  _resident(bblk_ends.shape), lane_spec, lane_spec],
        out_specs=pl.BlockSpec((1, SEGMENTS, n_state), lambda i, t: (i, 0, 0)),
        out_shape=jax.ShapeDtypeStruct((b, SEGMENTS, n_state), F32),
        scratch_shapes=[
            pltpu.VMEM((d // LANES, SEGMENTS * _s5_pitch(tm_e), LANES), F32),
            pltpu.VMEM((SEGMENTS * tm_e // S5_ENDS_STEPS, S5_ENDS_STEPS * d), F32),
            pltpu.VMEM((SEGMENTS * tm_e // S5_ENDS_STEPS, two_half), F32),
            pltpu.VMEM((SEGMENTS * tm_e // S5_ENDS_STEPS, two_half), F32),
            pltpu.VMEM((SEGMENTS, n_state), F32),
        ],
        compiler_params=_params("parallel", "arbitrary"),
        name="s5_ends",
    )(xv, g2, bblk_ends, ajr, aji)

    out, s_last = pl.pallas_call(
        _s5_main_body,
        grid=grid,
        in_specs=[
            x_spec, _resident((1, d)), _resident(bblk.shape), _resident(cblk.shape),
            lane_spec, lane_spec, lane_spec, lane_spec, _resident((1, d)),
            _resident(glu_w.shape),
            pl.BlockSpec((1, SEGMENTS, n_state), lambda i, t: (i, 0, 0)),
            pl.BlockSpec((1, 1, n_state), lambda i, t: (i, 0, 0)),
        ],
        out_specs=[x_spec, pl.BlockSpec((1, 1, n_state), lambda i, t: (i, 0, 0))],
        out_shape=[jax.ShapeDtypeStruct(xv.shape, F32),
                   jax.ShapeDtypeStruct((b, 1, n_state), F32)],
        scratch_shapes=[
            pltpu.VMEM((d // LANES, SEGMENTS * _s5_pitch(tm), LANES), F32), pltpu.VMEM((rows, d), F32),
            pltpu.VMEM((rows, two_half), F32), pltpu.VMEM((rows, two_half), F32),
            pltpu.VMEM((rows, two_half), F32), pltpu.VMEM((rows, two_half), F32),
            pltpu.VMEM((rows, d), F32), pltpu.VMEM((SEGMENTS, n_state), F32),
        ],
        compiler_params=_params("parallel", "arbitrary"),
        name="s5_main",
    )(xv, g2, bblk, cblk, are, aim, apr, api, d_skip.reshape(1, d).astype(F32),
      glu_w.astype(BF16), ends, _s5_state_to_lanes(s_re, s_im))
    n_re, n_im = _s5_state_from_lanes(s_last, g, p)
    return out.reshape(b, s, d), n_re, n_im


def _to_pairs(t):
    b, rows, heads, hd = t.shape
    return t.reshape(b, rows, heads // 2, 2 * hd).transpose(0, 2, 1, 3).astype(BF16)


def _even_mixer(x, p, e, conv0, h0, cache_k, cache_v, prompt):
    b, s, d = x.shape
    tm = min(s, WINDOW)
    ya, q, k, v, k_tail, v_tail, h_last, c_tail = _inproj(
        x, p['mix_norm_l'], p['ab_w_in'][e], p['q_norm'][e], p['k_norm'][e], conv0, h0,
        p['conv_w'][e], p['conv_b'][e], p['lru_wa'][e], p['lru_ba'][e], p['lru_wx'][e],
        p['lru_bx'][e], p['lru_lambda'][e], tm)
    w = ya.shape[2]
    heads = w // HEAD_DIM
    h_new = h_last[:, 0, :]
    conv_new = c_tail[:, SUBLANES - (CONV_WIDTH - 1):, :]
    if prompt:
        nq = ATTN_BLOCK
        n_prev = WINDOW // nq
        nblk = ATTN_BLOCKS_PER_STEP
        bias = _band_bias(p['rel_bias'][e], nq, [WINDOW - nq * v for v in range(n_prev + 1)])
        maps = [functools.partial(lambda i, t, off: (i, 0, jnp.maximum(nblk * t + off, 0), 0),
                                  off=idx - n_prev) for idx in range(n_prev + nblk)]
        bias_maps = [functools.partial(lambda i, t, j: (jnp.minimum(nblk * t + j, n_prev), 0, 0, 0), j=j)
                     for j in range(nblk)]
        yb = _attn(q, [(k, nq, m) for m in maps], [(v, nq, m) for m in maps], bias, bias_maps, n_prev)
    else:
        wc = cache_k.shape[1]
        assert wc == WINDOW and s == CHUNK and PAST_LEN % CHUNK == 0
        bias = _band_bias(p['rel_bias'][e], s, [0])
        first = lambda i, t: (i, 0, 0, 0)
        yb = _attn(q, [(_to_pairs(cache_k), wc, first), (k, s, first)],
                   [(_to_pairs(cache_v), wc, first), (v, s, first)], bias,
                   [lambda i, t: (0, 0, 0, 0)], 1)
    y = _outproj(x, ya, yb, p['ab_w_out'][e], tm)
    return (y, conv_new, h_new, k_tail.reshape(b, -1, heads, HEAD_DIM),
            v_tail.reshape(b, -1, heads, HEAD_DIM))


def _trunk(x, p, conv_st, lru_st, cache_k, cache_v, ssm_re_st, ssm_im_st, prompt):
    b, s, d = x.shape
    n = b * s
    tm = min(n, FFN_ROWS)
    depth = p['ffn1_norm'].shape[0]
    conv_out, lru_out, k_out, v_out, re_out, im_out = [], [], [], [], [], []
    for l in range(depth):
        x = _ffn(x.reshape(n, d), p['ffn1_norm'][l], *p['ffn1_w'], l, tm).reshape(b, s, d)
        if l % 2 == 0:
            e = l // 2
            w = p['conv_w'].shape[-1]
            if prompt:
                c_prev = jnp.zeros((b, CONV_WIDTH - 1, w), F32)
                h_prev = jnp.zeros((b, w), F32)
                ck = cv = None
            else:
                c_prev, h_prev, ck, cv = conv_st[e], lru_st[e], cache_k[e], cache_v[e]
            pe = dict(p, mix_norm_l=p['mix_norm'][l])
            x, c_new, h_new, k_new, v_new = _even_mixer(x, pe, e, c_prev, h_prev, ck, cv, prompt)
            conv_out.append(c_new)
            lru_out.append(h_new)
            k_out.append(k_new)
            v_out.append(v_new)
        else:
            o = l // 2
            g, st = p['ssm_A_re'].shape[1:]
            if prompt:
                s_re = jnp.zeros((b, g, st), F32)
                s_im = jnp.zeros((b, g, st), F32)
            else:
                s_re, s_im = ssm_re_st[o], ssm_im_st[o]
            x, n_re, n_im = _odd_mixer(
                x, p['mix_norm'][l], s_re, s_im, p['ssm_A_re'][o], p['ssm_A_im'][o], p['ssm_B_re'][o],
                p['ssm_B_im'][o], p['ssm_C_re'][o], p['ssm_C_im'][o], p['ssm_D'][o],
                p['ssm_log_dt'][o], p['glu_w'][o], min(s // SEGMENTS, 64))
            re_out.append(n_re)
            im_out.append(n_im)
        x = _ffn(x.reshape(n, d), p['ffn2_norm'][l], *p['ffn2_w'], l, tm).reshape(b, s, d)
    return (x, jnp.stack(conv_out), jnp.stack(lru_out), jnp.stack(k_out), jnp.stack(v_out),
            jnp.stack(re_out), jnp.stack(im_out))


def kernel(x_prompt, x_sample, state_rglru_conv, state_rglru_h, cache_band_k, cache_band_v,
           state_ssm_re, state_ssm_im, ffn1_norm, ffn1_w_in, ffn1_w_out, mix_norm, ffn2_norm,
           ffn2_w_in, ffn2_w_out, ab_w_in, conv_w, conv_b, lru_wa, lru_ba, lru_wx, lru_bx,
           lru_lambda, q_norm, k_norm, rel_bias, ab_w_out, ssm_A_re, ssm_A_im, ssm_B_re, ssm_B_im,
           ssm_C_re, ssm_C_im, ssm_D, ssm_log_dt, glu_w):
    depth = ffn1_norm.shape[0]
    p = dict(ffn1_norm=ffn1_norm, mix_norm=mix_norm, ffn2_norm=ffn2_norm, ab_w_in=ab_w_in,
             conv_w=conv_w, conv_b=conv_b, lru_wa=lru_wa, lru_ba=lru_ba, lru_wx=lru_wx, lru_bx=lru_bx,
             lru_lambda=lru_lambda, q_norm=q_norm, k_norm=k_norm, rel_bias=rel_bias, ab_w_out=ab_w_out,
             ssm_A_re=ssm_A_re, ssm_A_im=ssm_A_im, ssm_B_re=ssm_B_re, ssm_B_im=ssm_B_im,
             ssm_C_re=ssm_C_re, ssm_C_im=ssm_C_im, ssm_D=ssm_D, ssm_log_dt=ssm_log_dt, glu_w=glu_w)
    p['ffn1_w'] = (ffn1_w_in.astype(BF16), ffn1_w_out.astype(BF16))
    p['ffn2_w'] = (ffn2_w_in.astype(BF16), ffn2_w_out.astype(BF16))
    y_prompt, p_conv, p_h, p_k, p_v, p_re, p_im = _trunk(
        x_prompt, p, None, None, None, None, None, None, True)
    y_sample, s_conv, s_h, s_k, s_v, s_re, s_im = _trunk(
        x_sample, p, state_rglru_conv, state_rglru_h, cache_band_k, cache_band_v,
        state_ssm_re, state_ssm_im, False)
    return (y_prompt, y_sample, p_conv, p_h, p_k, p_v, p_re, p_im, s_conv, s_h, s_k, s_v, s_re, s_im)
```

```python
import functools
import math

import jax
import jax.numpy as jnp
from jax import lax
from jax.experimental import pallas as pl
from jax.experimental.pallas import tpu as pltpu

F32 = jnp.float32
BF16 = jnp.bfloat16

LANES = 128
SUBLANES = 8
VMEM_LIMIT_BYTES = 56 * 1024 * 1024

EPS = 1e-6
CHUNK = 64
LEFT_CHUNKS = 8
WINDOW = LEFT_CHUNKS * CHUNK
BAND = WINDOW + CHUNK
MAX_REL = 128
PAST_LEN = 4096
HEAD_DIM = 64
CONV_WIDTH = 4
LRU_C = 8.0
SSM_GROUP = 16
SSM_STATE = 64
SEGMENTS = SUBLANES
S5_GROUP_BLOCK = 16
S5_ENDS_STEPS = 4
S5_ENDS_ROWS = 1024
FFN_CHUNK = 256
ATTN_BLOCK = 256
ATTN_BLOCKS_PER_STEP = 2
FFN_ROWS = 512
NEG_INF = -1e30


def _params(*sem):
    return pltpu.CompilerParams(dimension_semantics=sem, vmem_limit_bytes=VMEM_LIMIT_BYTES)


def _resident(shape):
    nd = len(shape)
    return pl.BlockSpec(shape, lambda *_: (0,) * nd, pipeline_mode=pl.Buffered(1))


def _rms(x, g):
    return x * lax.rsqrt(jnp.mean(x * x, axis=-1, keepdims=True) + EPS) * g


def _sigmoid(x):
    return 1.0 / (1.0 + jnp.exp(-x))


def _ffn_body(x_ref, g_ref, win_ref, wout_ref, o_ref, acc_ref):
    d_ff = wout_ref.shape[0]
    x = x_ref[...]
    h = _rms(x, g_ref[...]).astype(BF16)
    for c in range(d_ff // FFN_CHUNK):
        cols = slice(c * FFN_CHUNK, (c + 1) * FFN_CHUNK)
        up_cols = slice(d_ff + c * FFN_CHUNK, d_ff + (c + 1) * FFN_CHUNK)
        gate = jnp.dot(h, win_ref[:, cols], preferred_element_type=F32)
        up = jnp.dot(h, win_ref[:, up_cols], preferred_element_type=F32)
        act = (gate * _sigmoid(gate) * up).astype(BF16)
        y = jnp.dot(act, wout_ref[cols, :], preferred_element_type=F32)
        if c == 0:
            acc_ref[...] = y
        else:
            acc_ref[...] += y
    o_ref[...] = x + 0.5 * acc_ref[...]


def _ffn(x2d, g, w_in_all, w_out_all, layer, tm):
    n, d = x2d.shape
    _, d_ff, _ = w_out_all.shape
    assert d_ff % FFN_CHUNK == 0 and n % tm == 0
    return pl.pallas_call(
        _ffn_body,
        grid=(n // tm,),
        in_specs=[
            pl.BlockSpec((tm, d), lambda i: (i, 0)),
            _resident((1, d)),
            pl.BlockSpec((None, d, 2 * d_ff), lambda i: (layer, 0, 0), pipeline_mode=pl.Buffered(1)),
            pl.BlockSpec((None, d_ff, d), lambda i: (layer, 0, 0), pipeline_mode=pl.Buffered(1)),
        ],
        out_specs=pl.BlockSpec((tm, d), lambda i: (i, 0)),
        out_shape=jax.ShapeDtypeStruct((n, d), F32),
        scratch_shapes=[pltpu.VMEM((tm, d), F32)],
        compiler_params=_params("parallel"),
        name="ffn",
    )(x2d, g.reshape(1, d), w_in_all, w_out_all)


def _head_norm(t, gain):
    low = lax.broadcasted_iota(jnp.int32, (t.shape[0], LANES), 1) < HEAD_DIM
    outs = []
    for j in range(t.shape[1] // LANES):
        blk = t[:, j * LANES:(j + 1) * LANES]
        sq = blk * blk
        tot = jnp.sum(sq, axis=-1, keepdims=True)
        lo = jnp.sum(jnp.where(low, sq, 0.0), axis=-1, keepdims=True)
        ms = jnp.where(low, lo, tot - lo) * (1.0 / HEAD_DIM)
        outs.append(blk * lax.rsqrt(ms + EPS) * gain[:, j * LANES:(j + 1) * LANES])
    return jnp.concatenate(outs, axis=-1)


def _block_diag(w):
    nb, n, _ = w.shape
    eye = jnp.eye(nb, dtype=w.dtype)
    return jnp.einsum('hij,hg->higj', w, eye).reshape(nb * n, nb * n)


def _rg_lru(xa, ga, c0_ref, h0_ref, cw_ref, cb_ref, wa_ref, ba_ref, wx_ref, bx_ref, lam_ref,
            ya_ref, hlast_ref, ctail_ref, ext_ref, hc_ref, a_ref, u_ref):
    t_len, w = xa.shape

    @pl.when(pl.program_id(1) == 0)
    def _():
        ext_ref[0:SUBLANES, :] = c0_ref[0]
        hc_ref[...] = jnp.broadcast_to(h0_ref[0], hc_ref.shape)

    ext_ref[SUBLANES:SUBLANES + t_len, :] = xa
    xc = cb_ref[...] + cw_ref[3:4, :] * xa
    for k in range(CONV_WIDTH - 1):
        off = SUBLANES - (CONV_WIDTH - 1) + k
        xc = xc + cw_ref[k:k + 1, :] * ext_ref[off:off + t_len, :]
    tail = xa[t_len - SUBLANES:, :]
    ext_ref[0:SUBLANES, :] = tail
    ctail_ref[0] = tail

    xcb = xc.astype(BF16)
    r = _sigmoid(jnp.dot(xcb, wa_ref[...], preferred_element_type=F32) + ba_ref[...])
    i = _sigmoid(jnp.dot(xcb, wx_ref[...], preferred_element_type=F32) + bx_ref[...])
    neg_lam = -lam_ref[...]
    softplus = jnp.maximum(neg_lam, 0.0) + jnp.log1p(jnp.exp(-jnp.abs(neg_lam)))
    log_a = -LRU_C * r * softplus
    a = jnp.exp(log_a)
    a_ref[...] = a
    u_ref[...] = jnp.sqrt(-jnp.tanh(log_a) * (a * a + 1.0)) * (i * xc)

    row = lax.broadcasted_iota(jnp.int32, (SUBLANES, w), 0)
    h = hc_ref[...]
    for blk in range(t_len // SUBLANES):
        rows = slice(blk * SUBLANES, (blk + 1) * SUBLANES)
        a = a_ref[rows, :]
        u = u_ref[rows, :]
        for d in (1, 2, 4):
            keep = row >= d
            a_sh = jnp.where(keep, pltpu.roll(a, d, 0), 1.0)
            u_sh = jnp.where(keep, pltpu.roll(u, d, 0), 0.0)
            u = u + a * u_sh
            a = a * a_sh
        hs = a * h + u
        ya_ref[0, rows, :] = (hs * jax.nn.gelu(ga[rows, :])).astype(ya_ref.dtype)
        h = jnp.broadcast_to(hs[SUBLANES - 1:SUBLANES, :], (SUBLANES, w))
    hc_ref[...] = h
    hlast_ref[0] = h


def _inproj_body(x_ref, g_ref, w_ref, qg_ref, kg_ref, c0_ref, h0_ref, cw_ref, cb_ref, wa_ref,
                 ba_ref, wx_ref, bx_ref, lam_ref, ya_ref, q_ref, k_ref, v_ref, kt_ref, vt_ref,
                 hlast_ref, ctail_ref, ext_ref, hc_ref, a_ref, u_ref):
    w = kt_ref.shape[2]
    h = _rms(x_ref[0], g_ref[...]).astype(BF16)
    pa = jnp.dot(h, w_ref[:, :2 * w], preferred_element_type=F32)
    _rg_lru(pa[:, :w], pa[:, w:], c0_ref, h0_ref, cw_ref, cb_ref, wa_ref, ba_ref, wx_ref, bx_ref,
            lam_ref, ya_ref, hlast_ref, ctail_ref, ext_ref, hc_ref, a_ref, u_ref)
    pb = jnp.dot(h, w_ref[:, 2 * w:], preferred_element_type=F32)
    q = _head_norm(pb[:, 0 * w:1 * w], qg_ref[...]) * (1.0 / math.sqrt(HEAD_DIM))
    k = _head_norm(pb[:, 1 * w:2 * w], kg_ref[...])
    v = pb[:, 2 * w:3 * w]
    kt_ref[0] = k
    vt_ref[0] = v
    for pr in range(w // LANES):
        lanes = slice(pr * LANES, (pr + 1) * LANES)
        q_ref[0, pr] = q[:, lanes].astype(BF16)
        k_ref[0, pr] = k[:, lanes].astype(BF16)
        v_ref[0, pr] = v[:, lanes].astype(BF16)


def _inproj(x, g, w_in, q_gain, k_gain, conv0, h0, conv_w, conv_b, wa, ba, wx, bx, lam, tm):
    b, s, d = x.shape
    w = w_in.shape[1] // 5
    heads = w // HEAD_DIM
    npair = w // LANES
    assert s % tm == 0 and tm % SUBLANES == 0
    pad = jnp.zeros((b, SUBLANES - (CONV_WIDTH - 1), w), F32)
    c0 = jnp.concatenate([pad, conv0.astype(F32)], axis=1)
    row = pl.BlockSpec((1, tm, w), lambda i, t: (i, t, 0))
    pair = pl.BlockSpec((1, npair, tm, LANES), lambda i, t: (i, 0, t, 0))
    tail = pl.BlockSpec((1, tm, w), lambda i, t: (i, 0, 0))
    state = pl.BlockSpec((1, SUBLANES, w), lambda i, t: (i, 0, 0))
    vec = _resident((1, w))
    return pl.pallas_call(
        _inproj_body,
        grid=(b, s // tm),
        in_specs=[
            pl.BlockSpec((1, tm, d), lambda i, t: (i, t, 0)),
            _resident((1, d)), _resident(w_in.shape), vec, vec,
            state, pl.BlockSpec((1, 1, w), lambda i, t: (i, 0, 0)),
            _resident((CONV_WIDTH, w)), vec, _resident((w, w)), vec, _resident((w, w)), vec, vec,
        ],
        out_specs=[row, pair, pair, pair, tail, tail, state, state],
        out_shape=[jax.ShapeDtypeStruct((b, s, w), BF16)]
        + [jax.ShapeDtypeStruct((b, npair, s, LANES), BF16)] * 3
        + [jax.ShapeDtypeStruct((b, tm, w), F32)] * 2
        + [jax.ShapeDtypeStruct((b, SUBLANES, w), F32)] * 2,
        scratch_shapes=[
            pltpu.VMEM((SUBLANES + tm, w), F32),
            pltpu.VMEM((SUBLANES, w), F32),
            pltpu.VMEM((tm, w), F32),
            pltpu.VMEM((tm, w), F32),
        ],
        compiler_params=_params("parallel", "arbitrary"),
        name="inproj",
    )(x, g.reshape(1, d), w_in.astype(BF16),
      jnp.tile(q_gain, heads).reshape(1, w), jnp.tile(k_gain, heads).reshape(1, w),
      c0, h0.astype(F32).reshape(b, 1, w), conv_w, conv_b.reshape(1, w),
      _block_diag(wa).astype(BF16), ba.reshape(1, w), _block_diag(wx).astype(BF16),
      bx.reshape(1, w), lam.reshape(1, w))


def _attn_body(*refs, n_prev, n_blocks):
    n_parts = n_prev + n_blocks
    q_ref = refs[0]
    k_parts = refs[1:1 + n_parts]
    v_parts = refs[1 + n_parts:1 + 2 * n_parts]
    bias_refs = refs[1 + 2 * n_parts:1 + 2 * n_parts + n_blocks]
    o_ref = refs[-1]
    npair = q_ref.shape[1]
    nq = q_ref.shape[2] // n_blocks
    low = lax.broadcasted_iota(jnp.int32, (nq, LANES), 1) < HEAD_DIM
    for pr in range(npair):
        for j in range(n_blocks):
            rows = slice(j * nq, (j + 1) * nq)
            qp = q_ref[0, pr, rows, :]
            zero = jnp.zeros_like(qp)
            qs = jnp.concatenate([jnp.where(low, qp, zero), jnp.where(low, zero, qp)], axis=0)
            kw = jnp.concatenate([part[0, pr] for part in k_parts[j:j + n_prev + 1]], axis=0)
            vw = jnp.concatenate([part[0, pr] for part in v_parts[j:j + n_prev + 1]], axis=0)
            s = lax.dot_general(qs, kw, (((1,), (1,)), ((), ())), preferred_element_type=F32)
            s = s + bias_refs[j][0, pr]
            p = jnp.exp(s - jnp.max(s, axis=-1, keepdims=True))
            denom = jnp.sum(p, axis=-1, keepdims=True)
            o = jnp.dot(p.astype(BF16), vw, preferred_element_type=F32) / denom
            o_ref[0, pr, rows, :] = jnp.where(low, o[:nq], o[nq:]).astype(o_ref.dtype)


def _band_bias(rel_bias, nq, invalid_cols):
    heads = rel_bias.shape[1]
    win = WINDOW + nq
    period = win + nq
    n_hi = WINDOW - MAX_REL
    n_lo = max(win - (WINDOW + MAX_REL + 1), 0)
    tab = rel_bias.astype(F32)
    mid = jnp.flip(tab, axis=0)[:win - n_hi - n_lo]
    vec = jnp.concatenate([jnp.broadcast_to(tab[-1:], (n_hi, heads)), mid,
                           jnp.broadcast_to(tab[:1], (n_lo, heads)),
                           jnp.broadcast_to(tab[-1:], (nq, heads))], axis=0)
    toep = jnp.tile(vec.T, (1, nq))[:, :nq * (period - 1)].reshape(heads, nq, period - 1)[:, :, :win]
    qc = jnp.arange(nq)[:, None] // CHUNK
    kc = jnp.arange(win)[None, :] // CHUNK
    band = (kc >= qc) & (kc <= qc + LEFT_CHUNKS)
    col = jnp.arange(win)[None, :]
    out = [jnp.where(band & (col >= c), toep, NEG_INF) for c in invalid_cols]
    return jnp.stack(out).reshape(len(invalid_cols), heads // 2, 2 * nq, win)


def _attn(q, k_parts, v_parts, bias, bias_maps, n_prev):
    b, npair, s, _ = q.shape
    n_blocks = len(bias_maps)
    assert len(k_parts) == len(v_parts) == n_prev + n_blocks
    rows = sum(r for _, r, _ in k_parts[n_prev:])

    def spec(nrows, index_map):
        return pl.BlockSpec((1, npair, nrows, LANES), index_map)

    own = spec(rows, lambda i, t: (i, 0, t, 0))
    return pl.pallas_call(
        functools.partial(_attn_body, n_prev=n_prev, n_blocks=n_blocks),
        grid=(b, s // rows),
        in_specs=([own] + [spec(r, m) for _, r, m in k_parts] + [spec(r, m) for _, r, m in v_parts]
                  + [pl.BlockSpec((1,) + bias.shape[1:], m) for m in bias_maps]),
        out_specs=own,
        out_shape=jax.ShapeDtypeStruct((b, npair, s, LANES), BF16),
        compiler_params=_params("parallel", "arbitrary"),
        name="attn",
    )(q, *[a for a, _, _ in k_parts], *[a for a, _, _ in v_parts], *([bias] * n_blocks))


def _outproj_body(x_ref, ya_ref, yb_ref, w_ref, o_ref):
    w = ya_ref.shape[2]
    yb = jnp.concatenate([yb_ref[0, pr] for pr in range(yb_ref.shape[1])], axis=-1)
    y = jnp.dot(ya_ref[0], w_ref[:w, :], preferred_element_type=F32)
    y = y + jnp.dot(yb, w_ref[w:, :], preferred_element_type=F32)
    o_ref[0] = x_ref[0] + y


def _outproj(x, ya, yb, w_out, tm):
    b, s, d = x.shape
    w = ya.shape[2]
    npair = yb.shape[1]
    return pl.pallas_call(
        _outproj_body,
        grid=(b, s // tm),
        in_specs=[
            pl.BlockSpec((1, tm, d), lambda i, t: (i, t, 0)),
            pl.BlockSpec((1, tm, w), lambda i, t: (i, t, 0)),
            pl.BlockSpec((1, npair, tm, LANES), lambda i, t: (i, 0, t, 0)),
            _resident(w_out.shape),
        ],
        out_specs=pl.BlockSpec((1, tm, d), lambda i, t: (i, t, 0)),
        out_shape=jax.ShapeDtypeStruct((b, s, d), F32),
        compiler_params=_params("parallel", "parallel"),
        name="outproj",
    )(x, ya, yb, w_out.astype(BF16))


def _s5_load_h(x_ref, g_ref, hnat, hperm):
    tm = x_ref.shape[2]
    pitch = hnat.shape[1] // SEGMENTS
    n_lane_blocks = hnat.shape[0]
    for r in range(SEGMENTS):
        h = _rms(x_ref[0, r], g_ref[...])
        for c in range(n_lane_blocks):
            hnat[c, r * pitch:r * pitch + tm, :] = h[:, c * LANES:(c + 1) * LANES]
    for m in range(tm):
        for c in range(n_lane_blocks):
            hperm[m * SEGMENTS:(m + 1) * SEGMENTS, c * LANES:(c + 1) * LANES] = (
                hnat[c, pl.ds(m, SEGMENTS, stride=pitch), :])


def _s5_pitch(tm):
    assert tm % SUBLANES == 0
    return tm + SUBLANES // 2


def _s5_scan(bu, sbuf, st, are_ref, aim_ref, jb, tm):
    half = bu.shape[1] // 2
    base = jb * bu.shape[1]
    step = 4 * LANES
    for c0 in range(0, half, step):
        ar = are_ref[jb, :, c0:c0 + step]
        ai = aim_ref[jb, :, c0:c0 + step]
        sr = st[:, base + c0:base + c0 + step]
        si = st[:, base + half + c0:base + half + c0 + step]
        for m in range(tm):
            rows = slice(m * SEGMENTS, (m + 1) * SEGMENTS)
            nr = ar * sr - ai * si + bu[rows, c0:c0 + step]
            ni = ar * si + ai * sr + bu[rows, half + c0:half + c0 + step]
            if sbuf is not None:
                sbuf[rows, c0:c0 + step] = nr
                sbuf[rows, half + c0:half + c0 + step] = ni
            sr, si = nr, ni
        st[:, base + c0:base + c0 + step] = sr
        st[:, base + half + c0:base + half + c0 + step] = si


def _s5_ends_body(x_ref, g_ref, bblk_ref, ajr_ref, aji_ref, ends_ref, hnat, hperm, bu0, bu1, st):
    tm, d = x_ref.shape[2], x_ref.shape[3]
    nb, kbj, _ = bblk_ref.shape
    kb = d // nb
    steps = kbj // kb
    pitch = hnat.shape[1] // SEGMENTS
    bus = (bu0, bu1)

    @pl.when(pl.program_id(1) == 0)
    def _():
        st[...] = jnp.zeros_like(st)

    for r in range(SEGMENTS):
        h = _rms(x_ref[0, r], g_ref[...])
        for c in range(hnat.shape[0]):
            hnat[c, r * pitch:r * pitch + tm, :] = h[:, c * LANES:(c + 1) * LANES]
    for mb in range(tm // steps):
        for j in range(steps):
            for c in range(hnat.shape[0]):
                jb, within = divmod(c * LANES, kb)
                dst = jb * kbj + j * kb + within
                hperm[mb * SEGMENTS:(mb + 1) * SEGMENTS, dst:dst + LANES] = (
                    hnat[c, pl.ds(mb * steps + j, SEGMENTS, stride=pitch), :])
    hb = hperm[...].astype(BF16)

    def b_u(jb):
        return jnp.dot(hb[:, jb * kbj:(jb + 1) * kbj], bblk_ref[jb], preferred_element_type=F32)

    bus[0][...] = b_u(0)
    for jb in range(nb):
        if jb + 1 < nb:
            bus[(jb + 1) % 2][...] = b_u(jb + 1)
        _s5_scan(bus[jb % 2], None, st, ajr_ref, aji_ref, jb, tm // steps)

    @pl.when(pl.program_id(1) == pl.num_programs(1) - 1)
    def _():
        ends_ref[0] = st[...]


def _s5_main_body(x_ref, g_ref, bblk_ref, cblk_ref, are_ref, aim_ref, apr_ref, api_ref, d_ref,
                  glu_ref, ends_ref, s0_ref, o_ref, slast_ref, hnat, hperm, bu0, bu1, sb0, sb1,
                  ybuf, st):
    tm = x_ref.shape[2]
    bus = (bu0, bu1)
    sbs = (sb0, sb1)
    d = x_ref.shape[3]
    nb, kb, two_half = bblk_ref.shape
    half = two_half // 2

    @pl.when(pl.program_id(1) == 0)
    def _():
        for jb in range(nb):
            re_cols = slice(jb * two_half, jb * two_half + half)
            im_cols = slice(jb * two_half + half, (jb + 1) * two_half)
            pr = apr_ref[jb, 0:1, :]
            pi = api_ref[jb, 0:1, :]
            er = s0_ref[0, :, re_cols]
            ei = s0_ref[0, :, im_cols]
            for r in range(SEGMENTS):
                st[r:r + 1, re_cols] = er
                st[r:r + 1, im_cols] = ei
                nr = pr * er - pi * ei + ends_ref[0, r:r + 1, re_cols]
                ni = pr * ei + pi * er + ends_ref[0, r:r + 1, im_cols]
                er, ei = nr, ni
            slast_ref[0, :, re_cols] = er
            slast_ref[0, :, im_cols] = ei

    _s5_load_h(x_ref, g_ref, hnat, hperm)
    h = hperm[...]
    hb = h.astype(BF16)

    def b_u(jb):
        return jnp.dot(hb[:, jb * kb:(jb + 1) * kb], bblk_ref[jb], preferred_element_type=F32)

    bus[0][...] = b_u(0)
    for jb in range(nb):
        if jb + 1 < nb:
            bus[(jb + 1) % 2][...] = b_u(jb + 1)
        _s5_scan(bus[jb % 2], sbs[jb % 2], st, are_ref, aim_ref, jb, tm)
        ybuf[:, jb * kb:(jb + 1) * kb] = jnp.dot(sbs[jb % 2][...].astype(BF16), cblk_ref[jb],
                                                 preferred_element_type=F32)
    y = ybuf[...] + d_ref[...] * h
    z = jnp.dot(y.astype(BF16), glu_ref[...], preferred_element_type=F32)
    o = z[:, :d] * _sigmoid(z[:, d:])
    for c in range(hnat.shape[0]):
        hnat[c, 0:SEGMENTS * tm, :] = o[:, c * LANES:(c + 1) * LANES]
    for r in range(SEGMENTS):
        for c in range(hnat.shape[0]):
            lanes = slice(c * LANES, (c + 1) * LANES)
            o_ref[0, r, :, lanes] = x_ref[0, r, :, lanes] + hnat[c, pl.ds(r, tm, stride=SEGMENTS), :]


def _s5_expand_body(bc_ref, cc_ref, bblk_ref, bends_ref, cblk_ref):
    n_b, _, kb, p = bc_ref.shape
    i_dim = cc_ref.shape[3]
    half = bblk_ref.shape[2] // 2
    log_p, log_i = p.bit_length() - 1, i_dim.bit_length() - 1
    assert p == 1 << log_p and i_dim == 1 << log_i

    def iota(shape, axis):
        return lax.broadcasted_iota(jnp.int32, shape, axis)

    sel_b = jnp.where((iota((p, half), 1) & (p - 1)) == iota((p, half), 0), 1.0, 0.0).astype(BF16)
    mask_b = (iota((kb, half), 0) >> log_i) == (iota((kb, half), 1) >> log_p)
    sel_c = jnp.where((iota((i_dim, kb), 1) & (i_dim - 1)) == iota((i_dim, kb), 0), 1.0, 0.0).astype(BF16)
    mask_c = (iota((half, kb), 0) >> log_p) == (iota((half, kb), 1) >> log_i)

    def expand_b(m):
        t = jnp.dot(bc_ref[m, 0].astype(BF16), sel_b, preferred_element_type=F32)
        return jnp.where(mask_b, t, 0.0).astype(BF16)

    bblk_ref[0, :, :half] = expand_b(0)
    bblk_ref[0, :, half:] = expand_b(1)
    for j in range((n_b - 2) // 2):
        bends_ref[0, j * kb:(j + 1) * kb, :half] = expand_b(2 + 2 * j)
        bends_ref[0, j * kb:(j + 1) * kb, half:] = expand_b(3 + 2 * j)
    for m in range(2):
        t = jnp.dot(cc_ref[m, 0].astype(BF16), sel_c, preferred_element_type=F32)
        cblk_ref[0, m * half:(m + 1) * half, :] = jnp.where(mask_c, t, 0.0).astype(BF16)


def _prep_s5(a_re, a_im, b_re, b_im, c_re, c_im, log_dt):
    g, p = a_re.shape
    gb = S5_GROUP_BLOCK
    nb = g // gb
    a = lax.complex(a_re.astype(F32), a_im.astype(F32))
    dt = jnp.exp(log_dt.astype(F32))[:, None]
    a_bar = jnp.exp(a * dt)
    b_bar = ((a_bar - 1.0) / a)[..., None] * lax.complex(b_re.astype(F32), b_im.astype(F32))
    mats = [b_bar]
    for _ in range(S5_ENDS_STEPS - 1):
        mats.append(mats[-1] * a_bar[..., None])
    mats = [b_bar] + mats[::-1]
    bc = jnp.stack([part for m in mats for part in (m.real, m.imag)])
    bc = bc.transpose(0, 1, 3, 2).reshape(len(mats) * 2, nb, gb * SSM_GROUP, p)
    cc = jnp.stack([c_re.astype(F32), -c_im.astype(F32)]).transpose(0, 1, 3, 2)
    cc = cc.reshape(2, nb, gb * p, SSM_GROUP)
    kb, half = gb * SSM_GROUP, gb * p
    bblk, bblk_ends, cblk = pl.pallas_call(
        _s5_expand_body,
        grid=(nb,),
        in_specs=[pl.BlockSpec((bc.shape[0], 1, kb, p), lambda j: (0, j, 0, 0)),
                  pl.BlockSpec((2, 1, half, SSM_GROUP), lambda j: (0, j, 0, 0))],
        out_specs=[pl.BlockSpec((1, kb, 2 * half), lambda j: (j, 0, 0)),
                   pl.BlockSpec((1, S5_ENDS_STEPS * kb, 2 * half), lambda j: (j, 0, 0)),
                   pl.BlockSpec((1, 2 * half, kb), lambda j: (j, 0, 0))],
        out_shape=[jax.ShapeDtypeStruct((nb, kb, 2 * half), BF16),
                   jax.ShapeDtypeStruct((nb, S5_ENDS_STEPS * kb, 2 * half), BF16),
                   jax.ShapeDtypeStruct((nb, 2 * half, kb), BF16)],
        compiler_params=_params("parallel"),
        name="s5_expand",
    )(bc, cc)
    return a_bar, bblk, bblk_ends, cblk


def _s5_lanes(v):
    g, p = v.shape
    nb = g // S5_GROUP_BLOCK
    return jnp.broadcast_to(v.reshape(nb, 1, S5_GROUP_BLOCK * p), (nb, SUBLANES, S5_GROUP_BLOCK * p))


def _s5_state_to_lanes(s_re, s_im):
    b, g, p = s_re.shape
    nb = g // S5_GROUP_BLOCK
    both = jnp.stack([s_re.reshape(b, nb, S5_GROUP_BLOCK * p), s_im.reshape(b, nb, S5_GROUP_BLOCK * p)],
                     axis=2)
    return both.reshape(b, 1, 2 * g * p).astype(F32)


def _s5_state_from_lanes(s, g, p):
    b = s.shape[0]
    both = s.reshape(b, g // S5_GROUP_BLOCK, 2, S5_GROUP_BLOCK, p)
    return both[:, :, 0].reshape(b, g, p), both[:, :, 1].reshape(b, g, p)


def _odd_mixer(x, norm_g, s_re, s_im, prep, d_skip, glu_w, tm):
    b, s, d = x.shape
    g, p = s_re.shape[1:]
    seg_len = s // SEGMENTS
    a_bar, bblk, bblk_ends, cblk = prep
    a_pow = a_bar
    for _ in range(int(math.log2(seg_len))):
        a_pow = a_pow * a_pow
    a_ends = a_bar
    for _ in range(S5_ENDS_STEPS - 1):
        a_ends = a_ends * a_bar
    are, aim, apr, api, ajr, aji = (_s5_lanes(v) for v in (a_bar.real, a_bar.imag, a_pow.real,
                                                            a_pow.imag, a_ends.real, a_ends.imag))
    nb, kb, two_half = bblk.shape
    n_state = nb * two_half
    xv = x.reshape(b, SEGMENTS, seg_len, d)
    rows = SEGMENTS * tm
    grid = (b, seg_len // tm)
    x_spec = pl.BlockSpec((1, SEGMENTS, tm, d), lambda i, t: (i, 0, t, 0))
    lane_spec = _resident(are.shape)
    g2 = norm_g.reshape(1, d)

    tm_e = min(seg_len, S5_ENDS_ROWS // SEGMENTS)
    assert seg_len % tm_e == 0 and tm_e % S5_ENDS_STEPS == 0
    ends = pl.pallas_call(
        _s5_ends_body,
        grid=(b, seg_len // tm_e),
        in_specs=[pl.BlockSpec((1, SEGMENTS, tm_e, d), lambda i, t: (i, 0, t, 0)), _resident((1, d)),
                  _resident(bblk_ends.shape), lane_spec, lane_spec],
        out_specs=pl.BlockSpec((1, SEGMENTS, n_state), lambda i, t: (i, 0, 0)),
        out_shape=jax.ShapeDtypeStruct((b, SEGMENTS, n_state), F32),
        scratch_shapes=[
            pltpu.VMEM((d // LANES, SEGMENTS * _s5_pitch(tm_e), LANES), F32),
            pltpu.VMEM((SEGMENTS * tm_e // S5_ENDS_STEPS, S5_ENDS_STEPS * d), F32),
            pltpu.VMEM((SEGMENTS * tm_e // S5_ENDS_STEPS, two_half), F32),
            pltpu.VMEM((SEGMENTS * tm_e // S5_ENDS_STEPS, two_half), F32),
            pltpu.VMEM((SEGMENTS, n_state), F32),
        ],
        compiler_params=_params("parallel", "arbitrary"),
        name="s5_ends",
    )(xv, g2, bblk_ends, ajr, aji)

    out, s_last = pl.pallas_call(
        _s5_main_body,
        grid=grid,
        in_specs=[
            x_spec, _resident((1, d)), _resident(bblk.shape), _resident(cblk.shape),
            lane_spec, lane_spec, lane_spec, lane_spec, _resident((1, d)),
            _resident(glu_w.shape),
            pl.BlockSpec((1, SEGMENTS, n_state), lambda i, t: (i, 0, 0)),
            pl.BlockSpec((1, 1, n_state), lambda i, t: (i, 0, 0)),
        ],
        out_specs=[x_spec, pl.BlockSpec((1, 1, n_state), lambda i, t: (i, 0, 0))],
        out_shape=[jax.ShapeDtypeStruct(xv.shape, F32),
                   jax.ShapeDtypeStruct((b, 1, n_state), F32)],
        scratch_shapes=[
            pltpu.VMEM((d // LANES, SEGMENTS * _s5_pitch(tm), LANES), F32), pltpu.VMEM((rows, d), F32),
            pltpu.VMEM((rows, two_half), F32), pltpu.VMEM((rows, two_half), F32),
            pltpu.VMEM((rows, two_half), F32), pltpu.VMEM((rows, two_half), F32),
            pltpu.VMEM((rows, d), F32), pltpu.VMEM((SEGMENTS, n_state), F32),
        ],
        compiler_params=_params("parallel", "arbitrary"),
        name="s5_main",
    )(xv, g2, bblk, cblk, are, aim, apr, api, d_skip.reshape(1, d).astype(F32),
      glu_w.astype(BF16), ends, _s5_state_to_lanes(s_re, s_im))
    n_re, n_im = _s5_state_from_lanes(s_last, g, p)
    return out.reshape(b, s, d), n_re, n_im


def _to_pairs(t):
    b, rows, heads, hd = t.shape
    return t.reshape(b, rows, heads // 2, 2 * hd).transpose(0, 2, 1, 3).astype(BF16)


def _even_mixer(x, p, e, conv0, h0, cache_k, cache_v, prompt):
    b, s, d = x.shape
    tm = min(s, WINDOW)
    ya, q, k, v, k_tail, v_tail, h_last, c_tail = _inproj(
        x, p['mix_norm_l'], p['ab_w_in'][e], p['q_norm'][e], p['k_norm'][e], conv0, h0,
        p['conv_w'][e], p['conv_b'][e], p['lru_wa'][e], p['lru_ba'][e], p['lru_wx'][e],
        p['lru_bx'][e], p['lru_lambda'][e], tm)
    w = ya.shape[2]
    heads = w // HEAD_DIM
    h_new = h_last[:, 0, :]
    conv_new = c_tail[:, SUBLANES - (CONV_WIDTH - 1):, :]
    if prompt:
        nq = ATTN_BLOCK
        n_prev = WINDOW // nq
        nblk = ATTN_BLOCKS_PER_STEP
        bias = _band_bias(p['rel_bias'][e], nq, [WINDOW - nq * v for v in range(n_prev + 1)])
        maps = [functools.partial(lambda i, t, off: (i, 0, jnp.maximum(nblk * t + off, 0), 0),
                                  off=idx - n_prev) for idx in range(n_prev + nblk)]
        bias_maps = [functools.partial(lambda i, t, j: (jnp.minimum(nblk * t + j, n_prev), 0, 0, 0), j=j)
                     for j in range(nblk)]
        yb = _attn(q, [(k, nq, m) for m in maps], [(v, nq, m) for m in maps], bias, bias_maps, n_prev)
    else:
        wc = cache_k.shape[1]
        assert wc == WINDOW and s == CHUNK and PAST_LEN % CHUNK == 0
        bias = _band_bias(p['rel_bias'][e], s, [0])
        first = lambda i, t: (i, 0, 0, 0)
        yb = _attn(q, [(_to_pairs(cache_k), wc, first), (k, s, first)],
                   [(_to_pairs(cache_v), wc, first), (v, s, first)], bias,
                   [lambda i, t: (0, 0, 0, 0)], 1)
    y = _outproj(x, ya, yb, p['ab_w_out'][e], tm)
    return (y, conv_new, h_new, k_tail.reshape(b, -1, heads, HEAD_DIM),
            v_tail.reshape(b, -1, heads, HEAD_DIM))


def _trunk(x, p, conv_st, lru_st, cache_k, cache_v, ssm_re_st, ssm_im_st, prompt):
    b, s, d = x.shape
    n = b * s
    tm = min(n, FFN_ROWS)
    depth = p['ffn1_norm'].shape[0]
    conv_out, lru_out, k_out, v_out, re_out, im_out = [], [], [], [], [], []
    for l in range(depth):
        x = _ffn(x.reshape(n, d), p['ffn1_norm'][l], *p['ffn1_w'], l, tm).reshape(b, s, d)
        if l % 2 == 0:
            e = l // 2
            w = p['conv_w'].shape[-1]
            if prompt:
                c_prev = jnp.zeros((b, CONV_WIDTH - 1, w), F32)
                h_prev = jnp.zeros((b, w), F32)
                ck = cv = None
            else:
                c_prev, h_prev, ck, cv = conv_st[e], lru_st[e], cache_k[e], cache_v[e]
            pe = dict(p, mix_norm_l=p['mix_norm'][l])
            x, c_new, h_new, k_new, v_new = _even_mixer(x, pe, e, c_prev, h_prev, ck, cv, prompt)
            conv_out.append(c_new)
            lru_out.append(h_new)
            k_out.append(k_new)
            v_out.append(v_new)
        else:
            o = l // 2
            g, st = p['ssm_A_re'].shape[1:]
            if prompt:
                s_re = jnp.zeros((b, g, st), F32)
                s_im = jnp.zeros((b, g, st), F32)
            else:
                s_re, s_im = ssm_re_st[o], ssm_im_st[o]
            x, n_re, n_im = _odd_mixer(x, p['mix_norm'][l], s_re, s_im, p['s5_prep'][o], p['ssm_D'][o],
                                       p['glu_w'][o], min(s // SEGMENTS, 64))
            re_out.append(n_re)
            im_out.append(n_im)
        x = _ffn(x.reshape(n, d), p['ffn2_norm'][l], *p['ffn2_w'], l, tm).reshape(b, s, d)
    return (x, jnp.stack(conv_out), jnp.stack(lru_out), jnp.stack(k_out), jnp.stack(v_out),
            jnp.stack(re_out), jnp.stack(im_out))


def kernel(x_prompt, x_sample, state_rglru_conv, state_rglru_h, cache_band_k, cache_band_v,
           state_ssm_re, state_ssm_im, ffn1_norm, ffn1_w_in, ffn1_w_out, mix_norm, ffn2_norm,
           ffn2_w_in, ffn2_w_out, ab_w_in, conv_w, conv_b, lru_wa, lru_ba, lru_wx, lru_bx,
           lru_lambda, q_norm, k_norm, rel_bias, ab_w_out, ssm_A_re, ssm_A_im, ssm_B_re, ssm_B_im,
           ssm_C_re, ssm_C_im, ssm_D, ssm_log_dt, glu_w):
    p = dict(ffn1_norm=ffn1_norm, mix_norm=mix_norm, ffn2_norm=ffn2_norm, ab_w_in=ab_w_in,
             conv_w=conv_w, conv_b=conv_b, lru_wa=lru_wa, lru_ba=lru_ba, lru_wx=lru_wx, lru_bx=lru_bx,
             lru_lambda=lru_lambda, q_norm=q_norm, k_norm=k_norm, rel_bias=rel_bias, ab_w_out=ab_w_out,
             ssm_A_re=ssm_A_re, ssm_A_im=ssm_A_im, ssm_B_re=ssm_B_re, ssm_B_im=ssm_B_im,
             ssm_C_re=ssm_C_re, ssm_C_im=ssm_C_im, ssm_D=ssm_D, ssm_log_dt=ssm_log_dt, glu_w=glu_w)
    p['ffn1_w'] = (ffn1_w_in.astype(BF16), ffn1_w_out.astype(BF16))
    p['ffn2_w'] = (ffn2_w_in.astype(BF16), ffn2_w_out.astype(BF16))
    p['s5_prep'] = [_prep_s5(ssm_A_re[o], ssm_A_im[o], ssm_B_re[o], ssm_B_im[o], ssm_C_re[o], ssm_C_im[o],
                             ssm_log_dt[o]) for o in range(ssm_A_re.shape[0])]
    y_prompt, p_conv, p_h, p_k, p_v, p_re, p_im = _trunk(
        x_prompt, p, None, None, None, None, None, None, True)
    y_sample, s_conv, s_h, s_k, s_v, s_re, s_im = _trunk(
        x_sample, p, state_rglru_conv, state_rglru_h, cache_band_k, cache_band_v,
        state_ssm_re, state_ssm_im, False)
    return (y_prompt, y_sample, p_conv, p_h, p_k, p_v, p_re, p_im, s_conv, s_h, s_k, s_v, s_re, s_im)
```

```python
import functools
import math

import jax
import jax.numpy as jnp
from jax import lax
from jax.experimental import pallas as pl
from jax.experimental.pallas import tpu as pltpu

F32 = jnp.float32
BF16 = jnp.bfloat16

LANES = 128
SUBLANES = 8
VMEM_LIMIT_BYTES = 56 * 1024 * 1024

EPS = 1e-6
CHUNK = 64
LEFT_CHUNKS = 8
WINDOW = LEFT_CHUNKS * CHUNK
BAND = WINDOW + CHUNK
MAX_REL = 128
PAST_LEN = 4096
HEAD_DIM = 64
CONV_WIDTH = 4
LRU_C = 8.0
SSM_GROUP = 16
SSM_STATE = 64
SEGMENTS = SUBLANES
S5_GROUP_BLOCK = 16
S5_ENDS_STEPS = 4
S5_ENDS_ROWS = 1024
FFN_CHUNK = 256
ATTN_BLOCK = 256
ATTN_BLOCKS_PER_STEP = 2
FFN_ROWS = 512
NEG_INF = -1e30


def _params(*sem):
    return pltpu.CompilerParams(dimension_semantics=sem, vmem_limit_bytes=VMEM_LIMIT_BYTES)


def _resident(shape):
    nd = len(shape)
    return pl.BlockSpec(shape, lambda *_: (0,) * nd, pipeline_mode=pl.Buffered(1))


def _rms(x, g):
    return x * lax.rsqrt(jnp.mean(x * x, axis=-1, keepdims=True) + EPS) * g


def _sigmoid(x):
    return 1.0 / (1.0 + jnp.exp(-x))


def _ffn_body(xa_ref, xb_ref, g_ref, win_ref, wout_ref, oa_ref, ob_ref, acc_ref):
    d_ff = wout_ref.shape[0]
    first = pl.program_id(0) == 0
    x = jnp.where(first, xb_ref[...], xa_ref[...])
    h = _rms(x, g_ref[...]).astype(BF16)
    for c in range(d_ff // FFN_CHUNK):
        cols = slice(c * FFN_CHUNK, (c + 1) * FFN_CHUNK)
        up_cols = slice(d_ff + c * FFN_CHUNK, d_ff + (c + 1) * FFN_CHUNK)
        gate = jnp.dot(h, win_ref[:, cols], preferred_element_type=F32)
        up = jnp.dot(h, win_ref[:, up_cols], preferred_element_type=F32)
        act = (gate * _sigmoid(gate) * up).astype(BF16)
        y = jnp.dot(act, wout_ref[cols, :], preferred_element_type=F32)
        if c == 0:
            acc_ref[...] = y
        else:
            acc_ref[...] += y
    oa_ref[...] = x + 0.5 * acc_ref[...]

    @pl.when(first)
    def _():
        ob_ref[...] = xb_ref[...] + 0.5 * acc_ref[...]


def _ffn(xa, xb, g, w_in_all, w_out_all, layer):
    n, d = xa.shape
    tm = FFN_ROWS
    _, d_ff, _ = w_out_all.shape
    assert d_ff % FFN_CHUNK == 0 and n % tm == 0 and xb.shape == (tm, d)
    a_spec = pl.BlockSpec((tm, d), lambda i: (jnp.maximum(i - 1, 0), 0))
    b_spec = pl.BlockSpec((tm, d), lambda i: (0, 0))
    return pl.pallas_call(
        _ffn_body,
        grid=(n // tm + 1,),
        in_specs=[
            a_spec, b_spec, _resident((1, d)),
            pl.BlockSpec((None, d, 2 * d_ff), lambda i: (layer, 0, 0), pipeline_mode=pl.Buffered(1)),
            pl.BlockSpec((None, d_ff, d), lambda i: (layer, 0, 0), pipeline_mode=pl.Buffered(1)),
        ],
        out_specs=[a_spec, b_spec],
        out_shape=[jax.ShapeDtypeStruct((n, d), F32), jax.ShapeDtypeStruct((tm, d), F32)],
        scratch_shapes=[pltpu.VMEM((tm, d), F32)],
        compiler_params=_params("arbitrary"),
        name="ffn",
    )(xa, xb, g.reshape(1, d), w_in_all, w_out_all)


def _head_norm(t, gain):
    low = lax.broadcasted_iota(jnp.int32, (t.shape[0], LANES), 1) < HEAD_DIM
    outs = []
    for j in range(t.shape[1] // LANES):
        blk = t[:, j * LANES:(j + 1) * LANES]
        sq = blk * blk
        tot = jnp.sum(sq, axis=-1, keepdims=True)
        lo = jnp.sum(jnp.where(low, sq, 0.0), axis=-1, keepdims=True)
        ms = jnp.where(low, lo, tot - lo) * (1.0 / HEAD_DIM)
        outs.append(blk * lax.rsqrt(ms + EPS) * gain[:, j * LANES:(j + 1) * LANES])
    return jnp.concatenate(outs, axis=-1)


def _block_diag(w):
    nb, n, _ = w.shape
    eye = jnp.eye(nb, dtype=w.dtype)
    return jnp.einsum('hij,hg->higj', w, eye).reshape(nb * n, nb * n)


def _rg_lru(xa, ga, c0_ref, h0_ref, cw_ref, cb_ref, wa_ref, ba_ref, wx_ref, bx_ref, lam_ref,
            ya_ref, hlast_ref, ctail_ref, ext_ref, hc_ref, a_ref, u_ref):
    t_len, w = xa.shape

    @pl.when(pl.program_id(1) == 0)
    def _():
        ext_ref[0:SUBLANES, :] = c0_ref[0]
        hc_ref[...] = jnp.broadcast_to(h0_ref[0], hc_ref.shape)

    ext_ref[SUBLANES:SUBLANES + t_len, :] = xa
    xc = cb_ref[...] + cw_ref[3:4, :] * xa
    for k in range(CONV_WIDTH - 1):
        off = SUBLANES - (CONV_WIDTH - 1) + k
        xc = xc + cw_ref[k:k + 1, :] * ext_ref[off:off + t_len, :]
    tail = xa[t_len - SUBLANES:, :]
    ext_ref[0:SUBLANES, :] = tail
    ctail_ref[0] = tail

    xcb = xc.astype(BF16)
    r = _sigmoid(jnp.dot(xcb, wa_ref[...], preferred_element_type=F32) + ba_ref[...])
    i = _sigmoid(jnp.dot(xcb, wx_ref[...], preferred_element_type=F32) + bx_ref[...])
    neg_lam = -lam_ref[...]
    softplus = jnp.maximum(neg_lam, 0.0) + jnp.log1p(jnp.exp(-jnp.abs(neg_lam)))
    log_a = -LRU_C * r * softplus
    a = jnp.exp(log_a)
    a_ref[...] = a
    u_ref[...] = jnp.sqrt(-jnp.tanh(log_a) * (a * a + 1.0)) * (i * xc)

    row = lax.broadcasted_iota(jnp.int32, (SUBLANES, w), 0)
    h = hc_ref[...]
    for blk in range(t_len // SUBLANES):
        rows = slice(blk * SUBLANES, (blk + 1) * SUBLANES)
        a = a_ref[rows, :]
        u = u_ref[rows, :]
        for d in (1, 2, 4):
            keep = row >= d
            a_sh = jnp.where(keep, pltpu.roll(a, d, 0), 1.0)
            u_sh = jnp.where(keep, pltpu.roll(u, d, 0), 0.0)
            u = u + a * u_sh
            a = a * a_sh
        hs = a * h + u
        ya_ref[0, rows, :] = (hs * jax.nn.gelu(ga[rows, :])).astype(ya_ref.dtype)
        h = jnp.broadcast_to(hs[SUBLANES - 1:SUBLANES, :], (SUBLANES, w))
    hc_ref[...] = h
    hlast_ref[0] = h


def _inproj_body(x_ref, g_ref, w_ref, qg_ref, kg_ref, c0_ref, h0_ref, cw_ref, cb_ref, wa_ref,
                 ba_ref, wx_ref, bx_ref, lam_ref, ya_ref, q_ref, k_ref, v_ref, kt_ref, vt_ref,
                 hlast_ref, ctail_ref, ext_ref, hc_ref, a_ref, u_ref):
    w = kt_ref.shape[2]
    h = _rms(x_ref[0], g_ref[...]).astype(BF16)
    pa = jnp.dot(h, w_ref[:, :2 * w], preferred_element_type=F32)
    _rg_lru(pa[:, :w], pa[:, w:], c0_ref, h0_ref, cw_ref, cb_ref, wa_ref, ba_ref, wx_ref, bx_ref,
            lam_ref, ya_ref, hlast_ref, ctail_ref, ext_ref, hc_ref, a_ref, u_ref)
    pb = jnp.dot(h, w_ref[:, 2 * w:], preferred_element_type=F32)
    q = _head_norm(pb[:, 0 * w:1 * w], qg_ref[...]) * (1.0 / math.sqrt(HEAD_DIM))
    k = _head_norm(pb[:, 1 * w:2 * w], kg_ref[...])
    v = pb[:, 2 * w:3 * w]
    kt_ref[0] = k
    vt_ref[0] = v
    for pr in range(w // LANES):
        lanes = slice(pr * LANES, (pr + 1) * LANES)
        q_ref[0, pr] = q[:, lanes].astype(BF16)
        k_ref[0, pr] = k[:, lanes].astype(BF16)
        v_ref[0, pr] = v[:, lanes].astype(BF16)


def _inproj(x, g, w_in, q_gain, k_gain, conv0, h0, conv_w, conv_b, wa, ba, wx, bx, lam, tm):
    b, s, d = x.shape
    w = w_in.shape[1] // 5
    heads = w // HEAD_DIM
    npair = w // LANES
    assert s % tm == 0 and tm % SUBLANES == 0
    pad = jnp.zeros((b, SUBLANES - (CONV_WIDTH - 1), w), F32)
    c0 = jnp.concatenate([pad, conv0.astype(F32)], axis=1)
    row = pl.BlockSpec((1, tm, w), lambda i, t: (i, t, 0))
    pair = pl.BlockSpec((1, npair, tm, LANES), lambda i, t: (i, 0, t, 0))
    tail = pl.BlockSpec((1, tm, w), lambda i, t: (i, 0, 0))
    state = pl.BlockSpec((1, SUBLANES, w), lambda i, t: (i, 0, 0))
    vec = _resident((1, w))
    return pl.pallas_call(
        _inproj_body,
        grid=(b, s // tm),
        in_specs=[
            pl.BlockSpec((1, tm, d), lambda i, t: (i, t, 0)),
            _resident((1, d)), _resident(w_in.shape), vec, vec,
            state, pl.BlockSpec((1, 1, w), lambda i, t: (i, 0, 0)),
            _resident((CONV_WIDTH, w)), vec, _resident((w, w)), vec, _resident((w, w)), vec, vec,
        ],
        out_specs=[row, pair, pair, pair, tail, tail, state, state],
        out_shape=[jax.ShapeDtypeStruct((b, s, w), BF16)]
        + [jax.ShapeDtypeStruct((b, npair, s, LANES), BF16)] * 3
        + [jax.ShapeDtypeStruct((b, tm, w), F32)] * 2
        + [jax.ShapeDtypeStruct((b, SUBLANES, w), F32)] * 2,
        scratch_shapes=[
            pltpu.VMEM((SUBLANES + tm, w), F32),
            pltpu.VMEM((SUBLANES, w), F32),
            pltpu.VMEM((tm, w), F32),
            pltpu.VMEM((tm, w), F32),
        ],
        compiler_params=_params("parallel", "arbitrary"),
        name="inproj",
    )(x, g.reshape(1, d), w_in.astype(BF16),
      jnp.tile(q_gain, heads).reshape(1, w), jnp.tile(k_gain, heads).reshape(1, w),
      c0, h0.astype(F32).reshape(b, 1, w), conv_w, conv_b.reshape(1, w),
      _block_diag(wa).astype(BF16), ba.reshape(1, w), _block_diag(wx).astype(BF16),
      bx.reshape(1, w), lam.reshape(1, w))


def _attn_body(*refs, n_prev, n_blocks):
    n_parts = n_prev + n_blocks
    q_ref = refs[0]
    k_parts = refs[1:1 + n_parts]
    v_parts = refs[1 + n_parts:1 + 2 * n_parts]
    bias_refs = refs[1 + 2 * n_parts:1 + 2 * n_parts + n_blocks]
    o_ref = refs[-1]
    npair = q_ref.shape[1]
    nq = q_ref.shape[2] // n_blocks
    low = lax.broadcasted_iota(jnp.int32, (nq, LANES), 1) < HEAD_DIM
    for pr in range(npair):
        for j in range(n_blocks):
            rows = slice(j * nq, (j + 1) * nq)
            qp = q_ref[0, pr, rows, :]
            zero = jnp.zeros_like(qp)
            qs = jnp.concatenate([jnp.where(low, qp, zero), jnp.where(low, zero, qp)], axis=0)
            kw = jnp.concatenate([part[0, pr] for part in k_parts[j:j + n_prev + 1]], axis=0)
            vw = jnp.concatenate([part[0, pr] for part in v_parts[j:j + n_prev + 1]], axis=0)
            s = lax.dot_general(qs, kw, (((1,), (1,)), ((), ())), preferred_element_type=F32)
            s = s + bias_refs[j][0, pr]
            p = jnp.exp(s - jnp.max(s, axis=-1, keepdims=True))
            denom = jnp.sum(p, axis=-1, keepdims=True)
            o = jnp.dot(p.astype(BF16), vw, preferred_element_type=F32) / denom
            o_ref[0, pr, rows, :] = jnp.where(low, o[:nq], o[nq:]).astype(o_ref.dtype)


def _band_bias(rel_bias, nq, invalid_cols):
    heads = rel_bias.shape[1]
    win = WINDOW + nq
    period = win + nq
    n_hi = WINDOW - MAX_REL
    n_lo = max(win - (WINDOW + MAX_REL + 1), 0)
    tab = rel_bias.astype(F32)
    mid = jnp.flip(tab, axis=0)[:win - n_hi - n_lo]
    vec = jnp.concatenate([jnp.broadcast_to(tab[-1:], (n_hi, heads)), mid,
                           jnp.broadcast_to(tab[:1], (n_lo, heads)),
                           jnp.broadcast_to(tab[-1:], (nq, heads))], axis=0)
    toep = jnp.tile(vec.T, (1, nq))[:, :nq * (period - 1)].reshape(heads, nq, period - 1)[:, :, :win]
    qc = jnp.arange(nq)[:, None] // CHUNK
    kc = jnp.arange(win)[None, :] // CHUNK
    band = (kc >= qc) & (kc <= qc + LEFT_CHUNKS)
    col = jnp.arange(win)[None, :]
    out = [jnp.where(band & (col >= c), toep, NEG_INF) for c in invalid_cols]
    return jnp.stack(out).reshape(len(invalid_cols), heads // 2, 2 * nq, win)


def _attn(q, k_parts, v_parts, bias, bias_maps, n_prev):
    b, npair, s, _ = q.shape
    n_blocks = len(bias_maps)
    assert len(k_parts) == len(v_parts) == n_prev + n_blocks
    rows = sum(r for _, r, _ in k_parts[n_prev:])

    def spec(nrows, index_map):
        return pl.BlockSpec((1, npair, nrows, LANES), index_map)

    own = spec(rows, lambda i, t: (i, 0, t, 0))
    return pl.pallas_call(
        functools.partial(_attn_body, n_prev=n_prev, n_blocks=n_blocks),
        grid=(b, s // rows),
        in_specs=([own] + [spec(r, m) for _, r, m in k_parts] + [spec(r, m) for _, r, m in v_parts]
                  + [pl.BlockSpec((1,) + bias.shape[1:], m) for m in bias_maps]),
        out_specs=own,
        out_shape=jax.ShapeDtypeStruct((b, npair, s, LANES), BF16),
        compiler_params=_params("parallel", "arbitrary"),
        name="attn",
    )(q, *[a for a, _, _ in k_parts], *[a for a, _, _ in v_parts], *([bias] * n_blocks))


def _outproj_body(x_ref, ya_ref, yb_ref, w_ref, o_ref):
    w = ya_ref.shape[2]
    yb = jnp.concatenate([yb_ref[0, pr] for pr in range(yb_ref.shape[1])], axis=-1)
    y = jnp.dot(ya_ref[0], w_ref[:w, :], preferred_element_type=F32)
    y = y + jnp.dot(yb, w_ref[w:, :], preferred_element_type=F32)
    o_ref[0] = x_ref[0] + y


def _outproj(x, ya, yb, w_out, tm):
    b, s, d = x.shape
    w = ya.shape[2]
    npair = yb.shape[1]
    return pl.pallas_call(
        _outproj_body,
        grid=(b, s // tm),
        in_specs=[
            pl.BlockSpec((1, tm, d), lambda i, t: (i, t, 0)),
            pl.BlockSpec((1, tm, w), lambda i, t: (i, t, 0)),
            pl.BlockSpec((1, npair, tm, LANES), lambda i, t: (i, 0, t, 0)),
            _resident(w_out.shape),
        ],
        out_specs=pl.BlockSpec((1, tm, d), lambda i, t: (i, t, 0)),
        out_shape=jax.ShapeDtypeStruct((b, s, d), F32),
        compiler_params=_params("parallel", "parallel"),
        name="outproj",
    )(x, ya, yb, w_out.astype(BF16))


def _s5_load_h(x_ref, g_ref, hnat, hperm):
    tm = x_ref.shape[2]
    pitch = hnat.shape[1] // SEGMENTS
    n_lane_blocks = hnat.shape[0]
    for r in range(SEGMENTS):
        h = _rms(x_ref[0, r], g_ref[...])
        for c in range(n_lane_blocks):
            hnat[c, r * pitch:r * pitch + tm, :] = h[:, c * LANES:(c + 1) * LANES]
    for m in range(tm):
        for c in range(n_lane_blocks):
            hperm[m * SEGMENTS:(m + 1) * SEGMENTS, c * LANES:(c + 1) * LANES] = (
                hnat[c, pl.ds(m, SEGMENTS, stride=pitch), :])


def _s5_pitch(tm):
    assert tm % SUBLANES == 0
    return tm + SUBLANES // 2


def _s5_scan(bu, sbuf, st, are_ref, aim_ref, jb, tm):
    half = bu.shape[1] // 2
    base = jb * bu.shape[1]
    step = 4 * LANES
    for c0 in range(0, half, step):
        ar = are_ref[jb, :, c0:c0 + step]
        ai = aim_ref[jb, :, c0:c0 + step]
        sr = st[:, base + c0:base + c0 + step]
        si = st[:, base + half + c0:base + half + c0 + step]
        for m in range(tm):
            rows = slice(m * SEGMENTS, (m + 1) * SEGMENTS)
            nr = ar * sr - ai * si + bu[rows, c0:c0 + step]
            ni = ar * si + ai * sr + bu[rows, half + c0:half + c0 + step]
            if sbuf is not None:
                sbuf[rows, c0:c0 + step] = nr
                sbuf[rows, half + c0:half + c0 + step] = ni
            sr, si = nr, ni
        st[:, base + c0:base + c0 + step] = sr
        st[:, base + half + c0:base + half + c0 + step] = si


def _s5_ends_body(x_ref, g_ref, bblk_ref, ajr_ref, aji_ref, ends_ref, hnat, hperm, bu0, bu1, st):
    tm, d = x_ref.shape[2], x_ref.shape[3]
    nb, kbj, _ = bblk_ref.shape
    kb = d // nb
    steps = kbj // kb
    pitch = hnat.shape[1] // SEGMENTS
    bus = (bu0, bu1)

    @pl.when(pl.program_id(1) == 0)
    def _():
        st[...] = jnp.zeros_like(st)

    for r in range(SEGMENTS):
        h = _rms(x_ref[0, r], g_ref[...])
        for c in range(hnat.shape[0]):
            hnat[c, r * pitch:r * pitch + tm, :] = h[:, c * LANES:(c + 1) * LANES]
    for mb in range(tm // steps):
        for j in range(steps):
            for c in range(hnat.shape[0]):
                jb, within = divmod(c * LANES, kb)
                dst = jb * kbj + j * kb + within
                hperm[mb * SEGMENTS:(mb + 1) * SEGMENTS, dst:dst + LANES] = (
                    hnat[c, pl.ds(mb * steps + j, SEGMENTS, stride=pitch), :])
    hb = hperm[...].astype(BF16)

    def b_u(jb):
        return jnp.dot(hb[:, jb * kbj:(jb + 1) * kbj], bblk_ref[jb], preferred_element_type=F32)

    bus[0][...] = b_u(0)
    for jb in range(nb):
        if jb + 1 < nb:
            bus[(jb + 1) % 2][...] = b_u(jb + 1)
        _s5_scan(bus[jb % 2], None, st, ajr_ref, aji_ref, jb, tm // steps)

    @pl.when(pl.program_id(1) == pl.num_programs(1) - 1)
    def _():
        ends_ref[0] = st[...]


def _s5_main_body(x_ref, g_ref, bblk_ref, cblk_ref, are_ref, aim_ref, apr_ref, api_ref, d_ref,
                  glu_ref, ends_ref, s0_ref, o_ref, slast_ref, hnat, hperm, bu0, bu1, sb0, sb1,
                  ybuf, st):
    tm = x_ref.shape[2]
    bus = (bu0, bu1)
    sbs = (sb0, sb1)
    d = x_ref.shape[3]
    nb, kb, two_half = bblk_ref.shape
    half = two_half // 2

    @pl.when(pl.program_id(1) == 0)
    def _():
        for jb in range(nb):
            re_cols = slice(jb * two_half, jb * two_half + half)
            im_cols = slice(jb * two_half + half, (jb + 1) * two_half)
            pr = apr_ref[jb, 0:1, :]
            pi = api_ref[jb, 0:1, :]
            er = s0_ref[0, :, re_cols]
            ei = s0_ref[0, :, im_cols]
            for r in range(SEGMENTS):
                st[r:r + 1, re_cols] = er
                st[r:r + 1, im_cols] = ei
                nr = pr * er - pi * ei + ends_ref[0, r:r + 1, re_cols]
                ni = pr * ei + pi * er + ends_ref[0, r:r + 1, im_cols]
                er, ei = nr, ni
            slast_ref[0, :, re_cols] = er
            slast_ref[0, :, im_cols] = ei

    _s5_load_h(x_ref, g_ref, hnat, hperm)
    h = hperm[...]
    hb = h.astype(BF16)

    def b_u(jb):
        return jnp.dot(hb[:, jb * kb:(jb + 1) * kb], bblk_ref[jb], preferred_element_type=F32)

    bus[0][...] = b_u(0)
    for jb in range(nb):
        if jb + 1 < nb:
            bus[(jb + 1) % 2][...] = b_u(jb + 1)
        _s5_scan(bus[jb % 2], sbs[jb % 2], st, are_ref, aim_ref, jb, tm)
        ybuf[:, jb * kb:(jb + 1) * kb] = jnp.dot(sbs[jb % 2][...].astype(BF16), cblk_ref[jb],
                                                 preferred_element_type=F32)
    y = ybuf[...] + d_ref[...] * h
    z = jnp.dot(y.astype(BF16), glu_ref[...], preferred_element_type=F32)
    o = z[:, :d] * _sigmoid(z[:, d:])
    for c in range(hnat.shape[0]):
        hnat[c, 0:SEGMENTS * tm, :] = o[:, c * LANES:(c + 1) * LANES]
    for r in range(SEGMENTS):
        for c in range(hnat.shape[0]):
            lanes = slice(c * LANES, (c + 1) * LANES)
            o_ref[0, r, :, lanes] = x_ref[0, r, :, lanes] + hnat[c, pl.ds(r, tm, stride=SEGMENTS), :]


def _s5_expand_body(bc_ref, cc_ref, bblk_ref, bends_ref, cblk_ref):
    n_b, _, kb, p = bc_ref.shape
    i_dim = cc_ref.shape[3]
    half = bblk_ref.shape[2] // 2
    log_p, log_i = p.bit_length() - 1, i_dim.bit_length() - 1
    assert p == 1 << log_p and i_dim == 1 << log_i

    def iota(shape, axis):
        return lax.broadcasted_iota(jnp.int32, shape, axis)

    sel_b = jnp.where((iota((p, half), 1) & (p - 1)) == iota((p, half), 0), 1.0, 0.0).astype(BF16)
    mask_b = (iota((kb, half), 0) >> log_i) == (iota((kb, half), 1) >> log_p)
    sel_c = jnp.where((iota((i_dim, kb), 1) & (i_dim - 1)) == iota((i_dim, kb), 0), 1.0, 0.0).astype(BF16)
    mask_c = (iota((half, kb), 0) >> log_p) == (iota((half, kb), 1) >> log_i)

    def expand_b(m):
        t = jnp.dot(bc_ref[m, 0].astype(BF16), sel_b, preferred_element_type=F32)
        return jnp.where(mask_b, t, 0.0).astype(BF16)

    bblk_ref[0, :, :half] = expand_b(0)
    bblk_ref[0, :, half:] = expand_b(1)
    for j in range((n_b - 2) // 2):
        bends_ref[0, j * kb:(j + 1) * kb, :half] = expand_b(2 + 2 * j)
        bends_ref[0, j * kb:(j + 1) * kb, half:] = expand_b(3 + 2 * j)
    for m in range(2):
        t = jnp.dot(cc_ref[m, 0].astype(BF16), sel_c, preferred_element_type=F32)
        cblk_ref[0, m * half:(m + 1) * half, :] = jnp.where(mask_c, t, 0.0).astype(BF16)


def _prep_s5(a_re, a_im, b_re, b_im, c_re, c_im, log_dt):
    g, p = a_re.shape
    gb = S5_GROUP_BLOCK
    nb = g // gb
    a = lax.complex(a_re.astype(F32), a_im.astype(F32))
    dt = jnp.exp(log_dt.astype(F32))[:, None]
    a_bar = jnp.exp(a * dt)
    b_bar = ((a_bar - 1.0) / a)[..., None] * lax.complex(b_re.astype(F32), b_im.astype(F32))
    mats = [b_bar]
    for _ in range(S5_ENDS_STEPS - 1):
        mats.append(mats[-1] * a_bar[..., None])
    mats = [b_bar] + mats[::-1]
    bc = jnp.stack([part for m in mats for part in (m.real, m.imag)])
    bc = bc.transpose(0, 1, 3, 2).reshape(len(mats) * 2, nb, gb * SSM_GROUP, p)
    cc = jnp.stack([c_re.astype(F32), -c_im.astype(F32)]).transpose(0, 1, 3, 2)
    cc = cc.reshape(2, nb, gb * p, SSM_GROUP)
    kb, half = gb * SSM_GROUP, gb * p
    bblk, bblk_ends, cblk = pl.pallas_call(
        _s5_expand_body,
        grid=(nb,),
        in_specs=[pl.BlockSpec((bc.shape[0], 1, kb, p), lambda j: (0, j, 0, 0)),
                  pl.BlockSpec((2, 1, half, SSM_GROUP), lambda j: (0, j, 0, 0))],
        out_specs=[pl.BlockSpec((1, kb, 2 * half), lambda j: (j, 0, 0)),
                   pl.BlockSpec((1, S5_ENDS_STEPS * kb, 2 * half), lambda j: (j, 0, 0)),
                   pl.BlockSpec((1, 2 * half, kb), lambda j: (j, 0, 0))],
        out_shape=[jax.ShapeDtypeStruct((nb, kb, 2 * half), BF16),
                   jax.ShapeDtypeStruct((nb, S5_ENDS_STEPS * kb, 2 * half), BF16),
                   jax.ShapeDtypeStruct((nb, 2 * half, kb), BF16)],
        compiler_params=_params("parallel"),
        name="s5_expand",
    )(bc, cc)
    return a_bar, bblk, bblk_ends, cblk


def _s5_lanes(v):
    g, p = v.shape
    nb = g // S5_GROUP_BLOCK
    return jnp.broadcast_to(v.reshape(nb, 1, S5_GROUP_BLOCK * p), (nb, SUBLANES, S5_GROUP_BLOCK * p))


def _s5_seq_body(x_ref, g_ref, bblk_ref, cblk_ref, are_ref, aim_ref, d_ref, glu_ref, s0_ref,
                 o_ref, slast_ref, hnat, hperm, bu0, bu1, sb0, sb1, ybuf, st):
    nseq, tt, d = x_ref.shape
    nb, kb, _ = bblk_ref.shape
    pitch = hnat.shape[1] // nseq
    bus = (bu0, bu1)
    sbs = (sb0, sb1)
    st[...] = s0_ref[...]
    for b in range(nseq):
        hb_nat = _rms(x_ref[b], g_ref[...])
        for c in range(hnat.shape[0]):
            hnat[c, b * pitch:b * pitch + tt, :] = hb_nat[:, c * LANES:(c + 1) * LANES]
    for t in range(tt):
        for c in range(hnat.shape[0]):
            hperm[t * nseq:(t + 1) * nseq, c * LANES:(c + 1) * LANES] = (
                hnat[c, pl.ds(t, nseq, stride=pitch), :])
    h = hperm[...]
    hb = h.astype(BF16)

    def b_u(jb):
        return jnp.dot(hb[:, jb * kb:(jb + 1) * kb], bblk_ref[jb], preferred_element_type=F32)

    bus[0][...] = b_u(0)
    for jb in range(nb):
        if jb + 1 < nb:
            bus[(jb + 1) % 2][...] = b_u(jb + 1)
        _s5_scan(bus[jb % 2], sbs[jb % 2], st, are_ref, aim_ref, jb, tt)
        ybuf[:, jb * kb:(jb + 1) * kb] = jnp.dot(sbs[jb % 2][...].astype(BF16), cblk_ref[jb],
                                                 preferred_element_type=F32)
    y = ybuf[...] + d_ref[...] * h
    z = jnp.dot(y.astype(BF16), glu_ref[...], preferred_element_type=F32)
    o = z[:, :d] * _sigmoid(z[:, d:])
    for c in range(hnat.shape[0]):
        hnat[c, 0:nseq * tt, :] = o[:, c * LANES:(c + 1) * LANES]
    for b in range(nseq):
        for c in range(hnat.shape[0]):
            lanes = slice(c * LANES, (c + 1) * LANES)
            o_ref[b, :, lanes] = x_ref[b, :, lanes] + hnat[c, pl.ds(b, tt, stride=nseq), :]
    slast_ref[...] = st[...]


def _s5_state_to_lanes(s_re, s_im):
    b, g, p = s_re.shape
    nb = g // S5_GROUP_BLOCK
    both = jnp.stack([s_re.reshape(b, nb, S5_GROUP_BLOCK * p), s_im.reshape(b, nb, S5_GROUP_BLOCK * p)],
                     axis=2)
    return both.reshape(b, 1, 2 * g * p).astype(F32)


def _s5_state_from_lanes(s, g, p):
    b = s.shape[0]
    both = s.reshape(b, g // S5_GROUP_BLOCK, 2, S5_GROUP_BLOCK, p)
    return both[:, :, 0].reshape(b, g, p), both[:, :, 1].reshape(b, g, p)


def _odd_mixer(x, norm_g, s_re, s_im, prep, d_skip, glu_w, tm):
    b, s, d = x.shape
    g, p = s_re.shape[1:]
    seg_len = s // SEGMENTS
    a_bar, bblk, bblk_ends, cblk = prep
    a_pow = a_bar
    for _ in range(int(math.log2(seg_len))):
        a_pow = a_pow * a_pow
    a_ends = a_bar
    for _ in range(S5_ENDS_STEPS - 1):
        a_ends = a_ends * a_bar
    are, aim, apr, api, ajr, aji = (_s5_lanes(v) for v in (a_bar.real, a_bar.imag, a_pow.real,
                                                            a_pow.imag, a_ends.real, a_ends.imag))
    nb, kb, two_half = bblk.shape
    n_state = nb * two_half
    xv = x.reshape(b, SEGMENTS, seg_len, d)
    rows = SEGMENTS * tm
    grid = (b, seg_len // tm)
    x_spec = pl.BlockSpec((1, SEGMENTS, tm, d), lambda i, t: (i, 0, t, 0))
    lane_spec = _resident(are.shape)
    g2 = norm_g.reshape(1, d)

    tm_e = min(seg_len, S5_ENDS_ROWS // SEGMENTS)
    assert seg_len % tm_e == 0 and tm_e % S5_ENDS_STEPS == 0
    ends = pl.pallas_call(
        _s5_ends_body,
        grid=(b, seg_len // tm_e),
        in_specs=[pl.BlockSpec((1, SEGMENTS, tm_e, d), lambda i, t: (i, 0, t, 0)), _resident((1, d)),
                  _resident(bblk_ends.shape), lane_spec, lane_spec],
        out_specs=pl.BlockSpec((1, SEGMENTS, n_state), lambda i, t: (i, 0, 0)),
        out_shape=jax.ShapeDtypeStruct((b, SEGMENTS, n_state), F32),
        scratch_shapes=[
            pltpu.VMEM((d // LANES, SEGMENTS * _s5_pitch(tm_e), LANES), F32),
            pltpu.VMEM((SEGMENTS * tm_e // S5_ENDS_STEPS, S5_ENDS_STEPS * d), F32),
            pltpu.VMEM((SEGMENTS * tm_e // S5_ENDS_STEPS, two_half), F32),
            pltpu.VMEM((SEGMENTS * tm_e // S5_ENDS_STEPS, two_half), F32),
            pltpu.VMEM((SEGMENTS, n_state), F32),
        ],
        compiler_params=_params("parallel", "arbitrary"),
        name="s5_ends",
    )(xv, g2, bblk_ends, ajr, aji)

    out, s_last = pl.pallas_call(
        _s5_main_body,
        grid=grid,
        in_specs=[
            x_spec, _resident((1, d)), _resident(bblk.shape), _resident(cblk.shape),
            lane_spec, lane_spec, lane_spec, lane_spec, _resident((1, d)),
            _resident(glu_w.shape),
            pl.BlockSpec((1, SEGMENTS, n_state), lambda i, t: (i, 0, 0)),
            pl.BlockSpec((1, 1, n_state), lambda i, t: (i, 0, 0)),
        ],
        out_specs=[x_spec, pl.BlockSpec((1, 1, n_state), lambda i, t: (i, 0, 0))],
        out_shape=[jax.ShapeDtypeStruct(xv.shape, F32),
                   jax.ShapeDtypeStruct((b, 1, n_state), F32)],
        scratch_shapes=[
            pltpu.VMEM((d // LANES, SEGMENTS * _s5_pitch(tm), LANES), F32), pltpu.VMEM((rows, d), F32),
            pltpu.VMEM((rows, two_half), F32), pltpu.VMEM((rows, two_half), F32),
            pltpu.VMEM((rows, two_half), F32), pltpu.VMEM((rows, two_half), F32),
            pltpu.VMEM((rows, d), F32), pltpu.VMEM((SEGMENTS, n_state), F32),
        ],
        compiler_params=_params("parallel", "arbitrary"),
        name="s5_main",
    )(xv, g2, bblk, cblk, are, aim, apr, api, d_skip.reshape(1, d).astype(F32),
      glu_w.astype(BF16), ends, _s5_state_to_lanes(s_re, s_im))
    n_re, n_im = _s5_state_from_lanes(s_last, g, p)
    return out.reshape(b, s, d), n_re, n_im


def _odd_mixer_seq(x, norm_g, s_re, s_im, prep, d_skip, glu_w):
    nseq, tt, d = x.shape
    g, p = s_re.shape[1:]
    a_bar, bblk, _, cblk = prep
    are, aim = _s5_lanes(a_bar.real), _s5_lanes(a_bar.imag)
    nb, kb, two_half = bblk.shape
    n_state = nb * two_half
    rows = nseq * tt
    assert nseq == SUBLANES and tt % SUBLANES == 0
    s0 = _s5_state_to_lanes(s_re, s_im).reshape(nseq, n_state)
    out, s_last = pl.pallas_call(
        _s5_seq_body,
        out_shape=[jax.ShapeDtypeStruct(x.shape, F32), jax.ShapeDtypeStruct((nseq, n_state), F32)],
        scratch_shapes=[
            pltpu.VMEM((d // LANES, nseq * _s5_pitch(tt), LANES), F32), pltpu.VMEM((rows, d), F32),
            pltpu.VMEM((rows, two_half), F32), pltpu.VMEM((rows, two_half), F32),
            pltpu.VMEM((rows, two_half), F32), pltpu.VMEM((rows, two_half), F32),
            pltpu.VMEM((rows, d), F32), pltpu.VMEM((SUBLANES, n_state), F32),
        ],
        compiler_params=pltpu.CompilerParams(vmem_limit_bytes=VMEM_LIMIT_BYTES),
        name="s5_seq",
    )(x, norm_g.reshape(1, d), bblk, cblk, are, aim, d_skip.reshape(1, d).astype(F32),
      glu_w.astype(BF16), s0)
    n_re, n_im = _s5_state_from_lanes(s_last.reshape(nseq, 1, n_state), g, p)
    return out, n_re, n_im


def _to_pairs(t):
    b, rows, heads, hd = t.shape
    return t.reshape(b, rows, heads // 2, 2 * hd).transpose(0, 2, 1, 3).astype(BF16)


def _even_mixer(x, p, e, conv0, h0, cache_k, cache_v, prompt):
    b, s, d = x.shape
    tm = min(s, WINDOW)
    ya, q, k, v, k_tail, v_tail, h_last, c_tail = _inproj(
        x, p['mix_norm_l'], p['ab_w_in'][e], p['q_norm'][e], p['k_norm'][e], conv0, h0,
        p['conv_w'][e], p['conv_b'][e], p['lru_wa'][e], p['lru_ba'][e], p['lru_wx'][e],
        p['lru_bx'][e], p['lru_lambda'][e], tm)
    w = ya.shape[2]
    heads = w // HEAD_DIM
    h_new = h_last[:, 0, :]
    conv_new = c_tail[:, SUBLANES - (CONV_WIDTH - 1):, :]
    if prompt:
        nq = ATTN_BLOCK
        n_prev = WINDOW // nq
        nblk = ATTN_BLOCKS_PER_STEP
        bias = _band_bias(p['rel_bias'][e], nq, [WINDOW - nq * v for v in range(n_prev + 1)])
        maps = [functools.partial(lambda i, t, off: (i, 0, jnp.maximum(nblk * t + off, 0), 0),
                                  off=idx - n_prev) for idx in range(n_prev + nblk)]
        bias_maps = [functools.partial(lambda i, t, j: (jnp.minimum(nblk * t + j, n_prev), 0, 0, 0), j=j)
                     for j in range(nblk)]
        yb = _attn(q, [(k, nq, m) for m in maps], [(v, nq, m) for m in maps], bias, bias_maps, n_prev)
    else:
        wc = cache_k.shape[1]
        assert wc == WINDOW and s == CHUNK and PAST_LEN % CHUNK == 0
        bias = _band_bias(p['rel_bias'][e], s, [0])
        first = lambda i, t: (i, 0, 0, 0)
        yb = _attn(q, [(_to_pairs(cache_k), wc, first), (k, s, first)],
                   [(_to_pairs(cache_v), wc, first), (v, s, first)], bias,
                   [lambda i, t: (0, 0, 0, 0)], 1)
    y = _outproj(x, ya, yb, p['ab_w_out'][e], tm)
    return (y, conv_new, h_new, k_tail.reshape(b, -1, heads, HEAD_DIM),
            v_tail.reshape(b, -1, heads, HEAD_DIM))


def _mixer(x, p, l, states, prompt):
    b, s, d = x.shape
    conv_st, lru_st, cache_k, cache_v, ssm_re_st, ssm_im_st = states
    if l % 2 == 0:
        e = l // 2
        w = p['conv_w'].shape[-1]
        if prompt:
            c_prev = jnp.zeros((b, CONV_WIDTH - 1, w), F32)
            h_prev = jnp.zeros((b, w), F32)
            ck = cv = None
        else:
            c_prev, h_prev, ck, cv = conv_st[e], lru_st[e], cache_k[e], cache_v[e]
        pe = dict(p, mix_norm_l=p['mix_norm'][l])
        x, c_new, h_new, k_new, v_new = _even_mixer(x, pe, e, c_prev, h_prev, ck, cv, prompt)
        return x, dict(conv=c_new, lru=h_new, k=k_new, v=v_new)
    o = l // 2
    g, st = p['ssm_A_re'].shape[1:]
    if prompt:
        x, n_re, n_im = _odd_mixer(x, p['mix_norm'][l], jnp.zeros((b, g, st), F32),
                                   jnp.zeros((b, g, st), F32), p['s5_prep'][o], p['ssm_D'][o],
                                   p['glu_w'][o], min(s // SEGMENTS, 64))
    else:
        x, n_re, n_im = _odd_mixer_seq(x, p['mix_norm'][l], ssm_re_st[o], ssm_im_st[o], p['s5_prep'][o],
                                       p['ssm_D'][o], p['glu_w'][o])
    return x, dict(re=n_re, im=n_im)


def _trunks(x_prompt, x_sample, p, sample_states):
    xs = [x_prompt, x_sample]
    d = x_prompt.shape[2]
    depth = p['ffn1_norm'].shape[0]
    outs = [dict(conv=[], lru=[], k=[], v=[], re=[], im=[]) for _ in xs]
    states = [(None,) * 6, sample_states]

    def ffn(norm, weights, l):
        ya, yb = _ffn(xs[0].reshape(-1, d), xs[1].reshape(-1, d), norm[l], *weights, l)
        return [ya.reshape(xs[0].shape), yb.reshape(xs[1].shape)]

    for l in range(depth):
        xs = ffn(p['ffn1_norm'], p['ffn1_w'], l)
        for i in range(2):
            xs[i], new = _mixer(xs[i], p, l, states[i], prompt=(i == 0))
            for name, val in new.items():
                outs[i][name].append(val)
        xs = ffn(p['ffn2_norm'], p['ffn2_w'], l)
    return [(x,) + tuple(jnp.stack(o[name]) for name in ('conv', 'lru', 'k', 'v', 're', 'im'))
            for x, o in zip(xs, outs)]


def kernel(x_prompt, x_sample, state_rglru_conv, state_rglru_h, cache_band_k, cache_band_v,
           state_ssm_re, state_ssm_im, ffn1_norm, ffn1_w_in, ffn1_w_out, mix_norm, ffn2_norm,
           ffn2_w_in, ffn2_w_out, ab_w_in, conv_w, conv_b, lru_wa, lru_ba, lru_wx, lru_bx,
           lru_lambda, q_norm, k_norm, rel_bias, ab_w_out, ssm_A_re, ssm_A_im, ssm_B_re, ssm_B_im,
           ssm_C_re, ssm_C_im, ssm_D, ssm_log_dt, glu_w):
    p = dict(ffn1_norm=ffn1_norm, mix_norm=mix_norm, ffn2_norm=ffn2_norm, ab_w_in=ab_w_in,
             conv_w=conv_w, conv_b=conv_b, lru_wa=lru_wa, lru_ba=lru_ba, lru_wx=lru_wx, lru_bx=lru_bx,
             lru_lambda=lru_lambda, q_norm=q_norm, k_norm=k_norm, rel_bias=rel_bias, ab_w_out=ab_w_out,
             ssm_A_re=ssm_A_re, ssm_A_im=ssm_A_im, ssm_B_re=ssm_B_re, ssm_B_im=ssm_B_im,
             ssm_C_re=ssm_C_re, ssm_C_im=ssm_C_im, ssm_D=ssm_D, ssm_log_dt=ssm_log_dt, glu_w=glu_w)
    p['ffn1_w'] = (ffn1_w_in.astype(BF16), ffn1_w_out.astype(BF16))
    p['ffn2_w'] = (ffn2_w_in.astype(BF16), ffn2_w_out.astype(BF16))
    p['s5_prep'] = [_prep_s5(ssm_A_re[o], ssm_A_im[o], ssm_B_re[o], ssm_B_im[o], ssm_C_re[o], ssm_C_im[o],
                             ssm_log_dt[o]) for o in range(ssm_A_re.shape[0])]
    (y_prompt, p_conv, p_h, p_k, p_v, p_re, p_im), (y_sample, s_conv, s_h, s_k, s_v, s_re, s_im) = _trunks(
        x_prompt, x_sample, p, (state_rglru_conv, state_rglru_h, cache_band_k, cache_band_v,
                                state_ssm_re, state_ssm_im))
    return (y_prompt, y_sample, p_conv, p_h, p_k, p_v, p_re, p_im, s_conv, s_h, s_k, s_v, s_re, s_im)
```

```python
import functools
import math

import jax
import jax.numpy as jnp
from jax import lax
from jax.experimental import pallas as pl
from jax.experimental.pallas import tpu as pltpu

F32 = jnp.float32
BF16 = jnp.bfloat16

LANES = 128
SUBLANES = 8
VMEM_LIMIT_BYTES = 56 * 1024 * 1024

EPS = 1e-6
CHUNK = 64
LEFT_CHUNKS = 8
WINDOW = LEFT_CHUNKS * CHUNK
MAX_REL = 128
PAST_LEN = 4096
HEAD_DIM = 64
CONV_WIDTH = 4
LRU_C = 8.0
SSM_GROUP = 16
SEGMENTS = SUBLANES
S5_GROUP_BLOCK = 16
S5_ENDS_STEPS = 4
S5_ENDS_ROWS = 1024
FFN_CHUNK = 256
ATTN_BLOCK = 256
ATTN_BLOCKS_PER_STEP = 4
FFN_ROWS = 512
NEG_INF = -1e30


def _params(*sem):
    return pltpu.CompilerParams(dimension_semantics=sem, vmem_limit_bytes=VMEM_LIMIT_BYTES)


def _resident(shape):
    nd = len(shape)
    return pl.BlockSpec(shape, lambda *_: (0,) * nd, pipeline_mode=pl.Buffered(1))


def _rms(x, g):
    return x * lax.rsqrt(jnp.mean(x * x, axis=-1, keepdims=True) + EPS) * g


def _sigmoid(x):
    return 1.0 / (1.0 + jnp.exp(-x))


def _ffn_body(xa_ref, xb_ref, g_ref, win_ref, wout_ref, oa_ref, ob_ref, acc_ref):
    d_ff = wout_ref.shape[0]
    first = pl.program_id(0) == 0
    x = jnp.where(first, xb_ref[...], xa_ref[...])
    h = _rms(x, g_ref[...]).astype(BF16)
    for c in range(d_ff // FFN_CHUNK):
        cols = slice(c * FFN_CHUNK, (c + 1) * FFN_CHUNK)
        up_cols = slice(d_ff + c * FFN_CHUNK, d_ff + (c + 1) * FFN_CHUNK)
        gate = jnp.dot(h, win_ref[:, cols], preferred_element_type=F32)
        up = jnp.dot(h, win_ref[:, up_cols], preferred_element_type=F32)
        act = (gate * _sigmoid(gate) * up).astype(BF16)
        y = jnp.dot(act, wout_ref[cols, :], preferred_element_type=F32)
        if c == 0:
            acc_ref[...] = y
        else:
            acc_ref[...] += y
    oa_ref[...] = x + 0.5 * acc_ref[...]

    @pl.when(first)
    def _():
        ob_ref[...] = xb_ref[...] + 0.5 * acc_ref[...]


def _ffn(xa, xb, g, w_in_all, w_out_all, layer):
    n, d = xa.shape
    tm = FFN_ROWS
    _, d_ff, _ = w_out_all.shape
    assert d_ff % FFN_CHUNK == 0 and n % tm == 0 and xb.shape == (tm, d)
    a_spec = pl.BlockSpec((tm, d), lambda i: (jnp.maximum(i - 1, 0), 0))
    b_spec = pl.BlockSpec((tm, d), lambda i: (0, 0))
    return pl.pallas_call(
        _ffn_body,
        grid=(n // tm + 1,),
        in_specs=[
            a_spec, b_spec, _resident((1, d)),
            pl.BlockSpec((None, d, 2 * d_ff), lambda i: (layer, 0, 0), pipeline_mode=pl.Buffered(1)),
            pl.BlockSpec((None, d_ff, d), lambda i: (layer, 0, 0), pipeline_mode=pl.Buffered(1)),
        ],
        out_specs=[a_spec, b_spec],
        out_shape=[jax.ShapeDtypeStruct((n, d), F32), jax.ShapeDtypeStruct((tm, d), F32)],
        scratch_shapes=[pltpu.VMEM((tm, d), F32)],
        compiler_params=_params("arbitrary"),
        name="ffn",
    )(xa, xb, g.reshape(1, d), w_in_all, w_out_all)


def _head_norm(t, gain):
    low = lax.broadcasted_iota(jnp.int32, (t.shape[0], LANES), 1) < HEAD_DIM
    outs = []
    for j in range(t.shape[1] // LANES):
        blk = t[:, j * LANES:(j + 1) * LANES]
        sq = blk * blk
        tot = jnp.sum(sq, axis=-1, keepdims=True)
        lo = jnp.sum(jnp.where(low, sq, 0.0), axis=-1, keepdims=True)
        ms = jnp.where(low, lo, tot - lo) * (1.0 / HEAD_DIM)
        outs.append(blk * lax.rsqrt(ms + EPS) * gain[:, j * LANES:(j + 1) * LANES])
    return jnp.concatenate(outs, axis=-1)


def _block_diag(w):
    nb, n, _ = w.shape
    eye = jnp.eye(nb, dtype=w.dtype)
    return jnp.einsum('hij,hg->higj', w, eye).reshape(nb * n, nb * n)


def _rg_lru(xa, ga, c0_ref, h0_ref, cw_ref, cb_ref, wa_ref, ba_ref, wx_ref, bx_ref, lam_ref,
            ya_ref, hlast_ref, ctail_ref, ext_ref, hc_ref, a_ref, u_ref):
    t_len, w = xa.shape

    @pl.when(pl.program_id(1) == 0)
    def _():
        ext_ref[0:SUBLANES, :] = c0_ref[0]
        hc_ref[...] = jnp.broadcast_to(h0_ref[0], hc_ref.shape)

    ext_ref[SUBLANES:SUBLANES + t_len, :] = xa
    xc = cb_ref[...] + cw_ref[3:4, :] * xa
    for k in range(CONV_WIDTH - 1):
        off = SUBLANES - (CONV_WIDTH - 1) + k
        xc = xc + cw_ref[k:k + 1, :] * ext_ref[off:off + t_len, :]
    tail = xa[t_len - SUBLANES:, :]
    ext_ref[0:SUBLANES, :] = tail
    ctail_ref[0] = tail

    xcb = xc.astype(BF16)
    r = _sigmoid(jnp.dot(xcb, wa_ref[...], preferred_element_type=F32) + ba_ref[...])
    i = _sigmoid(jnp.dot(xcb, wx_ref[...], preferred_element_type=F32) + bx_ref[...])
    neg_lam = -lam_ref[...]
    softplus = jnp.maximum(neg_lam, 0.0) + jnp.log1p(jnp.exp(-jnp.abs(neg_lam)))
    log_a = -LRU_C * r * softplus
    a = jnp.exp(log_a)
    a_ref[...] = a
    u_ref[...] = jnp.sqrt(-jnp.tanh(log_a) * (a * a + 1.0)) * (i * xc)

    row = lax.broadcasted_iota(jnp.int32, (SUBLANES, w), 0)
    h = hc_ref[...]
    for blk in range(t_len // SUBLANES):
        rows = slice(blk * SUBLANES, (blk + 1) * SUBLANES)
        a = a_ref[rows, :]
        u = u_ref[rows, :]
        for d in (1, 2, 4):
            keep = row >= d
            a_sh = jnp.where(keep, pltpu.roll(a, d, 0), 1.0)
            u_sh = jnp.where(keep, pltpu.roll(u, d, 0), 0.0)
            u = u + a * u_sh
            a = a * a_sh
        hs = a * h + u
        ya_ref[0, rows, :] = (hs * jax.nn.gelu(ga[rows, :])).astype(ya_ref.dtype)
        h = jnp.broadcast_to(hs[SUBLANES - 1:SUBLANES, :], (SUBLANES, w))
    hc_ref[...] = h
    hlast_ref[0] = h


def _inproj_body(x_ref, g_ref, w_ref, qg_ref, kg_ref, c0_ref, h0_ref, cw_ref, cb_ref, wa_ref,
                 ba_ref, wx_ref, bx_ref, lam_ref, ya_ref, q_ref, k_ref, v_ref, kt_ref, vt_ref,
                 hlast_ref, ctail_ref, ext_ref, hc_ref, a_ref, u_ref):
    w = kt_ref.shape[2]
    h = _rms(x_ref[0], g_ref[...]).astype(BF16)
    pa = jnp.dot(h, w_ref[:, :2 * w], preferred_element_type=F32)
    _rg_lru(pa[:, :w], pa[:, w:], c0_ref, h0_ref, cw_ref, cb_ref, wa_ref, ba_ref, wx_ref, bx_ref,
            lam_ref, ya_ref, hlast_ref, ctail_ref, ext_ref, hc_ref, a_ref, u_ref)
    pb = jnp.dot(h, w_ref[:, 2 * w:], preferred_element_type=F32)
    q = _head_norm(pb[:, 0 * w:1 * w], qg_ref[...]) * (1.0 / math.sqrt(HEAD_DIM))
    k = _head_norm(pb[:, 1 * w:2 * w], kg_ref[...])
    v = pb[:, 2 * w:3 * w]
    kt_ref[0] = k
    vt_ref[0] = v
    for pr in range(w // LANES):
        lanes = slice(pr * LANES, (pr + 1) * LANES)
        q_ref[0, pr] = q[:, lanes].astype(BF16)
        k_ref[0, pr] = k[:, lanes].astype(BF16)
        v_ref[0, pr] = v[:, lanes].astype(BF16)


def _inproj(x, g, w_in, q_gain, k_gain, conv0, h0, conv_w, conv_b, wa, ba, wx, bx, lam, tm):
    b, s, d = x.shape
    w = w_in.shape[1] // 5
    heads = w // HEAD_DIM
    npair = w // LANES
    assert s % tm == 0 and tm % SUBLANES == 0
    pad = jnp.zeros((b, SUBLANES - (CONV_WIDTH - 1), w), F32)
    c0 = jnp.concatenate([pad, conv0.astype(F32)], axis=1)
    row = pl.BlockSpec((1, tm, w), lambda i, t: (i, t, 0))
    pair = pl.BlockSpec((1, npair, tm, LANES), lambda i, t: (i, 0, t, 0))
    tail = pl.BlockSpec((1, tm, w), lambda i, t: (i, 0, 0))
    state = pl.BlockSpec((1, SUBLANES, w), lambda i, t: (i, 0, 0))
    vec = _resident((1, w))
    return pl.pallas_call(
        _inproj_body,
        grid=(b, s // tm),
        in_specs=[
            pl.BlockSpec((1, tm, d), lambda i, t: (i, t, 0)),
            _resident((1, d)), _resident(w_in.shape), vec, vec,
            state, pl.BlockSpec((1, 1, w), lambda i, t: (i, 0, 0)),
            _resident((CONV_WIDTH, w)), vec, _resident((w, w)), vec, _resident((w, w)), vec, vec,
        ],
        out_specs=[row, pair, pair, pair, tail, tail, state, state],
        out_shape=[jax.ShapeDtypeStruct((b, s, w), BF16)]
        + [jax.ShapeDtypeStruct((b, npair, s, LANES), BF16)] * 3
        + [jax.ShapeDtypeStruct((b, tm, w), F32)] * 2
        + [jax.ShapeDtypeStruct((b, SUBLANES, w), F32)] * 2,
        scratch_shapes=[
            pltpu.VMEM((SUBLANES + tm, w), F32),
            pltpu.VMEM((SUBLANES, w), F32),
            pltpu.VMEM((tm, w), F32),
            pltpu.VMEM((tm, w), F32),
        ],
        compiler_params=_params("parallel", "arbitrary"),
        name="inproj",
    )(x, g.reshape(1, d), w_in.astype(BF16),
      jnp.tile(q_gain, heads).reshape(1, w), jnp.tile(k_gain, heads).reshape(1, w),
      c0, h0.astype(F32).reshape(b, 1, w), conv_w, conv_b.reshape(1, w),
      _block_diag(wa).astype(BF16), ba.reshape(1, w), _block_diag(wx).astype(BF16),
      bx.reshape(1, w), lam.reshape(1, w))


def _attn_body(*refs, n_prev, n_blocks):
    n_parts = n_prev + n_blocks
    q_ref = refs[0]
    k_parts = refs[1:1 + n_parts]
    v_parts = refs[1 + n_parts:1 + 2 * n_parts]
    bias_ref, o_ref = refs[1 + 2 * n_parts:]
    last_variant = bias_ref.shape[0] - 1
    npair = q_ref.shape[1]
    nq = q_ref.shape[2] // n_blocks
    low = lax.broadcasted_iota(jnp.int32, (nq, LANES), 1) < HEAD_DIM
    for pr in range(npair):
        for j in range(n_blocks):
            rows = slice(j * nq, (j + 1) * nq)
            qp = q_ref[0, pr, rows, :]
            zero = jnp.zeros_like(qp)
            qs = jnp.concatenate([jnp.where(low, qp, zero), jnp.where(low, zero, qp)], axis=0)
            kw = jnp.concatenate([part[0, pr] for part in k_parts[j:j + n_prev + 1]], axis=0)
            vw = jnp.concatenate([part[0, pr] for part in v_parts[j:j + n_prev + 1]], axis=0)
            s = lax.dot_general(qs, kw, (((1,), (1,)), ((), ())), preferred_element_type=F32)
            variant = jnp.minimum(n_blocks * pl.program_id(1) + j, last_variant)
            s = s + bias_ref[variant, pr]
            p = jnp.exp(s - jnp.max(s, axis=-1, keepdims=True))
            denom = jnp.sum(p, axis=-1, keepdims=True)
            o = jnp.dot(p.astype(BF16), vw, preferred_element_type=F32) / denom
            o_ref[0, pr, rows, :] = jnp.where(low, o[:nq], o[nq:]).astype(o_ref.dtype)


def _band_bias(rel_bias, nq, invalid_cols):
    heads = rel_bias.shape[1]
    win = WINDOW + nq
    period = win + nq
    n_hi = WINDOW - MAX_REL
    n_lo = max(win - (WINDOW + MAX_REL + 1), 0)
    tab = rel_bias.astype(F32)
    mid = jnp.flip(tab, axis=0)[:win - n_hi - n_lo]
    vec = jnp.concatenate([jnp.broadcast_to(tab[-1:], (n_hi, heads)), mid,
                           jnp.broadcast_to(tab[:1], (n_lo, heads)),
                           jnp.broadcast_to(tab[-1:], (nq, heads))], axis=0)
    toep = jnp.tile(vec.T, (1, nq))[:, :nq * (period - 1)].reshape(heads, nq, period - 1)[:, :, :win]
    qc = jnp.arange(nq)[:, None] // CHUNK
    kc = jnp.arange(win)[None, :] // CHUNK
    band = (kc >= qc) & (kc <= qc + LEFT_CHUNKS)
    col = jnp.arange(win)[None, :]
    out = [jnp.where(band & (col >= c), toep, NEG_INF) for c in invalid_cols]
    return jnp.stack(out).reshape(len(invalid_cols), heads // 2, 2 * nq, win)


def _attn(q, k_parts, v_parts, bias, n_blocks, n_prev):
    b, npair, s, _ = q.shape
    assert len(k_parts) == len(v_parts) == n_prev + n_blocks
    rows = sum(r for _, r, _ in k_parts[n_prev:])

    def spec(nrows, index_map):
        return pl.BlockSpec((1, npair, nrows, LANES), index_map)

    own = spec(rows, lambda i, t: (i, 0, t, 0))
    return pl.pallas_call(
        functools.partial(_attn_body, n_prev=n_prev, n_blocks=n_blocks),
        grid=(b, s // rows),
        in_specs=([own] + [spec(r, m) for _, r, m in k_parts] + [spec(r, m) for _, r, m in v_parts]
                  + [_resident(bias.shape)]),
        out_specs=own,
        out_shape=jax.ShapeDtypeStruct((b, npair, s, LANES), BF16),
        compiler_params=_params("parallel", "arbitrary"),
        name="attn",
    )(q, *[a for a, _, _ in k_parts], *[a for a, _, _ in v_parts], bias)


def _outproj_body(x_ref, ya_ref, yb_ref, w_ref, o_ref):
    w = ya_ref.shape[2]
    yb = jnp.concatenate([yb_ref[0, pr] for pr in range(yb_ref.shape[1])], axis=-1)
    y = jnp.dot(ya_ref[0], w_ref[:w, :], preferred_element_type=F32)
    y = y + jnp.dot(yb, w_ref[w:, :], preferred_element_type=F32)
    o_ref[0] = x_ref[0] + y


def _outproj(x, ya, yb, w_out, tm):
    b, s, d = x.shape
    w = ya.shape[2]
    npair = yb.shape[1]
    return pl.pallas_call(
        _outproj_body,
        grid=(b, s // tm),
        in_specs=[
            pl.BlockSpec((1, tm, d), lambda i, t: (i, t, 0)),
            pl.BlockSpec((1, tm, w), lambda i, t: (i, t, 0)),
            pl.BlockSpec((1, npair, tm, LANES), lambda i, t: (i, 0, t, 0)),
            _resident(w_out.shape),
        ],
        out_specs=pl.BlockSpec((1, tm, d), lambda i, t: (i, t, 0)),
        out_shape=jax.ShapeDtypeStruct((b, s, d), F32),
        compiler_params=_params("parallel", "parallel"),
        name="outproj",
    )(x, ya, yb, w_out.astype(BF16))


def _s5_load_h(x_ref, g_ref, hnat, hperm):
    tm = x_ref.shape[2]
    pitch = hnat.shape[1] // SEGMENTS
    n_lane_blocks = hnat.shape[0]
    for r in range(SEGMENTS):
        h = _rms(x_ref[0, r], g_ref[...])
        for c in range(n_lane_blocks):
            hnat[c, r * pitch:r * pitch + tm, :] = h[:, c * LANES:(c + 1) * LANES]
    for m in range(tm):
        for c in range(n_lane_blocks):
            hperm[m * SEGMENTS:(m + 1) * SEGMENTS, c * LANES:(c + 1) * LANES] = (
                hnat[c, pl.ds(m, SEGMENTS, stride=pitch), :])


def _s5_pitch(tm):
    assert tm % SUBLANES == 0
    return tm + SUBLANES // 2


def _s5_scan(bu, sbuf, st, are_ref, aim_ref, jb, tm):
    half = bu.shape[1] // 2
    base = jb * bu.shape[1]
    step = 4 * LANES
    for c0 in range(0, half, step):
        ar = are_ref[jb, :, c0:c0 + step]
        ai = aim_ref[jb, :, c0:c0 + step]
        sr = st[:, base + c0:base + c0 + step]
        si = st[:, base + half + c0:base + half + c0 + step]
        for m in range(tm):
            rows = slice(m * SEGMENTS, (m + 1) * SEGMENTS)
            nr = ar * sr - ai * si + bu[rows, c0:c0 + step]
            ni = ar * si + ai * sr + bu[rows, half + c0:half + c0 + step]
            if sbuf is not None:
                sbuf[rows, c0:c0 + step] = nr
                sbuf[rows, half + c0:half + c0 + step] = ni
            sr, si = nr, ni
        st[:, base + c0:base + c0 + step] = sr
        st[:, base + half + c0:base + half + c0 + step] = si


def _s5_ends_body(x_ref, g_ref, bblk_ref, ajr_ref, aji_ref, ends_ref, hnat, hperm, bu0, bu1, st):
    tm, d = x_ref.shape[2], x_ref.shape[3]
    nb, kbj, _ = bblk_ref.shape
    kb = d // nb
    steps = kbj // kb
    pitch = hnat.shape[1] // SEGMENTS
    bus = (bu0, bu1)

    @pl.when(pl.program_id(1) == 0)
    def _():
        st[...] = jnp.zeros_like(st)

    for r in range(SEGMENTS):
        h = _rms(x_ref[0, r], g_ref[...])
        for c in range(hnat.shape[0]):
            hnat[c, r * pitch:r * pitch + tm, :] = h[:, c * LANES:(c + 1) * LANES]
    for mb in range(tm // steps):
        for j in range(steps):
            for c in range(hnat.shape[0]):
                jb, within = divmod(c * LANES, kb)
                dst = jb * kbj + j * kb + within
                hperm[mb * SEGMENTS:(mb + 1) * SEGMENTS, dst:dst + LANES] = (
                    hnat[c, pl.ds(mb * steps + j, SEGMENTS, stride=pitch), :])
    hb = hperm[...].astype(BF16)

    def b_u(jb):
        return jnp.dot(hb[:, jb * kbj:(jb + 1) * kbj], bblk_ref[jb], preferred_element_type=F32)

    bus[0][...] = b_u(0)
    for jb in range(nb):
        if jb + 1 < nb:
            bus[(jb + 1) % 2][...] = b_u(jb + 1)
        _s5_scan(bus[jb % 2], None, st, ajr_ref, aji_ref, jb, tm // steps)

    @pl.when(pl.program_id(1) == pl.num_programs(1) - 1)
    def _():
        ends_ref[0] = st[...]


def _s5_main_body(x_ref, g_ref, bblk_ref, cblk_ref, are_ref, aim_ref, apr_ref, api_ref, d_ref,
                  glu_ref, ends_ref, s0_ref, o_ref, slast_ref, hnat, hperm, bu0, bu1, sb0, sb1,
                  ybuf, st):
    tm = x_ref.shape[2]
    bus = (bu0, bu1)
    sbs = (sb0, sb1)
    d = x_ref.shape[3]
    nb, kb, two_half = bblk_ref.shape
    half = two_half // 2

    @pl.when(pl.program_id(1) == 0)
    def _():
        for jb in range(nb):
            re_cols = slice(jb * two_half, jb * two_half + half)
            im_cols = slice(jb * two_half + half, (jb + 1) * two_half)
            pr = apr_ref[jb, 0:1, :]
            pi = api_ref[jb, 0:1, :]
            er = s0_ref[0, :, re_cols]
            ei = s0_ref[0, :, im_cols]
            for r in range(SEGMENTS):
                st[r:r + 1, re_cols] = er
                st[r:r + 1, im_cols] = ei
                nr = pr * er - pi * ei + ends_ref[0, r:r + 1, re_cols]
                ni = pr * ei + pi * er + ends_ref[0, r:r + 1, im_cols]
                er, ei = nr, ni
            slast_ref[0, :, re_cols] = er
            slast_ref[0, :, im_cols] = ei

    _s5_load_h(x_ref, g_ref, hnat, hperm)
    h = hperm[...]
    hb = h.astype(BF16)

    def b_u(jb):
        return jnp.dot(hb[:, jb * kb:(jb + 1) * kb], bblk_ref[jb], preferred_element_type=F32)

    bus[0][...] = b_u(0)
    for jb in range(nb):
        if jb + 1 < nb:
            bus[(jb + 1) % 2][...] = b_u(jb + 1)
        _s5_scan(bus[jb % 2], sbs[jb % 2], st, are_ref, aim_ref, jb, tm)
        ybuf[:, jb * kb:(jb + 1) * kb] = jnp.dot(sbs[jb % 2][...].astype(BF16), cblk_ref[jb],
                                                 preferred_element_type=F32)
    y = ybuf[...] + d_ref[...] * h
    z = jnp.dot(y.astype(BF16), glu_ref[...], preferred_element_type=F32)
    o = z[:, :d] * _sigmoid(z[:, d:])
    for c in range(hnat.shape[0]):
        hnat[c, 0:SEGMENTS * tm, :] = o[:, c * LANES:(c + 1) * LANES]
    for r in range(SEGMENTS):
        for c in range(hnat.shape[0]):
            lanes = slice(c * LANES, (c + 1) * LANES)
            o_ref[0, r, :, lanes] = x_ref[0, r, :, lanes] + hnat[c, pl.ds(r, tm, stride=SEGMENTS), :]


def _s5_expand_body(bc_ref, cc_ref, bblk_ref, bends_ref, cblk_ref):
    n_b, _, kb, p = bc_ref.shape
    i_dim = cc_ref.shape[3]
    half = bblk_ref.shape[2] // 2
    log_p, log_i = p.bit_length() - 1, i_dim.bit_length() - 1
    assert p == 1 << log_p and i_dim == 1 << log_i

    def iota(shape, axis):
        return lax.broadcasted_iota(jnp.int32, shape, axis)

    sel_b = jnp.where((iota((p, half), 1) & (p - 1)) == iota((p, half), 0), 1.0, 0.0).astype(BF16)
    mask_b = (iota((kb, half), 0) >> log_i) == (iota((kb, half), 1) >> log_p)
    sel_c = jnp.where((iota((i_dim, kb), 1) & (i_dim - 1)) == iota((i_dim, kb), 0), 1.0, 0.0).astype(BF16)
    mask_c = (iota((half, kb), 0) >> log_p) == (iota((half, kb), 1) >> log_i)

    def expand_b(m):
        t = jnp.dot(bc_ref[m, 0].astype(BF16), sel_b, preferred_element_type=F32)
        return jnp.where(mask_b, t, 0.0).astype(BF16)

    bblk_ref[0, :, :half] = expand_b(0)
    bblk_ref[0, :, half:] = expand_b(1)
    for j in range((n_b - 2) // 2):
        bends_ref[0, j * kb:(j + 1) * kb, :half] = expand_b(2 + 2 * j)
        bends_ref[0, j * kb:(j + 1) * kb, half:] = expand_b(3 + 2 * j)
    for m in range(2):
        t = jnp.dot(cc_ref[m, 0].astype(BF16), sel_c, preferred_element_type=F32)
        cblk_ref[0, m * half:(m + 1) * half, :] = jnp.where(mask_c, t, 0.0).astype(BF16)


def _prep_s5(a_re, a_im, b_re, b_im, c_re, c_im, log_dt):
    g, p = a_re.shape
    gb = S5_GROUP_BLOCK
    nb = g // gb
    a = lax.complex(a_re.astype(F32), a_im.astype(F32))
    dt = jnp.exp(log_dt.astype(F32))[:, None]
    a_bar = jnp.exp(a * dt)
    b_bar = ((a_bar - 1.0) / a)[..., None] * lax.complex(b_re.astype(F32), b_im.astype(F32))
    mats = [b_bar]
    for _ in range(S5_ENDS_STEPS - 1):
        mats.append(mats[-1] * a_bar[..., None])
    mats = [b_bar] + mats[::-1]
    bc = jnp.stack([part for m in mats for part in (m.real, m.imag)])
    bc = bc.transpose(0, 1, 3, 2).reshape(len(mats) * 2, nb, gb * SSM_GROUP, p)
    cc = jnp.stack([c_re.astype(F32), -c_im.astype(F32)]).transpose(0, 1, 3, 2)
    cc = cc.reshape(2, nb, gb * p, SSM_GROUP)
    kb, half = gb * SSM_GROUP, gb * p
    bblk, bblk_ends, cblk = pl.pallas_call(
        _s5_expand_body,
        grid=(nb,),
        in_specs=[pl.BlockSpec((bc.shape[0], 1, kb, p), lambda j: (0, j, 0, 0)),
                  pl.BlockSpec((2, 1, half, SSM_GROUP), lambda j: (0, j, 0, 0))],
        out_specs=[pl.BlockSpec((1, kb, 2 * half), lambda j: (j, 0, 0)),
                   pl.BlockSpec((1, S5_ENDS_STEPS * kb, 2 * half), lambda j: (j, 0, 0)),
                   pl.BlockSpec((1, 2 * half, kb), lambda j: (j, 0, 0))],
        out_shape=[jax.ShapeDtypeStruct((nb, kb, 2 * half), BF16),
                   jax.ShapeDtypeStruct((nb, S5_ENDS_STEPS * kb, 2 * half), BF16),
                   jax.ShapeDtypeStruct((nb, 2 * half, kb), BF16)],
        compiler_params=_params("parallel"),
        name="s5_expand",
    )(bc, cc)
    return a_bar, bblk, bblk_ends, cblk


def _s5_lanes(v):
    g, p = v.shape
    nb = g // S5_GROUP_BLOCK
    return jnp.broadcast_to(v.reshape(nb, 1, S5_GROUP_BLOCK * p), (nb, SUBLANES, S5_GROUP_BLOCK * p))


def _s5_seq_body(x_ref, g_ref, bblk_ref, cblk_ref, are_ref, aim_ref, d_ref, glu_ref, s0_ref,
                 o_ref, slast_ref, hnat, hperm, bu0, bu1, sb0, sb1, ybuf, st):
    nseq, tt, d = x_ref.shape
    nb, kb, _ = bblk_ref.shape
    pitch = hnat.shape[1] // nseq
    bus = (bu0, bu1)
    sbs = (sb0, sb1)
    st[...] = s0_ref[...]
    for b in range(nseq):
        hb_nat = _rms(x_ref[b], g_ref[...])
        for c in range(hnat.shape[0]):
            hnat[c, b * pitch:b * pitch + tt, :] = hb_nat[:, c * LANES:(c + 1) * LANES]
    for t in range(tt):
        for c in range(hnat.shape[0]):
            hperm[t * nseq:(t + 1) * nseq, c * LANES:(c + 1) * LANES] = (
                hnat[c, pl.ds(t, nseq, stride=pitch), :])
    h = hperm[...]
    hb = h.astype(BF16)

    def b_u(jb):
        return jnp.dot(hb[:, jb * kb:(jb + 1) * kb], bblk_ref[jb], preferred_element_type=F32)

    bus[0][...] = b_u(0)
    for jb in range(nb):
        if jb + 1 < nb:
            bus[(jb + 1) % 2][...] = b_u(jb + 1)
        _s5_scan(bus[jb % 2], sbs[jb % 2], st, are_ref, aim_ref, jb, tt)
        ybuf[:, jb * kb:(jb + 1) * kb] = jnp.dot(sbs[jb % 2][...].astype(BF16), cblk_ref[jb],
                                                 preferred_element_type=F32)
    y = ybuf[...] + d_ref[...] * h
    z = jnp.dot(y.astype(BF16), glu_ref[...], preferred_element_type=F32)
    o = z[:, :d] * _sigmoid(z[:, d:])
    for c in range(hnat.shape[0]):
        hnat[c, 0:nseq * tt, :] = o[:, c * LANES:(c + 1) * LANES]
    for b in range(nseq):
        for c in range(hnat.shape[0]):
            lanes = slice(c * LANES, (c + 1) * LANES)
            o_ref[b, :, lanes] = x_ref[b, :, lanes] + hnat[c, pl.ds(b, tt, stride=nseq), :]
    slast_ref[...] = st[...]


def _s5_state_to_lanes(s_re, s_im):
    b, g, p = s_re.shape
    nb = g // S5_GROUP_BLOCK
    both = jnp.stack([s_re.reshape(b, nb, S5_GROUP_BLOCK * p), s_im.reshape(b, nb, S5_GROUP_BLOCK * p)],
                     axis=2)
    return both.reshape(b, 1, 2 * g * p).astype(F32)


def _s5_state_from_lanes(s, g, p):
    b = s.shape[0]
    both = s.reshape(b, g // S5_GROUP_BLOCK, 2, S5_GROUP_BLOCK, p)
    return both[:, :, 0].reshape(b, g, p), both[:, :, 1].reshape(b, g, p)


def _odd_mixer(x, norm_g, s_re, s_im, prep, d_skip, glu_w, tm):
    b, s, d = x.shape
    g, p = s_re.shape[1:]
    seg_len = s // SEGMENTS
    a_bar, bblk, bblk_ends, cblk = prep
    a_pow = a_bar
    for _ in range(int(math.log2(seg_len))):
        a_pow = a_pow * a_pow
    a_ends = a_bar
    for _ in range(S5_ENDS_STEPS - 1):
        a_ends = a_ends * a_bar
    are, aim, apr, api, ajr, aji = (_s5_lanes(v) for v in (a_bar.real, a_bar.imag, a_pow.real,
                                                            a_pow.imag, a_ends.real, a_ends.imag))
    nb, kb, two_half = bblk.shape
    n_state = nb * two_half
    xv = x.reshape(b, SEGMENTS, seg_len, d)
    rows = SEGMENTS * tm
    grid = (b, seg_len // tm)
    x_spec = pl.BlockSpec((1, SEGMENTS, tm, d), lambda i, t: (i, 0, t, 0))
    lane_spec = _resident(are.shape)
    g2 = norm_g.reshape(1, d)

    tm_e = min(seg_len, S5_ENDS_ROWS // SEGMENTS)
    assert seg_len % tm_e == 0 and tm_e % S5_ENDS_STEPS == 0
    ends = pl.pallas_call(
        _s5_ends_body,
        grid=(b, seg_len // tm_e),
        in_specs=[pl.BlockSpec((1, SEGMENTS, tm_e, d), lambda i, t: (i, 0, t, 0)), _resident((1, d)),
                  _resident(bblk_ends.shape), lane_spec, lane_spec],
        out_specs=pl.BlockSpec((1, SEGMENTS, n_state), lambda i, t: (i, 0, 0)),
        out_shape=jax.ShapeDtypeStruct((b, SEGMENTS, n_state), F32),
        scratch_shapes=[
            pltpu.VMEM((d // LANES, SEGMENTS * _s5_pitch(tm_e), LANES), F32),
            pltpu.VMEM((SEGMENTS * tm_e // S5_ENDS_STEPS, S5_ENDS_STEPS * d), F32),
            pltpu.VMEM((SEGMENTS * tm_e // S5_ENDS_STEPS, two_half), F32),
            pltpu.VMEM((SEGMENTS * tm_e // S5_ENDS_STEPS, two_half), F32),
            pltpu.VMEM((SEGMENTS, n_state), F32),
        ],
        compiler_params=_params("parallel", "arbitrary"),
        name="s5_ends",
    )(xv, g2, bblk_ends, ajr, aji)

    out, s_last = pl.pallas_call(
        _s5_main_body,
        grid=grid,
        in_specs=[
            x_spec, _resident((1, d)), _resident(bblk.shape), _resident(cblk.shape),
            lane_spec, lane_spec, lane_spec, lane_spec, _resident((1, d)),
            _resident(glu_w.shape),
            pl.BlockSpec((1, SEGMENTS, n_state), lambda i, t: (i, 0, 0)),
            pl.BlockSpec((1, 1, n_state), lambda i, t: (i, 0, 0)),
        ],
        out_specs=[x_spec, pl.BlockSpec((1, 1, n_state), lambda i, t: (i, 0, 0))],
        out_shape=[jax.ShapeDtypeStruct(xv.shape, F32),
                   jax.ShapeDtypeStruct((b, 1, n_state), F32)],
        scratch_shapes=[
            pltpu.VMEM((d // LANES, SEGMENTS * _s5_pitch(tm), LANES), F32), pltpu.VMEM((rows, d), F32),
            pltpu.VMEM((rows, two_half), F32), pltpu.VMEM((rows, two_half), F32),
            pltpu.VMEM((rows, two_half), F32), pltpu.VMEM((rows, two_half), F32),
            pltpu.VMEM((rows, d), F32), pltpu.VMEM((SEGMENTS, n_state), F32),
        ],
        compiler_params=_params("parallel", "arbitrary"),
        name="s5_main",
    )(xv, g2, bblk, cblk, are, aim, apr, api, d_skip.reshape(1, d).astype(F32),
      glu_w.astype(BF16), ends, _s5_state_to_lanes(s_re, s_im))
    n_re, n_im = _s5_state_from_lanes(s_last, g, p)
    return out.reshape(b, s, d), n_re, n_im


def _odd_mixer_seq(x, norm_g, s_re, s_im, prep, d_skip, glu_w):
    nseq, tt, d = x.shape
    g, p = s_re.shape[1:]
    a_bar, bblk, _, cblk = prep
    are, aim = _s5_lanes(a_bar.real), _s5_lanes(a_bar.imag)
    nb, kb, two_half = bblk.shape
    n_state = nb * two_half
    rows = nseq * tt
    assert nseq == SUBLANES and tt % SUBLANES == 0
    s0 = _s5_state_to_lanes(s_re, s_im).reshape(nseq, n_state)
    out, s_last = pl.pallas_call(
        _s5_seq_body,
        out_shape=[jax.ShapeDtypeStruct(x.shape, F32), jax.ShapeDtypeStruct((nseq, n_state), F32)],
        scratch_shapes=[
            pltpu.VMEM((d // LANES, nseq * _s5_pitch(tt), LANES), F32), pltpu.VMEM((rows, d), F32),
            pltpu.VMEM((rows, two_half), F32), pltpu.VMEM((rows, two_half), F32),
            pltpu.VMEM((rows, two_half), F32), pltpu.VMEM((rows, two_half), F32),
            pltpu.VMEM((rows, d), F32), pltpu.VMEM((SUBLANES, n_state), F32),
        ],
        compiler_params=pltpu.CompilerParams(vmem_limit_bytes=VMEM_LIMIT_BYTES),
        name="s5_seq",
    )(x, norm_g.reshape(1, d), bblk, cblk, are, aim, d_skip.reshape(1, d).astype(F32),
      glu_w.astype(BF16), s0)
    n_re, n_im = _s5_state_from_lanes(s_last.reshape(nseq, 1, n_state), g, p)
    return out, n_re, n_im


def _to_pairs(t):
    b, rows, heads, hd = t.shape
    return t.reshape(b, rows, heads // 2, 2 * hd).transpose(0, 2, 1, 3).astype(BF16)


def _even_mixer(x, p, e, conv0, h0, cache_k, cache_v, prompt):
    b, s, d = x.shape
    tm = min(s, WINDOW)
    ya, q, k, v, k_tail, v_tail, h_last, c_tail = _inproj(
        x, p['mix_norm_l'], p['ab_w_in'][e], p['q_norm'][e], p['k_norm'][e], conv0, h0,
        p['conv_w'][e], p['conv_b'][e], p['lru_wa'][e], p['lru_ba'][e], p['lru_wx'][e],
        p['lru_bx'][e], p['lru_lambda'][e], tm)
    w = ya.shape[2]
    heads = w // HEAD_DIM
    h_new = h_last[:, 0, :]
    conv_new = c_tail[:, SUBLANES - (CONV_WIDTH - 1):, :]
    if prompt:
        nq = ATTN_BLOCK
        n_prev = WINDOW // nq
        nblk = ATTN_BLOCKS_PER_STEP
        bias = _band_bias(p['rel_bias'][e], nq, [WINDOW - nq * v for v in range(n_prev + 1)])
        maps = [functools.partial(lambda i, t, off: (i, 0, jnp.maximum(nblk * t + off, 0), 0),
                                  off=idx - n_prev) for idx in range(n_prev + nblk)]
        yb = _attn(q, [(k, nq, m) for m in maps], [(v, nq, m) for m in maps], bias, nblk, n_prev)
    else:
        wc = cache_k.shape[1]
        assert wc == WINDOW and s == CHUNK and PAST_LEN % CHUNK == 0
        bias = _band_bias(p['rel_bias'][e], s, [0])
        first = lambda i, t: (i, 0, 0, 0)
        yb = _attn(q, [(_to_pairs(cache_k), wc, first), (k, s, first)],
                   [(_to_pairs(cache_v), wc, first), (v, s, first)], bias, 1, 1)
    y = _outproj(x, ya, yb, p['ab_w_out'][e], tm)
    return (y, conv_new, h_new, k_tail.reshape(b, -1, heads, HEAD_DIM),
            v_tail.reshape(b, -1, heads, HEAD_DIM))


def _mixer(x, p, l, states, prompt):
    b, s, d = x.shape
    conv_st, lru_st, cache_k, cache_v, ssm_re_st, ssm_im_st = states
    if l % 2 == 0:
        e = l // 2
        w = p['conv_w'].shape[-1]
        if prompt:
            c_prev = jnp.zeros((b, CONV_WIDTH - 1, w), F32)
            h_prev = jnp.zeros((b, w), F32)
            ck = cv = None
        else:
            c_prev, h_prev, ck, cv = conv_st[e], lru_st[e], cache_k[e], cache_v[e]
        pe = dict(p, mix_norm_l=p['mix_norm'][l])
        x, c_new, h_new, k_new, v_new = _even_mixer(x, pe, e, c_prev, h_prev, ck, cv, prompt)
        return x, dict(conv=c_new, lru=h_new, k=k_new, v=v_new)
    o = l // 2
    g, st = p['ssm_A_re'].shape[1:]
    if prompt:
        x, n_re, n_im = _odd_mixer(x, p['mix_norm'][l], jnp.zeros((b, g, st), F32),
                                   jnp.zeros((b, g, st), F32), p['s5_prep'][o], p['ssm_D'][o],
                                   p['glu_w'][o], min(s // SEGMENTS, 64))
    else:
        x, n_re, n_im = _odd_mixer_seq(x, p['mix_norm'][l], ssm_re_st[o], ssm_im_st[o], p['s5_prep'][o],
                                       p['ssm_D'][o], p['glu_w'][o])
    return x, dict(re=n_re, im=n_im)


def _trunks(x_prompt, x_sample, p, sample_states):
    xs = [x_prompt, x_sample]
    d = x_prompt.shape[2]
    depth = p['ffn1_norm'].shape[0]
    outs = [dict(conv=[], lru=[], k=[], v=[], re=[], im=[]) for _ in xs]
    states = [(None,) * 6, sample_states]

    def ffn(norm, weights, l):
        ya, yb = _ffn(xs[0].reshape(-1, d), xs[1].reshape(-1, d), norm[l], *weights, l)
        return [ya.reshape(xs[0].shape), yb.reshape(xs[1].shape)]

    for l in range(depth):
        xs = ffn(p['ffn1_norm'], p['ffn1_w'], l)
        for i in range(2):
            xs[i], new = _mixer(xs[i], p, l, states[i], prompt=(i == 0))
            for name, val in new.items():
                outs[i][name].append(val)
        xs = ffn(p['ffn2_norm'], p['ffn2_w'], l)
    return [(x,) + tuple(jnp.stack(o[name]) for name in ('conv', 'lru', 'k', 'v', 're', 'im'))
            for x, o in zip(xs, outs)]


def kernel(x_prompt, x_sample, state_rglru_conv, state_rglru_h, cache_band_k, cache_band_v,
           state_ssm_re, state_ssm_im, ffn1_norm, ffn1_w_in, ffn1_w_out, mix_norm, ffn2_norm,
           ffn2_w_in, ffn2_w_out, ab_w_in, conv_w, conv_b, lru_wa, lru_ba, lru_wx, lru_bx,
           lru_lambda, q_norm, k_norm, rel_bias, ab_w_out, ssm_A_re, ssm_A_im, ssm_B_re, ssm_B_im,
           ssm_C_re, ssm_C_im, ssm_D, ssm_log_dt, glu_w):
    p = dict(ffn1_norm=ffn1_norm, mix_norm=mix_norm, ffn2_norm=ffn2_norm, ab_w_in=ab_w_in,
             conv_w=conv_w, conv_b=conv_b, lru_wa=lru_wa, lru_ba=lru_ba, lru_wx=lru_wx, lru_bx=lru_bx,
             lru_lambda=lru_lambda, q_norm=q_norm, k_norm=k_norm, rel_bias=rel_bias, ab_w_out=ab_w_out,
             ssm_A_re=ssm_A_re, ssm_A_im=ssm_A_im, ssm_B_re=ssm_B_re, ssm_B_im=ssm_B_im,
             ssm_C_re=ssm_C_re, ssm_C_im=ssm_C_im, ssm_D=ssm_D, ssm_log_dt=ssm_log_dt, glu_w=glu_w)
    p['ffn1_w'] = (ffn1_w_in.astype(BF16), ffn1_w_out.astype(BF16))
    p['ffn2_w'] = (ffn2_w_in.astype(BF16), ffn2_w_out.astype(BF16))
    p['s5_prep'] = [_prep_s5(ssm_A_re[o], ssm_A_im[o], ssm_B_re[o], ssm_B_im[o], ssm_C_re[o], ssm_C_im[o],
                             ssm_log_dt[o]) for o in range(ssm_A_re.shape[0])]
    (y_prompt, p_conv, p_h, p_k, p_v, p_re, p_im), (y_sample, s_conv, s_h, s_k, s_v, s_re, s_im) = _trunks(
        x_prompt, x_sample, p, (state_rglru_conv, state_rglru_h, cache_band_k, cache_band_v,
                                state_ssm_re, state_ssm_im))
    return (y_prompt, y_sample, p_conv, p_h, p_k, p_v, p_re, p_im, s_conv, s_h, s_k, s_v, s_re, s_im)
```

```python
import functools
import math

import jax
import jax.numpy as jnp
from jax import lax
from jax.experimental import pallas as pl
from jax.experimental.pallas import tpu as pltpu

F32 = jnp.float32
BF16 = jnp.bfloat16

LANES = 128
SUBLANES = 8
VMEM_LIMIT_BYTES = 56 * 1024 * 1024

EPS = 1e-6
CHUNK = 64
LEFT_CHUNKS = 8
WINDOW = LEFT_CHUNKS * CHUNK
MAX_REL = 128
PAST_LEN = 4096
HEAD_DIM = 64
CONV_WIDTH = 4
LRU_C = 8.0
SSM_GROUP = 16
SEGMENTS = SUBLANES
S5_GROUP_BLOCK = 16
S5_ENDS_STEPS = 4
S5_ENDS_ROWS = 1024
FFN_CHUNK = 256
ATTN_BLOCK = 256
ATTN_BLOCKS_PER_STEP = 4
PROJ_PIECE = 256
FFN_ROWS = 512
NEG_INF = -1e30


def _params(*sem):
    return pltpu.CompilerParams(dimension_semantics=sem, vmem_limit_bytes=VMEM_LIMIT_BYTES)


def _resident(shape):
    nd = len(shape)
    return pl.BlockSpec(shape, lambda *_: (0,) * nd, pipeline_mode=pl.Buffered(1))


def _rms(x, g):
    return x * lax.rsqrt(jnp.mean(x * x, axis=-1, keepdims=True) + EPS) * g


def _sigmoid(x):
    return 1.0 / (1.0 + jnp.exp(-x))


def _exact_zero_from(v):
    bits = pltpu.bitcast(v[:SUBLANES, :LANES].astype(F32), jnp.uint32)
    half = lax.shift_right_logical(bits, jnp.uint32(16))
    return pltpu.bitcast(lax.shift_right_logical(half, jnp.uint32(16)), F32)


def _ffn_body(xa_ref, xb_ref, g_ref, win_ref, wout_ref, oa_ref, ob_ref, acc_ref):
    d_ff = wout_ref.shape[0]
    first = pl.program_id(0) == 0
    x = jnp.where(first, xb_ref[...], xa_ref[...])
    h = _rms(x, g_ref[...]).astype(BF16)
    n_chunks = d_ff // FFN_CHUNK
    for c in range(n_chunks):
        cols = slice(c * FFN_CHUNK, (c + 1) * FFN_CHUNK)
        up_cols = slice(d_ff + c * FFN_CHUNK, d_ff + (c + 1) * FFN_CHUNK)
        gate = jnp.dot(h, win_ref[:, cols], preferred_element_type=F32)
        up = jnp.dot(h, win_ref[:, up_cols], preferred_element_type=F32)
        act = (gate * _sigmoid(gate) * up).astype(BF16)
        y = jnp.dot(act, wout_ref[cols, :], preferred_element_type=F32)
        if c == 0:
            acc_ref[...] = y
        elif c < n_chunks - 1:
            acc_ref[...] += y
        else:
            oa_ref[...] = x + 0.5 * (acc_ref[...] + y)

    @pl.when(first)
    def _():
        ob_ref[...] = oa_ref[...]


def _ffn(xa, xb, g, w_in_all, w_out_all, layer):
    n, d = xa.shape
    tm = FFN_ROWS
    _, d_ff, _ = w_out_all.shape
    assert d_ff % FFN_CHUNK == 0 and n % tm == 0 and xb.shape == (tm, d)
    a_spec = pl.BlockSpec((tm, d), lambda i: (jnp.maximum(i - 1, 0), 0))
    b_spec = pl.BlockSpec((tm, d), lambda i: (0, 0))
    return pl.pallas_call(
        _ffn_body,
        grid=(n // tm + 1,),
        in_specs=[
            a_spec, b_spec, _resident((1, d)),
            pl.BlockSpec((None, d, 2 * d_ff), lambda i: (layer, 0, 0), pipeline_mode=pl.Buffered(1)),
            pl.BlockSpec((None, d_ff, d), lambda i: (layer, 0, 0), pipeline_mode=pl.Buffered(1)),
        ],
        out_specs=[a_spec, b_spec],
        out_shape=[jax.ShapeDtypeStruct((n, d), F32), jax.ShapeDtypeStruct((tm, d), F32)],
        scratch_shapes=[pltpu.VMEM((tm, d), F32)],
        compiler_params=_params("arbitrary"),
        name="ffn",
    )(xa, xb, g.reshape(1, d), w_in_all, w_out_all)


def _head_norm(t, gain):
    low = lax.broadcasted_iota(jnp.int32, (t.shape[0], LANES), 1) < HEAD_DIM
    outs = []
    for j in range(t.shape[1] // LANES):
        blk = t[:, j * LANES:(j + 1) * LANES]
        sq = blk * blk
        tot = jnp.sum(sq, axis=-1, keepdims=True)
        lo = jnp.sum(jnp.where(low, sq, 0.0), axis=-1, keepdims=True)
        ms = jnp.where(low, lo, tot - lo) * (1.0 / HEAD_DIM)
        outs.append(blk * lax.rsqrt(ms + EPS) * gain[:, j * LANES:(j + 1) * LANES])
    return jnp.concatenate(outs, axis=-1)


def _block_diag(w):
    nb, n, _ = w.shape
    eye = jnp.eye(nb, dtype=w.dtype)
    return jnp.einsum('hij,hg->higj', w, eye).reshape(nb * n, nb * n)


def _rg_lru(xa, ga, c0_ref, h0_ref, cw_ref, cb_ref, wa_ref, ba_ref, wx_ref, bx_ref, lam_ref,
            ya_ref, hlast_ref, ctail_ref, ext_ref, hc_ref, a_ref, u_ref, interleave=()):
    t_len, w = xa.shape

    @pl.when(pl.program_id(1) == 0)
    def _():
        ext_ref[0:SUBLANES, :] = c0_ref[0]
        hc_ref[...] = jnp.broadcast_to(h0_ref[0], hc_ref.shape)

    ext_ref[SUBLANES:SUBLANES + t_len, :] = xa
    xc = cb_ref[...] + cw_ref[3:4, :] * xa
    for k in range(CONV_WIDTH - 1):
        off = SUBLANES - (CONV_WIDTH - 1) + k
        xc = xc + cw_ref[k:k + 1, :] * ext_ref[off:off + t_len, :]
    tail = xa[t_len - SUBLANES:, :]
    ext_ref[0:SUBLANES, :] = tail
    ctail_ref[0] = tail

    xcb = xc.astype(BF16)
    r = _sigmoid(jnp.dot(xcb, wa_ref[...], preferred_element_type=F32) + ba_ref[...])
    i = _sigmoid(jnp.dot(xcb, wx_ref[...], preferred_element_type=F32) + bx_ref[...])
    neg_lam = -lam_ref[...]
    softplus = jnp.maximum(neg_lam, 0.0) + jnp.log1p(jnp.exp(-jnp.abs(neg_lam)))
    log_a = -LRU_C * r * softplus
    a = jnp.exp(log_a)
    a_ref[...] = a
    u_ref[...] = jnp.sqrt(-jnp.tanh(log_a) * (a * a + 1.0)) * (i * xc)

    row = lax.broadcasted_iota(jnp.int32, (SUBLANES, w), 0)
    h = hc_ref[...]
    n_blk = t_len // SUBLANES
    for blk in range(n_blk):
        if interleave and blk % (n_blk // len(interleave)) == 0 and blk // (n_blk // len(interleave)) < len(interleave):
            zero = _exact_zero_from(interleave[blk // (n_blk // len(interleave))]())
            h = h + jnp.concatenate([zero] * (w // LANES), axis=-1)
        rows = slice(blk * SUBLANES, (blk + 1) * SUBLANES)
        a = a_ref[rows, :]
        u = u_ref[rows, :]
        for d in (1, 2, 4):
            keep = row >= d
            a_sh = jnp.where(keep, pltpu.roll(a, d, 0), 1.0)
            u_sh = jnp.where(keep, pltpu.roll(u, d, 0), 0.0)
            u = u + a * u_sh
            a = a * a_sh
        hs = a * h + u
        ya_ref[0, rows, :] = (hs * jax.nn.gelu(ga[rows, :])).astype(ya_ref.dtype)
        h = jnp.broadcast_to(hs[SUBLANES - 1:SUBLANES, :], (SUBLANES, w))
    hc_ref[...] = h
    hlast_ref[0] = h


def _inproj_body(x_ref, g_ref, w_ref, qg_ref, kg_ref, c0_ref, h0_ref, cw_ref, cb_ref, wa_ref,
                 ba_ref, wx_ref, bx_ref, lam_ref, ya_ref, q_ref, k_ref, v_ref, kt_ref, vt_ref,
                 hlast_ref, ctail_ref, ext_ref, hc_ref, a_ref, u_ref, pb_ref):
    w = kt_ref.shape[2]
    h = _rms(x_ref[0], g_ref[...]).astype(BF16)
    pa = jnp.dot(h, w_ref[:, :2 * w], preferred_element_type=F32)
    n_pieces = 3 * w // PROJ_PIECE

    def piece(j):
        cols = slice(j * PROJ_PIECE, (j + 1) * PROJ_PIECE)
        pb_ref[:, cols] = jnp.dot(h, w_ref[:, 2 * w + j * PROJ_PIECE:2 * w + (j + 1) * PROJ_PIECE],
                                  preferred_element_type=F32)
        return pb_ref[:SUBLANES, j * PROJ_PIECE:j * PROJ_PIECE + LANES]

    _rg_lru(pa[:, :w], pa[:, w:], c0_ref, h0_ref, cw_ref, cb_ref, wa_ref, ba_ref, wx_ref, bx_ref,
            lam_ref, ya_ref, hlast_ref, ctail_ref, ext_ref, hc_ref, a_ref, u_ref,
            interleave=[functools.partial(piece, j) for j in range(n_pieces)])
    q = _head_norm(pb_ref[:, 0 * w:1 * w], qg_ref[...]) * (1.0 / math.sqrt(HEAD_DIM))
    k = _head_norm(pb_ref[:, 1 * w:2 * w], kg_ref[...])
    v = pb_ref[:, 2 * w:3 * w]
    kt_ref[0] = k
    vt_ref[0] = v
    for pr in range(w // LANES):
        lanes = slice(pr * LANES, (pr + 1) * LANES)
        q_ref[0, pr] = q[:, lanes].astype(BF16)
        k_ref[0, pr] = k[:, lanes].astype(BF16)
        v_ref[0, pr] = v[:, lanes].astype(BF16)


def _inproj(x, g, w_in, q_gain, k_gain, conv0, h0, conv_w, conv_b, wa, ba, wx, bx, lam, tm):
    b, s, d = x.shape
    w = w_in.shape[1] // 5
    heads = w // HEAD_DIM
    npair = w // LANES
    assert s % tm == 0 and tm % SUBLANES == 0
    pad = jnp.zeros((b, SUBLANES - (CONV_WIDTH - 1), w), F32)
    c0 = jnp.concatenate([pad, conv0.astype(F32)], axis=1)
    row = pl.BlockSpec((1, tm, w), lambda i, t: (i, t, 0))
    pair = pl.BlockSpec((1, npair, tm, LANES), lambda i, t: (i, 0, t, 0))
    tail = pl.BlockSpec((1, tm, w), lambda i, t: (i, 0, 0))
    state = pl.BlockSpec((1, SUBLANES, w), lambda i, t: (i, 0, 0))
    vec = _resident((1, w))
    return pl.pallas_call(
        _inproj_body,
        grid=(b, s // tm),
        in_specs=[
            pl.BlockSpec((1, tm, d), lambda i, t: (i, t, 0)),
            _resident((1, d)), _resident(w_in.shape), vec, vec,
            state, pl.BlockSpec((1, 1, w), lambda i, t: (i, 0, 0)),
            _resident((CONV_WIDTH, w)), vec, _resident((w, w)), vec, _resident((w, w)), vec, vec,
        ],
        out_specs=[row, pair, pair, pair, tail, tail, state, state],
        out_shape=[jax.ShapeDtypeStruct((b, s, w), BF16)]
        + [jax.ShapeDtypeStruct((b, npair, s, LANES), BF16)] * 3
        + [jax.ShapeDtypeStruct((b, tm, w), F32)] * 2
        + [jax.ShapeDtypeStruct((b, SUBLANES, w), F32)] * 2,
        scratch_shapes=[
            pltpu.VMEM((SUBLANES + tm, w), F32),
            pltpu.VMEM((SUBLANES, w), F32),
            pltpu.VMEM((tm, w), F32),
            pltpu.VMEM((tm, w), F32),
            pltpu.VMEM((tm, 3 * w), F32),
        ],
        compiler_params=_params("parallel", "arbitrary"),
        name="inproj",
    )(x, g.reshape(1, d), w_in.astype(BF16),
      jnp.tile(q_gain, heads).reshape(1, w), jnp.tile(k_gain, heads).reshape(1, w),
      c0, h0.astype(F32).reshape(b, 1, w), conv_w, conv_b.reshape(1, w),
      _block_diag(wa).astype(BF16), ba.reshape(1, w), _block_diag(wx).astype(BF16),
      bx.reshape(1, w), lam.reshape(1, w))


def _attn_body(*refs, n_prev, n_blocks):
    n_parts = n_prev + n_blocks
    q_ref = refs[0]
    k_parts = refs[1:1 + n_parts]
    v_parts = refs[1 + n_parts:1 + 2 * n_parts]
    bias_ref, o_ref = refs[1 + 2 * n_parts:]
    last_variant = bias_ref.shape[0] - 1
    npair = q_ref.shape[1]
    nq = q_ref.shape[2] // n_blocks
    low = lax.broadcasted_iota(jnp.int32, (nq, LANES), 1) < HEAD_DIM
    for pr in range(npair):
        for j in range(n_blocks):
            rows = slice(j * nq, (j + 1) * nq)
            qp = q_ref[0, pr, rows, :]
            zero = jnp.zeros_like(qp)
            qs = jnp.concatenate([jnp.where(low, qp, zero), jnp.where(low, zero, qp)], axis=0)
            kw = jnp.concatenate([part[0, pr] for part in k_parts[j:j + n_prev + 1]], axis=0)
            vw = jnp.concatenate([part[0, pr] for part in v_parts[j:j + n_prev + 1]], axis=0)
            s = lax.dot_general(qs, kw, (((1,), (1,)), ((), ())), preferred_element_type=F32)
            variant = jnp.minimum(n_blocks * pl.program_id(1) + j, last_variant)
            s = s + bias_ref[variant, pr]
            p = jnp.exp(s - jnp.max(s, axis=-1, keepdims=True))
            denom = jnp.sum(p, axis=-1, keepdims=True)
            o = jnp.dot(p.astype(BF16), vw, preferred_element_type=F32) / denom
            o_ref[0, pr, rows, :] = jnp.where(low, o[:nq], o[nq:]).astype(o_ref.dtype)


def _band_bias(rel_bias, nq, invalid_cols):
    heads = rel_bias.shape[1]
    win = WINDOW + nq
    period = win + nq
    n_hi = WINDOW - MAX_REL
    n_lo = max(win - (WINDOW + MAX_REL + 1), 0)
    tab = rel_bias.astype(F32)
    mid = jnp.flip(tab, axis=0)[:win - n_hi - n_lo]
    vec = jnp.concatenate([jnp.broadcast_to(tab[-1:], (n_hi, heads)), mid,
                           jnp.broadcast_to(tab[:1], (n_lo, heads)),
                           jnp.broadcast_to(tab[-1:], (nq, heads))], axis=0)
    toep = jnp.tile(vec.T, (1, nq))[:, :nq * (period - 1)].reshape(heads, nq, period - 1)[:, :, :win]
    qc = jnp.arange(nq)[:, None] // CHUNK
    kc = jnp.arange(win)[None, :] // CHUNK
    band = (kc >= qc) & (kc <= qc + LEFT_CHUNKS)
    col = jnp.arange(win)[None, :]
    out = [jnp.where(band & (col >= c), toep, NEG_INF) for c in invalid_cols]
    return jnp.stack(out).reshape(len(invalid_cols), heads // 2, 2 * nq, win)


def _attn(q, k_parts, v_parts, bias, n_blocks, n_prev):
    b, npair, s, _ = q.shape
    assert len(k_parts) == len(v_parts) == n_prev + n_blocks
    rows = sum(r for _, r, _ in k_parts[n_prev:])

    def spec(nrows, index_map):
        return pl.BlockSpec((1, npair, nrows, LANES), index_map)

    own = spec(rows, lambda i, t: (i, 0, t, 0))
    return pl.pallas_call(
        functools.partial(_attn_body, n_prev=n_prev, n_blocks=n_blocks),
        grid=(b, s // rows),
        in_specs=([own] + [spec(r, m) for _, r, m in k_parts] + [spec(r, m) for _, r, m in v_parts]
                  + [_resident(bias.shape)]),
        out_specs=own,
        out_shape=jax.ShapeDtypeStruct((b, npair, s, LANES), BF16),
        compiler_params=_params("parallel", "arbitrary"),
        name="attn",
    )(q, *[a for a, _, _ in k_parts], *[a for a, _, _ in v_parts], bias)


def _outproj_body(x_ref, ya_ref, yb_ref, w_ref, o_ref):
    w = ya_ref.shape[2]
    yb = jnp.concatenate([yb_ref[0, pr] for pr in range(yb_ref.shape[1])], axis=-1)
    y = jnp.dot(ya_ref[0], w_ref[:w, :], preferred_element_type=F32)
    y = y + jnp.dot(yb, w_ref[w:, :], preferred_element_type=F32)
    o_ref[0] = x_ref[0] + y


def _outproj(x, ya, yb, w_out, tm):
    b, s, d = x.shape
    w = ya.shape[2]
    npair = yb.shape[1]
    return pl.pallas_call(
        _outproj_body,
        grid=(b, s // tm),
        in_specs=[
            pl.BlockSpec((1, tm, d), lambda i, t: (i, t, 0)),
            pl.BlockSpec((1, tm, w), lambda i, t: (i, t, 0)),
            pl.BlockSpec((1, npair, tm, LANES), lambda i, t: (i, 0, t, 0)),
            _resident(w_out.shape),
        ],
        out_specs=pl.BlockSpec((1, tm, d), lambda i, t: (i, t, 0)),
        out_shape=jax.ShapeDtypeStruct((b, s, d), F32),
        compiler_params=_params("parallel", "parallel"),
        name="outproj",
    )(x, ya, yb, w_out.astype(BF16))


def _s5_load_h(x_ref, g_ref, hnat, hperm):
    tm = x_ref.shape[2]
    pitch = hnat.shape[1] // SEGMENTS
    n_lane_blocks = hnat.shape[0]
    for r in range(SEGMENTS):
        h = _rms(x_ref[0, r], g_ref[...])
        for c in range(n_lane_blocks):
            hnat[c, r * pitch:r * pitch + tm, :] = h[:, c * LANES:(c + 1) * LANES]
    for m in range(tm):
        for c in range(n_lane_blocks):
            hperm[m * SEGMENTS:(m + 1) * SEGMENTS, c * LANES:(c + 1) * LANES] = (
                hnat[c, pl.ds(m, SEGMENTS, stride=pitch), :])


def _s5_pitch(tm):
    assert tm % SUBLANES == 0
    return tm + SUBLANES // 2


def _s5_scan(bu, sbuf, st, are_ref, aim_ref, jb, tm):
    half = bu.shape[1] // 2
    base = jb * bu.shape[1]
    step = 4 * LANES
    for c0 in range(0, half, step):
        ar = are_ref[jb, :, c0:c0 + step]
        ai = aim_ref[jb, :, c0:c0 + step]
        sr = st[:, base + c0:base + c0 + step]
        si = st[:, base + half + c0:base + half + c0 + step]
        for m in range(tm):
            rows = slice(m * SEGMENTS, (m + 1) * SEGMENTS)
            nr = ar * sr - ai * si + bu[rows, c0:c0 + step]
            ni = ar * si + ai * sr + bu[rows, half + c0:half + c0 + step]
            if sbuf is not None:
                sbuf[rows, c0:c0 + step] = nr
                sbuf[rows, half + c0:half + c0 + step] = ni
            sr, si = nr, ni
        st[:, base + c0:base + c0 + step] = sr
        st[:, base + half + c0:base + half + c0 + step] = si


def _s5_ends_body(x_ref, g_ref, bblk_ref, ajr_ref, aji_ref, ends_ref, hnat, hperm, bu0, bu1, st):
    tm, d = x_ref.shape[2], x_ref.shape[3]
    nb, kbj, _ = bblk_ref.shape
    kb = d // nb
    steps = kbj // kb
    pitch = hnat.shape[1] // SEGMENTS
    bus = (bu0, bu1)

    @pl.when(pl.program_id(1) == 0)
    def _():
        st[...] = jnp.zeros_like(st)

    for r in range(SEGMENTS):
        h = _rms(x_ref[0, r], g_ref[...])
        for c in range(hnat.shape[0]):
            hnat[c, r * pitch:r * pitch + tm, :] = h[:, c * LANES:(c + 1) * LANES]
    for mb in range(tm // steps):
        for j in range(steps):
            for c in range(hnat.shape[0]):
                jb, within = divmod(c * LANES, kb)
                dst = jb * kbj + j * kb + within
                hperm[mb * SEGMENTS:(mb + 1) * SEGMENTS, dst:dst + LANES] = (
                    hnat[c, pl.ds(mb * steps + j, SEGMENTS, stride=pitch), :])
    hb = hperm[...].astype(BF16)

    def b_u(jb):
        return jnp.dot(hb[:, jb * kbj:(jb + 1) * kbj], bblk_ref[jb], preferred_element_type=F32)

    bus[0][...] = b_u(0)
    for jb in range(nb):
        if jb + 1 < nb:
            bus[(jb + 1) % 2][...] = b_u(jb + 1)
        _s5_scan(bus[jb % 2], None, st, ajr_ref, aji_ref, jb, tm // steps)

    @pl.when(pl.program_id(1) == pl.num_programs(1) - 1)
    def _():
        ends_ref[0] = st[...]


def _s5_main_body(x_ref, g_ref, bblk_ref, cblk_ref, are_ref, aim_ref, apr_ref, api_ref, d_ref,
                  glu_ref, ends_ref, s0_ref, o_ref, slast_ref, hnat, hperm, bu0, bu1, sb0, sb1,
                  ybuf, st):
    tm = x_ref.shape[2]
    bus = (bu0, bu1)
    sbs = (sb0, sb1)
    d = x_ref.shape[3]
    nb, kb, two_half = bblk_ref.shape
    half = two_half // 2

    @pl.when(pl.program_id(1) == 0)
    def _():
        for jb in range(nb):
            re_cols = slice(jb * two_half, jb * two_half + half)
            im_cols = slice(jb * two_half + half, (jb + 1) * two_half)
            pr = apr_ref[jb, 0:1, :]
            pi = api_ref[jb, 0:1, :]
            er = s0_ref[0, :, re_cols]
            ei = s0_ref[0, :, im_cols]
            for r in range(SEGMENTS):
                st[r:r + 1, re_cols] = er
                st[r:r + 1, im_cols] = ei
                nr = pr * er - pi * ei + ends_ref[0, r:r + 1, re_cols]
                ni = pr * ei + pi * er + ends_ref[0, r:r + 1, im_cols]
                er, ei = nr, ni
            slast_ref[0, :, re_cols] = er
            slast_ref[0, :, im_cols] = ei

    _s5_load_h(x_ref, g_ref, hnat, hperm)
    h = hperm[...]
    hb = h.astype(BF16)

    def b_u(jb):
        return jnp.dot(hb[:, jb * kb:(jb + 1) * kb], bblk_ref[jb], preferred_element_type=F32)

    bus[0][...] = b_u(0)
    for jb in range(nb):
        if jb + 1 < nb:
            bus[(jb + 1) % 2][...] = b_u(jb + 1)
        _s5_scan(bus[jb % 2], sbs[jb % 2], st, are_ref, aim_ref, jb, tm)
        ybuf[:, jb * kb:(jb + 1) * kb] = jnp.dot(sbs[jb % 2][...].astype(BF16), cblk_ref[jb],
                                                 preferred_element_type=F32)
    y = ybuf[...] + d_ref[...] * h
    z = jnp.dot(y.astype(BF16), glu_ref[...], preferred_element_type=F32)
    o = z[:, :d] * _sigmoid(z[:, d:])
    for c in range(hnat.shape[0]):
        hnat[c, 0:SEGMENTS * tm, :] = o[:, c * LANES:(c + 1) * LANES]
    for r in range(SEGMENTS):
        for c in range(hnat.shape[0]):
            lanes = slice(c * LANES, (c + 1) * LANES)
            o_ref[0, r, :, lanes] = x_ref[0, r, :, lanes] + hnat[c, pl.ds(r, tm, stride=SEGMENTS), :]


def _s5_expand_body(bc_ref, cc_ref, bblk_ref, bends_ref, cblk_ref):
    n_b, _, kb, p = bc_ref.shape
    i_dim = cc_ref.shape[3]
    half = bblk_ref.shape[2] // 2
    log_p, log_i = p.bit_length() - 1, i_dim.bit_length() - 1
    assert p == 1 << log_p and i_dim == 1 << log_i

    def iota(shape, axis):
        return lax.broadcasted_iota(jnp.int32, shape, axis)

    sel_b = jnp.where((iota((p, half), 1) & (p - 1)) == iota((p, half), 0), 1.0, 0.0).astype(BF16)
    mask_b = (iota((kb, half), 0) >> log_i) == (iota((kb, half), 1) >> log_p)
    sel_c = jnp.where((iota((i_dim, kb), 1) & (i_dim - 1)) == iota((i_dim, kb), 0), 1.0, 0.0).astype(BF16)
    mask_c = (iota((half, kb), 0) >> log_p) == (iota((half, kb), 1) >> log_i)

    def expand_b(m):
        t = jnp.dot(bc_ref[m, 0].astype(BF16), sel_b, preferred_element_type=F32)
        return jnp.where(mask_b, t, 0.0).astype(BF16)

    bblk_ref[0, :, :half] = expand_b(0)
    bblk_ref[0, :, half:] = expand_b(1)
    for j in range((n_b - 2) // 2):
        bends_ref[0, j * kb:(j + 1) * kb, :half] = expand_b(2 + 2 * j)
        bends_ref[0, j * kb:(j + 1) * kb, half:] = expand_b(3 + 2 * j)
    for m in range(2):
        t = jnp.dot(cc_ref[m, 0].astype(BF16), sel_c, preferred_element_type=F32)
        cblk_ref[0, m * half:(m + 1) * half, :] = jnp.where(mask_c, t, 0.0).astype(BF16)


def _prep_s5(a_re, a_im, b_re, b_im, c_re, c_im, log_dt):
    g, p = a_re.shape
    gb = S5_GROUP_BLOCK
    nb = g // gb
    a = lax.complex(a_re.astype(F32), a_im.astype(F32))
    dt = jnp.exp(log_dt.astype(F32))[:, None]
    a_bar = jnp.exp(a * dt)
    b_bar = ((a_bar - 1.0) / a)[..., None] * lax.complex(b_re.astype(F32), b_im.astype(F32))
    mats = [b_bar]
    for _ in range(S5_ENDS_STEPS - 1):
        mats.append(mats[-1] * a_bar[..., None])
    mats = [b_bar] + mats[::-1]
    bc = jnp.stack([part for m in mats for part in (m.real, m.imag)])
    bc = bc.transpose(0, 1, 3, 2).reshape(len(mats) * 2, nb, gb * SSM_GROUP, p)
    cc = jnp.stack([c_re.astype(F32), -c_im.astype(F32)]).transpose(0, 1, 3, 2)
    cc = cc.reshape(2, nb, gb * p, SSM_GROUP)
    kb, half = gb * SSM_GROUP, gb * p
    bblk, bblk_ends, cblk = pl.pallas_call(
        _s5_expand_body,
        grid=(nb,),
        in_specs=[pl.BlockSpec((bc.shape[0], 1, kb, p), lambda j: (0, j, 0, 0)),
                  pl.BlockSpec((2, 1, half, SSM_GROUP), lambda j: (0, j, 0, 0))],
        out_specs=[pl.BlockSpec((1, kb, 2 * half), lambda j: (j, 0, 0)),
                   pl.BlockSpec((1, S5_ENDS_STEPS * kb, 2 * half), lambda j: (j, 0, 0)),
                   pl.BlockSpec((1, 2 * half, kb), lambda j: (j, 0, 0))],
        out_shape=[jax.ShapeDtypeStruct((nb, kb, 2 * half), BF16),
                   jax.ShapeDtypeStruct((nb, S5_ENDS_STEPS * kb, 2 * half), BF16),
                   jax.ShapeDtypeStruct((nb, 2 * half, kb), BF16)],
        compiler_params=_params("parallel"),
        name="s5_expand",
    )(bc, cc)
    return a_bar, bblk, bblk_ends, cblk


def _s5_lanes(v):
    g, p = v.shape
    nb = g // S5_GROUP_BLOCK
    return jnp.broadcast_to(v.reshape(nb, 1, S5_GROUP_BLOCK * p), (nb, SUBLANES, S5_GROUP_BLOCK * p))


def _s5_seq_body(x_ref, g_ref, bblk_ref, cblk_ref, are_ref, aim_ref, d_ref, glu_ref, s0_ref,
                 o_ref, slast_ref, hnat, hperm, bu0, bu1, sb0, sb1, ybuf, st):
    nseq, tt, d = x_ref.shape
    nb, kb, _ = bblk_ref.shape
    pitch = hnat.shape[1] // nseq
    bus = (bu0, bu1)
    sbs = (sb0, sb1)
    st[...] = s0_ref[...]
    for b in range(nseq):
        hb_nat = _rms(x_ref[b], g_ref[...])
        for c in range(hnat.shape[0]):
            hnat[c, b * pitch:b * pitch + tt, :] = hb_nat[:, c * LANES:(c + 1) * LANES]
    for t in range(tt):
        for c in range(hnat.shape[0]):
            hperm[t * nseq:(t + 1) * nseq, c * LANES:(c + 1) * LANES] = (
                hnat[c, pl.ds(t, nseq, stride=pitch), :])
    h = hperm[...]
    hb = h.astype(BF16)

    def b_u(jb):
        return jnp.dot(hb[:, jb * kb:(jb + 1) * kb], bblk_ref[jb], preferred_element_type=F32)

    bus[0][...] = b_u(0)
    for jb in range(nb):
        if jb + 1 < nb:
            bus[(jb + 1) % 2][...] = b_u(jb + 1)
        _s5_scan(bus[jb % 2], sbs[jb % 2], st, are_ref, aim_ref, jb, tt)
        ybuf[:, jb * kb:(jb + 1) * kb] = jnp.dot(sbs[jb % 2][...].astype(BF16), cblk_ref[jb],
                                                 preferred_element_type=F32)
    y = ybuf[...] + d_ref[...] * h
    z = jnp.dot(y.astype(BF16), glu_ref[...], preferred_element_type=F32)
    o = z[:, :d] * _sigmoid(z[:, d:])
    for c in range(hnat.shape[0]):
        hnat[c, 0:nseq * tt, :] = o[:, c * LANES:(c + 1) * LANES]
    for b in range(nseq):
        for c in range(hnat.shape[0]):
            lanes = slice(c * LANES, (c + 1) * LANES)
            o_ref[b, :, lanes] = x_ref[b, :, lanes] + hnat[c, pl.ds(b, tt, stride=nseq), :]
    slast_ref[...] = st[...]


def _s5_state_to_lanes(s_re, s_im):
    b, g, p = s_re.shape
    nb = g // S5_GROUP_BLOCK
    both = jnp.stack([s_re.reshape(b, nb, S5_GROUP_BLOCK * p), s_im.reshape(b, nb, S5_GROUP_BLOCK * p)],
                     axis=2)
    return both.reshape(b, 1, 2 * g * p).astype(F32)


def _s5_state_from_lanes(s, g, p):
    b = s.shape[0]
    both = s.reshape(b, g // S5_GROUP_BLOCK, 2, S5_GROUP_BLOCK, p)
    return both[:, :, 0].reshape(b, g, p), both[:, :, 1].reshape(b, g, p)


def _odd_mixer(x, norm_g, s_re, s_im, prep, d_skip, glu_w, tm):
    b, s, d = x.shape
    g, p = s_re.shape[1:]
    seg_len = s // SEGMENTS
    a_bar, bblk, bblk_ends, cblk = prep
    a_pow = a_bar
    for _ in range(int(math.log2(seg_len))):
        a_pow = a_pow * a_pow
    a_ends = a_bar
    for _ in range(S5_ENDS_STEPS - 1):
        a_ends = a_ends * a_bar
    are, aim, apr, api, ajr, aji = (_s5_lanes(v) for v in (a_bar.real, a_bar.imag, a_pow.real,
                                                            a_pow.imag, a_ends.real, a_ends.imag))
    nb, kb, two_half = bblk.shape
    n_state = nb * two_half
    xv = x.reshape(b, SEGMENTS, seg_len, d)
    rows = SEGMENTS * tm
    grid = (b, seg_len // tm)
    x_spec = pl.BlockSpec((1, SEGMENTS, tm, d), lambda i, t: (i, 0, t, 0))
    lane_spec = _resident(are.shape)
    g2 = norm_g.reshape(1, d)

    tm_e = min(seg_len, S5_ENDS_ROWS // SEGMENTS)
    assert seg_len % tm_e == 0 and tm_e % S5_ENDS_STEPS == 0
    ends = pl.pallas_call(
        _s5_ends_body,
        grid=(b, seg_len // tm_e),
        in_specs=[pl.BlockSpec((1, SEGMENTS, tm_e, d), lambda i, t: (i, 0, t, 0)), _resident((1, d)),
                  _resident(bblk_ends.shape), lane_spec, lane_spec],
        out_specs=pl.BlockSpec((1, SEGMENTS, n_state), lambda i, t: (i, 0, 0)),
        out_shape=jax.ShapeDtypeStruct((b, SEGMENTS, n_state), F32),
        scratch_shapes=[
            pltpu.VMEM((d // LANES, SEGMENTS * _s5_pitch(tm_e), LANES), F32),
            pltpu.VMEM((SEGMENTS * tm_e // S5_ENDS_STEPS, S5_ENDS_STEPS * d), F32),
            pltpu.VMEM((SEGMENTS * tm_e // S5_ENDS_STEPS, two_half), F32),
            pltpu.VMEM((SEGMENTS * tm_e // S5_ENDS_STEPS, two_half), F32),
            pltpu.VMEM((SEGMENTS, n_state), F32),
        ],
        compiler_params=_params("parallel", "arbitrary"),
        name="s5_ends",
    )(xv, g2, bblk_ends, ajr, aji)

    out, s_last = pl.pallas_call(
        _s5_main_body,
        grid=grid,
        in_specs=[
            x_spec, _resident((1, d)), _resident(bblk.shape), _resident(cblk.shape),
            lane_spec, lane_spec, lane_spec, lane_spec, _resident((1, d)),
            _resident(glu_w.shape),
            pl.BlockSpec((1, SEGMENTS, n_state), lambda i, t: (i, 0, 0)),
            pl.BlockSpec((1, 1, n_state), lambda i, t: (i, 0, 0)),
        ],
        out_specs=[x_spec, pl.BlockSpec((1, 1, n_state), lambda i, t: (i, 0, 0))],
        out_shape=[jax.ShapeDtypeStruct(xv.shape, F32),
                   jax.ShapeDtypeStruct((b, 1, n_state), F32)],
        scratch_shapes=[
            pltpu.VMEM((d // LANES, SEGMENTS * _s5_pitch(tm), LANES), F32), pltpu.VMEM((rows, d), F32),
            pltpu.VMEM((rows, two_half), F32), pltpu.VMEM((rows, two_half), F32),
            pltpu.VMEM((rows, two_half), F32), pltpu.VMEM((rows, two_half), F32),
            pltpu.VMEM((rows, d), F32), pltpu.VMEM((SEGMENTS, n_state), F32),
        ],
        compiler_params=_params("parallel", "arbitrary"),
        name="s5_main",
    )(xv, g2, bblk, cblk, are, aim, apr, api, d_skip.reshape(1, d).astype(F32),
      glu_w.astype(BF16), ends, _s5_state_to_lanes(s_re, s_im))
    n_re, n_im = _s5_state_from_lanes(s_last, g, p)
    return out.reshape(b, s, d), n_re, n_im


def _odd_mixer_seq(x, norm_g, s_re, s_im, prep, d_skip, glu_w):
    nseq, tt, d = x.shape
    g, p = s_re.shape[1:]
    a_bar, bblk, _, cblk = prep
    are, aim = _s5_lanes(a_bar.real), _s5_lanes(a_bar.imag)
    nb, kb, two_half = bblk.shape
    n_state = nb * two_half
    rows = nseq * tt
    assert nseq == SUBLANES and tt % SUBLANES == 0
    s0 = _s5_state_to_lanes(s_re, s_im).reshape(nseq, n_state)
    out, s_last = pl.pallas_call(
        _s5_seq_body,
        out_shape=[jax.ShapeDtypeStruct(x.shape, F32), jax.ShapeDtypeStruct((nseq, n_state), F32)],
        scratch_shapes=[
            pltpu.VMEM((d // LANES, nseq * _s5_pitch(tt), LANES), F32), pltpu.VMEM((rows, d), F32),
            pltpu.VMEM((rows, two_half), F32), pltpu.VMEM((rows, two_half), F32),
            pltpu.VMEM((rows, two_half), F32), pltpu.VMEM((rows, two_half), F32),
            pltpu.VMEM((rows, d), F32), pltpu.VMEM((SUBLANES, n_state), F32),
        ],
        compiler_params=pltpu.CompilerParams(vmem_limit_bytes=VMEM_LIMIT_BYTES),
        name="s5_seq",
    )(x, norm_g.reshape(1, d), bblk, cblk, are, aim, d_skip.reshape(1, d).astype(F32),
      glu_w.astype(BF16), s0)
    n_re, n_im = _s5_state_from_lanes(s_last.reshape(nseq, 1, n_state), g, p)
    return out, n_re, n_im


def _to_pairs(t):
    b, rows, heads, hd = t.shape
    return t.reshape(b, rows, heads // 2, 2 * hd).transpose(0, 2, 1, 3).astype(BF16)


def _even_mixer(x, p, e, conv0, h0, cache_k, cache_v, prompt):
    b, s, d = x.shape
    tm = min(s, WINDOW)
    ya, q, k, v, k_tail, v_tail, h_last, c_tail = _inproj(
        x, p['mix_norm_l'], p['ab_w_in'][e], p['q_norm'][e], p['k_norm'][e], conv0, h0,
        p['conv_w'][e], p['conv_b'][e], p['lru_wa'][e], p['lru_ba'][e], p['lru_wx'][e],
        p['lru_bx'][e], p['lru_lambda'][e], tm)
    w = ya.shape[2]
    heads = w // HEAD_DIM
    h_new = h_last[:, 0, :]
    conv_new = c_tail[:, SUBLANES - (CONV_WIDTH - 1):, :]
    if prompt:
        nq = ATTN_BLOCK
        n_prev = WINDOW // nq
        nblk = ATTN_BLOCKS_PER_STEP
        bias = _band_bias(p['rel_bias'][e], nq, [WINDOW - nq * v for v in range(n_prev + 1)])
        maps = [functools.partial(lambda i, t, off: (i, 0, jnp.maximum(nblk * t + off, 0), 0),
                                  off=idx - n_prev) for idx in range(n_prev + nblk)]
        yb = _attn(q, [(k, nq, m) for m in maps], [(v, nq, m) for m in maps], bias, nblk, n_prev)
    else:
        wc = cache_k.shape[1]
        assert wc == WINDOW and s == CHUNK and PAST_LEN % CHUNK == 0
        bias = _band_bias(p['rel_bias'][e], s, [0])
        first = lambda i, t: (i, 0, 0, 0)
        yb = _attn(q, [(_to_pairs(cache_k), wc, first), (k, s, first)],
                   [(_to_pairs(cache_v), wc, first), (v, s, first)], bias, 1, 1)
    y = _outproj(x, ya, yb, p['ab_w_out'][e], tm)
    return (y, conv_new, h_new, k_tail.reshape(b, -1, heads, HEAD_DIM),
            v_tail.reshape(b, -1, heads, HEAD_DIM))


def _mixer(x, p, l, states, prompt):
    b, s, d = x.shape
    conv_st, lru_st, cache_k, cache_v, ssm_re_st, ssm_im_st = states
    if l % 2 == 0:
        e = l // 2
        w = p['conv_w'].shape[-1]
        if prompt:
            c_prev = jnp.zeros((b, CONV_WIDTH - 1, w), F32)
            h_prev = jnp.zeros((b, w), F32)
            ck = cv = None
        else:
            c_prev, h_prev, ck, cv = conv_st[e], lru_st[e], cache_k[e], cache_v[e]
        pe = dict(p, mix_norm_l=p['mix_norm'][l])
        x, c_new, h_new, k_new, v_new = _even_mixer(x, pe, e, c_prev, h_prev, ck, cv, prompt)
        return x, dict(conv=c_new, lru=h_new, k=k_new, v=v_new)
    o = l // 2
    g, st = p['ssm_A_re'].shape[1:]
    if prompt:
        x, n_re, n_im = _odd_mixer(x, p['mix_norm'][l], jnp.zeros((b, g, st), F32),
                                   jnp.zeros((b, g, st), F32), p['s5_prep'][o], p['ssm_D'][o],
                                   p['glu_w'][o], min(s // SEGMENTS, 64))
    else:
        x, n_re, n_im = _odd_mixer_seq(x, p['mix_norm'][l], ssm_re_st[o], ssm_im_st[o], p['s5_prep'][o],
                                       p['ssm_D'][o], p['glu_w'][o])
    return x, dict(re=n_re, im=n_im)


def _trunks(x_prompt, x_sample, p, sample_states):
    xs = [x_prompt, x_sample]
    d = x_prompt.shape[2]
    depth = p['ffn1_norm'].shape[0]
    outs = [dict(conv=[], lru=[], k=[], v=[], re=[], im=[]) for _ in xs]
    states = [(None,) * 6, sample_states]

    def ffn(norm, weights, l):
        ya, yb = _ffn(xs[0].reshape(-1, d), xs[1].reshape(-1, d), norm[l], *weights, l)
        return [ya.reshape(xs[0].shape), yb.reshape(xs[1].shape)]

    for l in range(depth):
        xs = ffn(p['ffn1_norm'], p['ffn1_w'], l)
        for i in range(2):
            xs[i], new = _mixer(xs[i], p, l, states[i], prompt=(i == 0))
            for name, val in new.items():
                outs[i][name].append(val)
        xs = ffn(p['ffn2_norm'], p['ffn2_w'], l)
    return [(x,) + tuple(jnp.stack(o[name]) for name in ('conv', 'lru', 'k', 'v', 're', 'im'))
            for x, o in zip(xs, outs)]


def kernel(x_prompt, x_sample, state_rglru_conv, state_rglru_h, cache_band_k, cache_band_v,
           state_ssm_re, state_ssm_im, ffn1_norm, ffn1_w_in, ffn1_w_out, mix_norm, ffn2_norm,
           ffn2_w_in, ffn2_w_out, ab_w_in, conv_w, conv_b, lru_wa, lru_ba, lru_wx, lru_bx,
           lru_lambda, q_norm, k_norm, rel_bias, ab_w_out, ssm_A_re, ssm_A_im, ssm_B_re, ssm_B_im,
           ssm_C_re, ssm_C_im, ssm_D, ssm_log_dt, glu_w):
    p = dict(ffn1_norm=ffn1_norm, mix_norm=mix_norm, ffn2_norm=ffn2_norm, ab_w_in=ab_w_in,
             conv_w=conv_w, conv_b=conv_b, lru_wa=lru_wa, lru_ba=lru_ba, lru_wx=lru_wx, lru_bx=lru_bx,
             lru_lambda=lru_lambda, q_norm=q_norm, k_norm=k_norm, rel_bias=rel_bias, ab_w_out=ab_w_out,
             ssm_A_re=ssm_A_re, ssm_A_im=ssm_A_im, ssm_B_re=ssm_B_re, ssm_B_im=ssm_B_im,
             ssm_C_re=ssm_C_re, ssm_C_im=ssm_C_im, ssm_D=ssm_D, ssm_log_dt=ssm_log_dt, glu_w=glu_w)
    p['ffn1_w'] = (ffn1_w_in.astype(BF16), ffn1_w_out.astype(BF16))
    p['ffn2_w'] = (ffn2_w_in.astype(BF16), ffn2_w_out.astype(BF16))
    p['s5_prep'] = [_prep_s5(ssm_A_re[o], ssm_A_im[o], ssm_B_re[o], ssm_B_im[o], ssm_C_re[o], ssm_C_im[o],
                             ssm_log_dt[o]) for o in range(ssm_A_re.shape[0])]
    (y_prompt, p_conv, p_h, p_k, p_v, p_re, p_im), (y_sample, s_conv, s_h, s_k, s_v, s_re, s_im) = _trunks(
        x_prompt, x_sample, p, (state_rglru_conv, state_rglru_h, cache_band_k, cache_band_v,
                                state_ssm_re, state_ssm_im))
    return (y_prompt, y_sample, p_conv, p_h, p_k, p_v, p_re, p_im, s_conv, s_h, s_k, s_v, s_re, s_im)
```

```python
import functools
import math

import jax
import jax.numpy as jnp
from jax import lax
from jax.experimental import pallas as pl
from jax.experimental.pallas import tpu as pltpu

F32 = jnp.float32
BF16 = jnp.bfloat16

LANES = 128
SUBLANES = 8
VMEM_LIMIT_BYTES = 56 * 1024 * 1024

EPS = 1e-6
CHUNK = 64
LEFT_CHUNKS = 8
WINDOW = LEFT_CHUNKS * CHUNK
MAX_REL = 128
PAST_LEN = 4096
HEAD_DIM = 64
CONV_WIDTH = 4
LRU_C = 8.0
SSM_GROUP = 16
SEGMENTS = SUBLANES
S5_GROUP_BLOCK = 16
S5_ENDS_STEPS = 4
S5_ENDS_ROWS = 1024
FFN_CHUNK = 256
ATTN_BLOCK = 256
ATTN_BLOCKS_PER_STEP = 4
PROJ_PIECE = 256
FFN_ROWS = 512
NEG_INF = -1e30


def _params(*sem):
    return pltpu.CompilerParams(dimension_semantics=sem, vmem_limit_bytes=VMEM_LIMIT_BYTES)


def _resident(shape):
    nd = len(shape)
    return pl.BlockSpec(shape, lambda *_: (0,) * nd, pipeline_mode=pl.Buffered(1))


def _rms(x, g):
    return x * lax.rsqrt(jnp.mean(x * x, axis=-1, keepdims=True) + EPS) * g


def _sigmoid(x):
    return 1.0 / (1.0 + jnp.exp(-x))


def _exact_zero_from(v):
    bits = pltpu.bitcast(v[:SUBLANES, :LANES].astype(F32), jnp.uint32)
    half = lax.shift_right_logical(bits, jnp.uint32(16))
    return pltpu.bitcast(lax.shift_right_logical(half, jnp.uint32(16)), F32)


def _ffn_body(xa_ref, xb_ref, g_ref, win_ref, wout_ref, oa_ref, ob_ref, acc_ref):
    d_ff = wout_ref.shape[0]
    first = pl.program_id(0) == 0
    x = jnp.where(first, xb_ref[...], xa_ref[...])
    h = _rms(x, g_ref[...]).astype(BF16)
    n_chunks = d_ff // FFN_CHUNK
    for c in range(n_chunks):
        cols = slice(c * FFN_CHUNK, (c + 1) * FFN_CHUNK)
        up_cols = slice(d_ff + c * FFN_CHUNK, d_ff + (c + 1) * FFN_CHUNK)
        gate = jnp.dot(h, win_ref[:, cols], preferred_element_type=F32)
        up = jnp.dot(h, win_ref[:, up_cols], preferred_element_type=F32)
        act = (gate * _sigmoid(gate) * up).astype(BF16)
        y = jnp.dot(act, wout_ref[cols, :], preferred_element_type=F32)
        if c == 0:
            acc_ref[...] = y
        elif c < n_chunks - 1:
            acc_ref[...] += y
        else:
            oa_ref[...] = x + 0.5 * (acc_ref[...] + y)

    @pl.when(first)
    def _():
        ob_ref[...] = oa_ref[...]


def _ffn(xa, xb, g, w_in_all, w_out_all, layer):
    n, d = xa.shape
    tm = FFN_ROWS
    _, d_ff, _ = w_out_all.shape
    assert d_ff % FFN_CHUNK == 0 and n % tm == 0 and xb.shape == (tm, d)
    a_spec = pl.BlockSpec((tm, d), lambda i: (jnp.maximum(i - 1, 0), 0))
    b_spec = pl.BlockSpec((tm, d), lambda i: (0, 0))
    return pl.pallas_call(
        _ffn_body,
        grid=(n // tm + 1,),
        in_specs=[
            a_spec, b_spec, _resident((1, d)),
            pl.BlockSpec((None, d, 2 * d_ff), lambda i: (layer, 0, 0), pipeline_mode=pl.Buffered(1)),
            pl.BlockSpec((None, d_ff, d), lambda i: (layer, 0, 0), pipeline_mode=pl.Buffered(1)),
        ],
        out_specs=[a_spec, b_spec],
        out_shape=[jax.ShapeDtypeStruct((n, d), F32), jax.ShapeDtypeStruct((tm, d), F32)],
        scratch_shapes=[pltpu.VMEM((tm, d), F32)],
        compiler_params=_params("arbitrary"),
        name="ffn",
    )(xa, xb, g.reshape(1, d), w_in_all, w_out_all)


def _head_norm(t, gain):
    low = lax.broadcasted_iota(jnp.int32, (t.shape[0], LANES), 1) < HEAD_DIM
    outs = []
    for j in range(t.shape[1] // LANES):
        blk = t[:, j * LANES:(j + 1) * LANES]
        sq = blk * blk
        tot = jnp.sum(sq, axis=-1, keepdims=True)
        lo = jnp.sum(jnp.where(low, sq, 0.0), axis=-1, keepdims=True)
        ms = jnp.where(low, lo, tot - lo) * (1.0 / HEAD_DIM)
        outs.append(blk * lax.rsqrt(ms + EPS) * gain[:, j * LANES:(j + 1) * LANES])
    return jnp.concatenate(outs, axis=-1)


def _block_diag(w):
    nb, n, _ = w.shape
    eye = jnp.eye(nb, dtype=w.dtype)
    return jnp.einsum('hij,hg->higj', w, eye).reshape(nb * n, nb * n)


def _rg_lru(xa, ga_ref, c0_ref, h0_ref, cw_ref, cb_ref, wa_ref, ba_ref, wx_ref, bx_ref, lam_ref,
            ya_ref, hlast_ref, ctail_ref, ext_ref, hc_ref, a_ref, u_ref, interleave=()):
    t_len, w = xa.shape

    @pl.when(pl.program_id(1) == 0)
    def _():
        ext_ref[0:SUBLANES, :] = c0_ref[0]
        hc_ref[...] = jnp.broadcast_to(h0_ref[0], hc_ref.shape)

    ext_ref[SUBLANES:SUBLANES + t_len, :] = xa
    xc = cb_ref[...] + cw_ref[3:4, :] * xa
    for k in range(CONV_WIDTH - 1):
        off = SUBLANES - (CONV_WIDTH - 1) + k
        xc = xc + cw_ref[k:k + 1, :] * ext_ref[off:off + t_len, :]
    tail = xa[t_len - SUBLANES:, :]
    ext_ref[0:SUBLANES, :] = tail
    ctail_ref[0] = tail

    xcb = xc.astype(BF16)
    r = _sigmoid(jnp.dot(xcb, wa_ref[...], preferred_element_type=F32) + ba_ref[...])
    i = _sigmoid(jnp.dot(xcb, wx_ref[...], preferred_element_type=F32) + bx_ref[...])
    neg_lam = -lam_ref[...]
    softplus = jnp.maximum(neg_lam, 0.0) + jnp.log1p(jnp.exp(-jnp.abs(neg_lam)))
    log_a = -LRU_C * r * softplus
    a = jnp.exp(log_a)
    a_ref[...] = a
    u_ref[...] = jnp.sqrt(-jnp.tanh(log_a) * (a * a + 1.0)) * (i * xc)

    row = lax.broadcasted_iota(jnp.int32, (SUBLANES, w), 0)
    h = hc_ref[...]
    n_blk = t_len // SUBLANES
    for blk in range(n_blk):
        if interleave and blk % (n_blk // len(interleave)) == 0 and blk // (n_blk // len(interleave)) < len(interleave):
            zero = _exact_zero_from(interleave[blk // (n_blk // len(interleave))]())
            h = h + jnp.concatenate([zero] * (w // LANES), axis=-1)
        rows = slice(blk * SUBLANES, (blk + 1) * SUBLANES)
        a = a_ref[rows, :]
        u = u_ref[rows, :]
        for d in (1, 2, 4):
            keep = row >= d
            a_sh = jnp.where(keep, pltpu.roll(a, d, 0), 1.0)
            u_sh = jnp.where(keep, pltpu.roll(u, d, 0), 0.0)
            u = u + a * u_sh
            a = a * a_sh
        hs = a * h + u
        u_ref[rows, :] = hs
        h = jnp.broadcast_to(hs[SUBLANES - 1:SUBLANES, :], (SUBLANES, w))
    hc_ref[...] = h
    hlast_ref[0] = h
    ya_ref[0] = (u_ref[...] * jax.nn.gelu(ga_ref[...])).astype(ya_ref.dtype)


def _inproj_body(x_ref, g_ref, w_ref, qg_ref, kg_ref, c0_ref, h0_ref, cw_ref, cb_ref, wa_ref,
                 ba_ref, wx_ref, bx_ref, lam_ref, ya_ref, q_ref, k_ref, v_ref, kt_ref, vt_ref,
                 hlast_ref, ctail_ref, ext_ref, hc_ref, a_ref, u_ref, pb_ref):
    w = kt_ref.shape[2]
    h = _rms(x_ref[0], g_ref[...]).astype(BF16)
    xa = jnp.dot(h, w_ref[:, :w], preferred_element_type=F32)
    n_pieces = 4 * w // PROJ_PIECE

    def piece(j):
        cols = slice(j * PROJ_PIECE, (j + 1) * PROJ_PIECE)
        pb_ref[:, cols] = jnp.dot(h, w_ref[:, w + j * PROJ_PIECE:w + (j + 1) * PROJ_PIECE],
                                  preferred_element_type=F32)
        return pb_ref[:SUBLANES, j * PROJ_PIECE:j * PROJ_PIECE + LANES]

    _rg_lru(xa, pb_ref.at[:, :w], c0_ref, h0_ref, cw_ref, cb_ref, wa_ref, ba_ref, wx_ref, bx_ref,
            lam_ref, ya_ref, hlast_ref, ctail_ref, ext_ref, hc_ref, a_ref, u_ref,
            interleave=[functools.partial(piece, j) for j in range(n_pieces)])
    q = _head_norm(pb_ref[:, 1 * w:2 * w], qg_ref[...]) * (1.0 / math.sqrt(HEAD_DIM))
    k = _head_norm(pb_ref[:, 2 * w:3 * w], kg_ref[...])
    v = pb_ref[:, 3 * w:4 * w]
    kt_ref[0] = k
    vt_ref[0] = v
    for pr in range(w // LANES):
        lanes = slice(pr * LANES, (pr + 1) * LANES)
        q_ref[0, pr] = q[:, lanes].astype(BF16)
        k_ref[0, pr] = k[:, lanes].astype(BF16)
        v_ref[0, pr] = v[:, lanes].astype(BF16)


def _inproj(x, g, w_in, q_gain, k_gain, conv0, h0, conv_w, conv_b, wa, ba, wx, bx, lam, tm):
    b, s, d = x.shape
    w = w_in.shape[1] // 5
    heads = w // HEAD_DIM
    npair = w // LANES
    assert s % tm == 0 and tm % SUBLANES == 0
    pad = jnp.zeros((b, SUBLANES - (CONV_WIDTH - 1), w), F32)
    c0 = jnp.concatenate([pad, conv0.astype(F32)], axis=1)
    row = pl.BlockSpec((1, tm, w), lambda i, t: (i, t, 0))
    pair = pl.BlockSpec((1, npair, tm, LANES), lambda i, t: (i, 0, t, 0))
    tail = pl.BlockSpec((1, tm, w), lambda i, t: (i, 0, 0))
    state = pl.BlockSpec((1, SUBLANES, w), lambda i, t: (i, 0, 0))
    vec = _resident((1, w))
    return pl.pallas_call(
        _inproj_body,
        grid=(b, s // tm),
        in_specs=[
            pl.BlockSpec((1, tm, d), lambda i, t: (i, t, 0)),
            _resident((1, d)), _resident(w_in.shape), vec, vec,
            state, pl.BlockSpec((1, 1, w), lambda i, t: (i, 0, 0)),
            _resident((CONV_WIDTH, w)), vec, _resident((w, w)), vec, _resident((w, w)), vec, vec,
        ],
        out_specs=[row, pair, pair, pair, tail, tail, state, state],
        out_shape=[jax.ShapeDtypeStruct((b, s, w), BF16)]
        + [jax.ShapeDtypeStruct((b, npair, s, LANES), BF16)] * 3
        + [jax.ShapeDtypeStruct((b, tm, w), F32)] * 2
        + [jax.ShapeDtypeStruct((b, SUBLANES, w), F32)] * 2,
        scratch_shapes=[
            pltpu.VMEM((SUBLANES + tm, w), F32),
            pltpu.VMEM((SUBLANES, w), F32),
            pltpu.VMEM((tm, w), F32),
            pltpu.VMEM((tm, w), F32),
            pltpu.VMEM((tm, 4 * w), F32),
        ],
        compiler_params=_params("parallel", "arbitrary"),
        name="inproj",
    )(x, g.reshape(1, d), w_in.astype(BF16),
      jnp.tile(q_gain, heads).reshape(1, w), jnp.tile(k_gain, heads).reshape(1, w),
      c0, h0.astype(F32).reshape(b, 1, w), conv_w, conv_b.reshape(1, w),
      _block_diag(wa).astype(BF16), ba.reshape(1, w), _block_diag(wx).astype(BF16),
      bx.reshape(1, w), lam.reshape(1, w))


def _attn_body(*refs, n_prev, n_blocks):
    n_parts = n_prev + n_blocks
    q_ref = refs[0]
    k_parts = refs[1:1 + n_parts]
    v_parts = refs[1 + n_parts:1 + 2 * n_parts]
    bias_ref, o_ref = refs[1 + 2 * n_parts:]
    last_variant = bias_ref.shape[0] - 1
    npair = q_ref.shape[1]
    nq = q_ref.shape[2] // n_blocks
    low = lax.broadcasted_iota(jnp.int32, (nq, LANES), 1) < HEAD_DIM
    for pr in range(npair):
        for j in range(n_blocks):
            rows = slice(j * nq, (j + 1) * nq)
            qp = q_ref[0, pr, rows, :]
            zero = jnp.zeros_like(qp)
            qs = jnp.concatenate([jnp.where(low, qp, zero), jnp.where(low, zero, qp)], axis=0)
            kw = jnp.concatenate([part[0, pr] for part in k_parts[j:j + n_prev + 1]], axis=0)
            vw = jnp.concatenate([part[0, pr] for part in v_parts[j:j + n_prev + 1]], axis=0)
            s = lax.dot_general(qs, kw, (((1,), (1,)), ((), ())), preferred_element_type=F32)
            variant = jnp.minimum(n_blocks * pl.program_id(1) + j, last_variant)
            s = s + bias_ref[variant, pr]
            p = jnp.exp(s - jnp.max(s, axis=-1, keepdims=True))
            denom = jnp.sum(p, axis=-1, keepdims=True)
            o = jnp.dot(p.astype(BF16), vw, preferred_element_type=F32) / denom
            o_ref[0, pr, rows, :] = jnp.where(low, o[:nq], o[nq:]).astype(o_ref.dtype)


def _band_bias(rel_bias, nq, invalid_cols):
    heads = rel_bias.shape[1]
    win = WINDOW + nq
    period = win + nq
    n_hi = WINDOW - MAX_REL
    n_lo = max(win - (WINDOW + MAX_REL + 1), 0)
    tab = rel_bias.astype(F32)
    mid = jnp.flip(tab, axis=0)[:win - n_hi - n_lo]
    vec = jnp.concatenate([jnp.broadcast_to(tab[-1:], (n_hi, heads)), mid,
                           jnp.broadcast_to(tab[:1], (n_lo, heads)),
                           jnp.broadcast_to(tab[-1:], (nq, heads))], axis=0)
    toep = jnp.tile(vec.T, (1, nq))[:, :nq * (period - 1)].reshape(heads, nq, period - 1)[:, :, :win]
    qc = jnp.arange(nq)[:, None] // CHUNK
    kc = jnp.arange(win)[None, :] // CHUNK
    band = (kc >= qc) & (kc <= qc + LEFT_CHUNKS)
    col = jnp.arange(win)[None, :]
    out = [jnp.where(band & (col >= c), toep, NEG_INF) for c in invalid_cols]
    return jnp.stack(out).reshape(len(invalid_cols), heads // 2, 2 * nq, win)


def _attn(q, k_parts, v_parts, bias, n_blocks, n_prev):
    b, npair, s, _ = q.shape
    assert len(k_parts) == len(v_parts) == n_prev + n_blocks
    rows = sum(r for _, r, _ in k_parts[n_prev:])

    def spec(nrows, index_map):
        return pl.BlockSpec((1, npair, nrows, LANES), index_map)

    own = spec(rows, lambda i, t: (i, 0, t, 0))
    return pl.pallas_call(
        functools.partial(_attn_body, n_prev=n_prev, n_blocks=n_blocks),
        grid=(b, s // rows),
        in_specs=([own] + [spec(r, m) for _, r, m in k_parts] + [spec(r, m) for _, r, m in v_parts]
                  + [_resident(bias.shape)]),
        out_specs=own,
        out_shape=jax.ShapeDtypeStruct((b, npair, s, LANES), BF16),
        compiler_params=_params("parallel", "arbitrary"),
        name="attn",
    )(q, *[a for a, _, _ in k_parts], *[a for a, _, _ in v_parts], bias)


def _outproj_body(x_ref, ya_ref, yb_ref, w_ref, o_ref):
    w = ya_ref.shape[2]
    yb = jnp.concatenate([yb_ref[0, pr] for pr in range(yb_ref.shape[1])], axis=-1)
    y = jnp.dot(ya_ref[0], w_ref[:w, :], preferred_element_type=F32)
    y = y + jnp.dot(yb, w_ref[w:, :], preferred_element_type=F32)
    o_ref[0] = x_ref[0] + y


def _outproj(x, ya, yb, w_out, tm):
    b, s, d = x.shape
    w = ya.shape[2]
    npair = yb.shape[1]
    return pl.pallas_call(
        _outproj_body,
        grid=(b, s // tm),
        in_specs=[
            pl.BlockSpec((1, tm, d), lambda i, t: (i, t, 0)),
            pl.BlockSpec((1, tm, w), lambda i, t: (i, t, 0)),
            pl.BlockSpec((1, npair, tm, LANES), lambda i, t: (i, 0, t, 0)),
            _resident(w_out.shape),
        ],
        out_specs=pl.BlockSpec((1, tm, d), lambda i, t: (i, t, 0)),
        out_shape=jax.ShapeDtypeStruct((b, s, d), F32),
        compiler_params=_params("parallel", "parallel"),
        name="outproj",
    )(x, ya, yb, w_out.astype(BF16))


def _s5_load_h(x_ref, g_ref, hnat, hperm):
    tm = x_ref.shape[2]
    pitch = hnat.shape[1] // SEGMENTS
    n_lane_blocks = hnat.shape[0]
    for r in range(SEGMENTS):
        h = _rms(x_ref[0, r], g_ref[...])
        for c in range(n_lane_blocks):
            hnat[c, r * pitch:r * pitch + tm, :] = h[:, c * LANES:(c + 1) * LANES]
    for m in range(tm):
        for c in range(n_lane_blocks):
            hperm[m * SEGMENTS:(m + 1) * SEGMENTS, c * LANES:(c + 1) * LANES] = (
                hnat[c, pl.ds(m, SEGMENTS, stride=pitch), :])


def _s5_pitch(tm):
    assert tm % SUBLANES == 0
    return tm + SUBLANES // 2


def _s5_scan(bu, sbuf, st, are_ref, aim_ref, jb, tm):
    half = bu.shape[1] // 2
    base = jb * bu.shape[1]
    step = 4 * LANES
    for c0 in range(0, half, step):
        ar = are_ref[jb, :, c0:c0 + step]
        ai = aim_ref[jb, :, c0:c0 + step]
        sr = st[:, base + c0:base + c0 + step]
        si = st[:, base + half + c0:base + half + c0 + step]
        for m in range(tm):
            rows = slice(m * SEGMENTS, (m + 1) * SEGMENTS)
            nr = ar * sr - ai * si + bu[rows, c0:c0 + step]
            ni = ar * si + ai * sr + bu[rows, half + c0:half + c0 + step]
            if sbuf is not None:
                sbuf[rows, c0:c0 + step] = nr
                sbuf[rows, half + c0:half + c0 + step] = ni
            sr, si = nr, ni
        st[:, base + c0:base + c0 + step] = sr
        st[:, base + half + c0:base + half + c0 + step] = si


def _s5_ends_body(x_ref, g_ref, bblk_ref, ajr_ref, aji_ref, ends_ref, hnat, hperm, bu0, bu1, st):
    tm, d = x_ref.shape[2], x_ref.shape[3]
    nb, kbj, _ = bblk_ref.shape
    kb = d // nb
    steps = kbj // kb
    pitch = hnat.shape[1] // SEGMENTS
    bus = (bu0, bu1)

    @pl.when(pl.program_id(1) == 0)
    def _():
        st[...] = jnp.zeros_like(st)

    for r in range(SEGMENTS):
        h = _rms(x_ref[0, r], g_ref[...])
        for c in range(hnat.shape[0]):
            hnat[c, r * pitch:r * pitch + tm, :] = h[:, c * LANES:(c + 1) * LANES]
    for mb in range(tm // steps):
        for j in range(steps):
            for c in range(hnat.shape[0]):
                jb, within = divmod(c * LANES, kb)
                dst = jb * kbj + j * kb + within
                hperm[mb * SEGMENTS:(mb + 1) * SEGMENTS, dst:dst + LANES] = (
                    hnat[c, pl.ds(mb * steps + j, SEGMENTS, stride=pitch), :])
    hb = hperm[...].astype(BF16)

    def b_u(jb):
        return jnp.dot(hb[:, jb * kbj:(jb + 1) * kbj], bblk_ref[jb], preferred_element_type=F32)

    bus[0][...] = b_u(0)
    for jb in range(nb):
        if jb + 1 < nb:
            bus[(jb + 1) % 2][...] = b_u(jb + 1)
        _s5_scan(bus[jb % 2], None, st, ajr_ref, aji_ref, jb, tm // steps)

    @pl.when(pl.program_id(1) == pl.num_programs(1) - 1)
    def _():
        ends_ref[0] = st[...]


def _s5_main_body(x_ref, g_ref, bblk_ref, cblk_ref, are_ref, aim_ref, apr_ref, api_ref, d_ref,
                  glu_ref, ends_ref, s0_ref, o_ref, slast_ref, hnat, hperm, bu0, bu1, sb0, sb1,
                  ybuf, st):
    tm = x_ref.shape[2]
    bus = (bu0, bu1)
    sbs = (sb0, sb1)
    d = x_ref.shape[3]
    nb, kb, two_half = bblk_ref.shape
    half = two_half // 2

    @pl.when(pl.program_id(1) == 0)
    def _():
        for jb in range(nb):
            re_cols = slice(jb * two_half, jb * two_half + half)
            im_cols = slice(jb * two_half + half, (jb + 1) * two_half)
            pr = apr_ref[jb, 0:1, :]
            pi = api_ref[jb, 0:1, :]
            er = s0_ref[0, :, re_cols]
            ei = s0_ref[0, :, im_cols]
            for r in range(SEGMENTS):
                st[r:r + 1, re_cols] = er
                st[r:r + 1, im_cols] = ei
                nr = pr * er - pi * ei + ends_ref[0, r:r + 1, re_cols]
                ni = pr * ei + pi * er + ends_ref[0, r:r + 1, im_cols]
                er, ei = nr, ni
            slast_ref[0, :, re_cols] = er
            slast_ref[0, :, im_cols] = ei

    _s5_load_h(x_ref, g_ref, hnat, hperm)
    h = hperm[...]
    hb = h.astype(BF16)

    def b_u(jb):
        return jnp.dot(hb[:, jb * kb:(jb + 1) * kb], bblk_ref[jb], preferred_element_type=F32)

    bus[0][...] = b_u(0)
    for jb in range(nb):
        if jb + 1 < nb:
            bus[(jb + 1) % 2][...] = b_u(jb + 1)
        _s5_scan(bus[jb % 2], sbs[jb % 2], st, are_ref, aim_ref, jb, tm)
        ybuf[:, jb * kb:(jb + 1) * kb] = jnp.dot(sbs[jb % 2][...].astype(BF16), cblk_ref[jb],
                                                 preferred_element_type=F32)
    y = ybuf[...] + d_ref[...] * h
    z = jnp.dot(y.astype(BF16), glu_ref[...], preferred_element_type=F32)
    o = z[:, :d] * _sigmoid(z[:, d:])
    for c in range(hnat.shape[0]):
        hnat[c, 0:SEGMENTS * tm, :] = o[:, c * LANES:(c + 1) * LANES]
    for r in range(SEGMENTS):
        for c in range(hnat.shape[0]):
            lanes = slice(c * LANES, (c + 1) * LANES)
            o_ref[0, r, :, lanes] = x_ref[0, r, :, lanes] + hnat[c, pl.ds(r, tm, stride=SEGMENTS), :]


def _s5_expand_body(bc_ref, cc_ref, bblk_ref, bends_ref, cblk_ref):
    n_b, _, kb, p = bc_ref.shape
    i_dim = cc_ref.shape[3]
    half = bblk_ref.shape[2] // 2
    log_p, log_i = p.bit_length() - 1, i_dim.bit_length() - 1
    assert p == 1 << log_p and i_dim == 1 << log_i

    def iota(shape, axis):
        return lax.broadcasted_iota(jnp.int32, shape, axis)

    sel_b = jnp.where((iota((p, half), 1) & (p - 1)) == iota((p, half), 0), 1.0, 0.0).astype(BF16)
    mask_b = (iota((kb, half), 0) >> log_i) == (iota((kb, half), 1) >> log_p)
    sel_c = jnp.where((iota((i_dim, kb), 1) & (i_dim - 1)) == iota((i_dim, kb), 0), 1.0, 0.0).astype(BF16)
    mask_c = (iota((half, kb), 0) >> log_p) == (iota((half, kb), 1) >> log_i)

    def expand_b(m):
        t = jnp.dot(bc_ref[m, 0].astype(BF16), sel_b, preferred_element_type=F32)
        return jnp.where(mask_b, t, 0.0).astype(BF16)

    bblk_ref[0, :, :half] = expand_b(0)
    bblk_ref[0, :, half:] = expand_b(1)
    for j in range((n_b - 2) // 2):
        bends_ref[0, j * kb:(j + 1) * kb, :half] = expand_b(2 + 2 * j)
        bends_ref[0, j * kb:(j + 1) * kb, half:] = expand_b(3 + 2 * j)
    for m in range(2):
        t = jnp.dot(cc_ref[m, 0].astype(BF16), sel_c, preferred_element_type=F32)
        cblk_ref[0, m * half:(m + 1) * half, :] = jnp.where(mask_c, t, 0.0).astype(BF16)


def _prep_s5(a_re, a_im, b_re, b_im, c_re, c_im, log_dt):
    g, p = a_re.shape
    gb = S5_GROUP_BLOCK
    nb = g // gb
    a = lax.complex(a_re.astype(F32), a_im.astype(F32))
    dt = jnp.exp(log_dt.astype(F32))[:, None]
    a_bar = jnp.exp(a * dt)
    b_bar = ((a_bar - 1.0) / a)[..., None] * lax.complex(b_re.astype(F32), b_im.astype(F32))
    mats = [b_bar]
    for _ in range(S5_ENDS_STEPS - 1):
        mats.append(mats[-1] * a_bar[..., None])
    mats = [b_bar] + mats[::-1]
    bc = jnp.stack([part for m in mats for part in (m.real, m.imag)])
    bc = bc.transpose(0, 1, 3, 2).reshape(len(mats) * 2, nb, gb * SSM_GROUP, p)
    cc = jnp.stack([c_re.astype(F32), -c_im.astype(F32)]).transpose(0, 1, 3, 2)
    cc = cc.reshape(2, nb, gb * p, SSM_GROUP)
    kb, half = gb * SSM_GROUP, gb * p
    bblk, bblk_ends, cblk = pl.pallas_call(
        _s5_expand_body,
        grid=(nb,),
        in_specs=[pl.BlockSpec((bc.shape[0], 1, kb, p), lambda j: (0, j, 0, 0)),
                  pl.BlockSpec((2, 1, half, SSM_GROUP), lambda j: (0, j, 0, 0))],
        out_specs=[pl.BlockSpec((1, kb, 2 * half), lambda j: (j, 0, 0)),
                   pl.BlockSpec((1, S5_ENDS_STEPS * kb, 2 * half), lambda j: (j, 0, 0)),
                   pl.BlockSpec((1, 2 * half, kb), lambda j: (j, 0, 0))],
        out_shape=[jax.ShapeDtypeStruct((nb, kb, 2 * half), BF16),
                   jax.ShapeDtypeStruct((nb, S5_ENDS_STEPS * kb, 2 * half), BF16),
                   jax.ShapeDtypeStruct((nb, 2 * half, kb), BF16)],
        compiler_params=_params("parallel"),
        name="s5_expand",
    )(bc, cc)
    return a_bar, bblk, bblk_ends, cblk


def _s5_lanes(v):
    g, p = v.shape
    nb = g // S5_GROUP_BLOCK
    return jnp.broadcast_to(v.reshape(nb, 1, S5_GROUP_BLOCK * p), (nb, SUBLANES, S5_GROUP_BLOCK * p))


def _s5_seq_body(x_ref, g_ref, bblk_ref, cblk_ref, are_ref, aim_ref, d_ref, glu_ref, s0_ref,
                 o_ref, slast_ref, hnat, hperm, bu0, bu1, sb0, sb1, ybuf, st):
    nseq, tt, d = x_ref.shape
    nb, kb, _ = bblk_ref.shape
    pitch = hnat.shape[1] // nseq
    bus = (bu0, bu1)
    sbs = (sb0, sb1)
    st[...] = s0_ref[...]
    for b in range(nseq):
        hb_nat = _rms(x_ref[b], g_ref[...])
        for c in range(hnat.shape[0]):
            hnat[c, b * pitch:b * pitch + tt, :] = hb_nat[:, c * LANES:(c + 1) * LANES]
    for t in range(tt):
        for c in range(hnat.shape[0]):
            hperm[t * nseq:(t + 1) * nseq, c * LANES:(c + 1) * LANES] = (
                hnat[c, pl.ds(t, nseq, stride=pitch), :])
    h = hperm[...]
    hb = h.astype(BF16)

    def b_u(jb):
        return jnp.dot(hb[:, jb * kb:(jb + 1) * kb], bblk_ref[jb], preferred_element_type=F32)

    bus[0][...] = b_u(0)
    for jb in range(nb):
        if jb + 1 < nb:
            bus[(jb + 1) % 2][...] = b_u(jb + 1)
        _s5_scan(bus[jb % 2], sbs[jb % 2], st, are_ref, aim_ref, jb, tt)
        ybuf[:, jb * kb:(jb + 1) * kb] = jnp.dot(sbs[jb % 2][...].astype(BF16), cblk_ref[jb],
                                                 preferred_element_type=F32)
    y = ybuf[...] + d_ref[...] * h
    z = jnp.dot(y.astype(BF16), glu_ref[...], preferred_element_type=F32)
    o = z[:, :d] * _sigmoid(z[:, d:])
    for c in range(hnat.shape[0]):
        hnat[c, 0:nseq * tt, :] = o[:, c * LANES:(c + 1) * LANES]
    for b in range(nseq):
        for c in range(hnat.shape[0]):
            lanes = slice(c * LANES, (c + 1) * LANES)
            o_ref[b, :, lanes] = x_ref[b, :, lanes] + hnat[c, pl.ds(b, tt, stride=nseq), :]
    slast_ref[...] = st[...]


def _s5_state_to_lanes(s_re, s_im):
    b, g, p = s_re.shape
    nb = g // S5_GROUP_BLOCK
    both = jnp.stack([s_re.reshape(b, nb, S5_GROUP_BLOCK * p), s_im.reshape(b, nb, S5_GROUP_BLOCK * p)],
                     axis=2)
    return both.reshape(b, 1, 2 * g * p).astype(F32)


def _s5_state_from_lanes(s, g, p):
    b = s.shape[0]
    both = s.reshape(b, g // S5_GROUP_BLOCK, 2, S5_GROUP_BLOCK, p)
    return both[:, :, 0].reshape(b, g, p), both[:, :, 1].reshape(b, g, p)


def _odd_mixer(x, norm_g, s_re, s_im, prep, d_skip, glu_w, tm):
    b, s, d = x.shape
    g, p = s_re.shape[1:]
    seg_len = s // SEGMENTS
    a_bar, bblk, bblk_ends, cblk = prep
    a_pow = a_bar
    for _ in range(int(math.log2(seg_len))):
        a_pow = a_pow * a_pow
    a_ends = a_bar
    for _ in range(S5_ENDS_STEPS - 1):
        a_ends = a_ends * a_bar
    are, aim, apr, api, ajr, aji = (_s5_lanes(v) for v in (a_bar.real, a_bar.imag, a_pow.real,
                                                            a_pow.imag, a_ends.real, a_ends.imag))
    nb, kb, two_half = bblk.shape
    n_state = nb * two_half
    xv = x.reshape(b, SEGMENTS, seg_len, d)
    rows = SEGMENTS * tm
    grid = (b, seg_len // tm)
    x_spec = pl.BlockSpec((1, SEGMENTS, tm, d), lambda i, t: (i, 0, t, 0))
    lane_spec = _resident(are.shape)
    g2 = norm_g.reshape(1, d)

    tm_e = min(seg_len, S5_ENDS_ROWS // SEGMENTS)
    assert seg_len % tm_e == 0 and tm_e % S5_ENDS_STEPS == 0
    ends = pl.pallas_call(
        _s5_ends_body,
        grid=(b, seg_len // tm_e),
        in_specs=[pl.BlockSpec((1, SEGMENTS, tm_e, d), lambda i, t: (i, 0, t, 0)), _resident((1, d)),
                  _resident(bblk_ends.shape), lane_spec, lane_spec],
        out_specs=pl.BlockSpec((1, SEGMENTS, n_state), lambda i, t: (i, 0, 0)),
        out_shape=jax.ShapeDtypeStruct((b, SEGMENTS, n_state), F32),
        scratch_shapes=[
            pltpu.VMEM((d // LANES, SEGMENTS * _s5_pitch(tm_e), LANES), F32),
            pltpu.VMEM((SEGMENTS * tm_e // S5_ENDS_STEPS, S5_ENDS_STEPS * d), F32),
            pltpu.VMEM((SEGMENTS * tm_e // S5_ENDS_STEPS, two_half), F32),
            pltpu.VMEM((SEGMENTS * tm_e // S5_ENDS_STEPS, two_half), F32),
            pltpu.VMEM((SEGMENTS, n_state), F32),
        ],
        compiler_params=_params("parallel", "arbitrary"),
        name="s5_ends",
    )(xv, g2, bblk_ends, ajr, aji)

    out, s_last = pl.pallas_call(
        _s5_main_body,
        grid=grid,
        in_specs=[
            x_spec, _resident((1, d)), _resident(bblk.shape), _resident(cblk.shape),
            lane_spec, lane_spec, lane_spec, lane_spec, _resident((1, d)),
            _resident(glu_w.shape),
            pl.BlockSpec((1, SEGMENTS, n_state), lambda i, t: (i, 0, 0)),
            pl.BlockSpec((1, 1, n_state), lambda i, t: (i, 0, 0)),
        ],
        out_specs=[x_spec, pl.BlockSpec((1, 1, n_state), lambda i, t: (i, 0, 0))],
        out_shape=[jax.ShapeDtypeStruct(xv.shape, F32),
                   jax.ShapeDtypeStruct((b, 1, n_state), F32)],
        scratch_shapes=[
            pltpu.VMEM((d // LANES, SEGMENTS * _s5_pitch(tm), LANES), F32), pltpu.VMEM((rows, d), F32),
            pltpu.VMEM((rows, two_half), F32), pltpu.VMEM((rows, two_half), F32),
            pltpu.VMEM((rows, two_half), F32), pltpu.VMEM((rows, two_half), F32),
            pltpu.VMEM((rows, d), F32), pltpu.VMEM((SEGMENTS, n_state), F32),
        ],
        compiler_params=_params("parallel", "arbitrary"),
        name="s5_main",
    )(xv, g2, bblk, cblk, are, aim, apr, api, d_skip.reshape(1, d).astype(F32),
      glu_w.astype(BF16), ends, _s5_state_to_lanes(s_re, s_im))
    n_re, n_im = _s5_state_from_lanes(s_last, g, p)
    return out.reshape(b, s, d), n_re, n_im


def _odd_mixer_seq(x, norm_g, s_re, s_im, prep, d_skip, glu_w):
    nseq, tt, d = x.shape
    g, p = s_re.shape[1:]
    a_bar, bblk, _, cblk = prep
    are, aim = _s5_lanes(a_bar.real), _s5_lanes(a_bar.imag)
    nb, kb, two_half = bblk.shape
    n_state = nb * two_half
    rows = nseq * tt
    assert nseq == SUBLANES and tt % SUBLANES == 0
    s0 = _s5_state_to_lanes(s_re, s_im).reshape(nseq, n_state)
    out, s_last = pl.pallas_call(
        _s5_seq_body,
        out_shape=[jax.ShapeDtypeStruct(x.shape, F32), jax.ShapeDtypeStruct((nseq, n_state), F32)],
        scratch_shapes=[
            pltpu.VMEM((d // LANES, nseq * _s5_pitch(tt), LANES), F32), pltpu.VMEM((rows, d), F32),
            pltpu.VMEM((rows, two_half), F32), pltpu.VMEM((rows, two_half), F32),
            pltpu.VMEM((rows, two_half), F32), pltpu.VMEM((rows, two_half), F32),
            pltpu.VMEM((rows, d), F32), pltpu.VMEM((SUBLANES, n_state), F32),
        ],
        compiler_params=pltpu.CompilerParams(vmem_limit_bytes=VMEM_LIMIT_BYTES),
        name="s5_seq",
    )(x, norm_g.reshape(1, d), bblk, cblk, are, aim, d_skip.reshape(1, d).astype(F32),
      glu_w.astype(BF16), s0)
    n_re, n_im = _s5_state_from_lanes(s_last.reshape(nseq, 1, n_state), g, p)
    return out, n_re, n_im


def _to_pairs(t):
    b, rows, heads, hd = t.shape
    return t.reshape(b, rows, heads // 2, 2 * hd).transpose(0, 2, 1, 3).astype(BF16)


def _even_mixer(x, p, e, conv0, h0, cache_k, cache_v, prompt):
    b, s, d = x.shape
    tm = min(s, WINDOW)
    ya, q, k, v, k_tail, v_tail, h_last, c_tail = _inproj(
        x, p['mix_norm_l'], p['ab_w_in'][e], p['q_norm'][e], p['k_norm'][e], conv0, h0,
        p['conv_w'][e], p['conv_b'][e], p['lru_wa'][e], p['lru_ba'][e], p['lru_wx'][e],
        p['lru_bx'][e], p['lru_lambda'][e], tm)
    w = ya.shape[2]
    heads = w // HEAD_DIM
    h_new = h_last[:, 0, :]
    conv_new = c_tail[:, SUBLANES - (CONV_WIDTH - 1):, :]
    if prompt:
        nq = ATTN_BLOCK
        n_prev = WINDOW // nq
        nblk = ATTN_BLOCKS_PER_STEP
        bias = _band_bias(p['rel_bias'][e], nq, [WINDOW - nq * v for v in range(n_prev + 1)])
        maps = [functools.partial(lambda i, t, off: (i, 0, jnp.maximum(nblk * t + off, 0), 0),
                                  off=idx - n_prev) for idx in range(n_prev + nblk)]
        yb = _attn(q, [(k, nq, m) for m in maps], [(v, nq, m) for m in maps], bias, nblk, n_prev)
    else:
        wc = cache_k.shape[1]
        assert wc == WINDOW and s == CHUNK and PAST_LEN % CHUNK == 0
        bias = _band_bias(p['rel_bias'][e], s, [0])
        first = lambda i, t: (i, 0, 0, 0)
        yb = _attn(q, [(_to_pairs(cache_k), wc, first), (k, s, first)],
                   [(_to_pairs(cache_v), wc, first), (v, s, first)], bias, 1, 1)
    y = _outproj(x, ya, yb, p['ab_w_out'][e], tm)
    return (y, conv_new, h_new, k_tail.reshape(b, -1, heads, HEAD_DIM),
            v_tail.reshape(b, -1, heads, HEAD_DIM))


def _mixer(x, p, l, states, prompt):
    b, s, d = x.shape
    conv_st, lru_st, cache_k, cache_v, ssm_re_st, ssm_im_st = states
    if l % 2 == 0:
        e = l // 2
        w = p['conv_w'].shape[-1]
        if prompt:
            c_prev = jnp.zeros((b, CONV_WIDTH - 1, w), F32)
            h_prev = jnp.zeros((b, w), F32)
            ck = cv = None
        else:
            c_prev, h_prev, ck, cv = conv_st[e], lru_st[e], cache_k[e], cache_v[e]
        pe = dict(p, mix_norm_l=p['mix_norm'][l])
        x, c_new, h_new, k_new, v_new = _even_mixer(x, pe, e, c_prev, h_prev, ck, cv, prompt)
        return x, dict(conv=c_new, lru=h_new, k=k_new, v=v_new)
    o = l // 2
    g, st = p['ssm_A_re'].shape[1:]
    if prompt:
        x, n_re, n_im = _odd_mixer(x, p['mix_norm'][l], jnp.zeros((b, g, st), F32),
                                   jnp.zeros((b, g, st), F32), p['s5_prep'][o], p['ssm_D'][o],
                                   p['glu_w'][o], min(s // SEGMENTS, 64))
    else:
        x, n_re, n_im = _odd_mixer_seq(x, p['mix_norm'][l], ssm_re_st[o], ssm_im_st[o], p['s5_prep'][o],
                                       p['ssm_D'][o], p['glu_w'][o])
    return x, dict(re=n_re, im=n_im)


def _trunks(x_prompt, x_sample, p, sample_states):
    xs = [x_prompt, x_sample]
    d = x_prompt.shape[2]
    depth = p['ffn1_norm'].shape[0]
    outs = [dict(conv=[], lru=[], k=[], v=[], re=[], im=[]) for _ in xs]
    states = [(None,) * 6, sample_states]

    def ffn(norm, weights, l):
        ya, yb = _ffn(xs[0].reshape(-1, d), xs[1].reshape(-1, d), norm[l], *weights, l)
        return [ya.reshape(xs[0].shape), yb.reshape(xs[1].shape)]

    for l in range(depth):
        xs = ffn(p['ffn1_norm'], p['ffn1_w'], l)
        for i in range(2):
            xs[i], new = _mixer(xs[i], p, l, states[i], prompt=(i == 0))
            for name, val in new.items():
                outs[i][name].append(val)
        xs = ffn(p['ffn2_norm'], p['ffn2_w'], l)
    return [(x,) + tuple(jnp.stack(o[name]) for name in ('conv', 'lru', 'k', 'v', 're', 'im'))
            for x, o in zip(xs, outs)]


def kernel(x_prompt, x_sample, state_rglru_conv, state_rglru_h, cache_band_k, cache_band_v,
           state_ssm_re, state_ssm_im, ffn1_norm, ffn1_w_in, ffn1_w_out, mix_norm, ffn2_norm,
           ffn2_w_in, ffn2_w_out, ab_w_in, conv_w, conv_b, lru_wa, lru_ba, lru_wx, lru_bx,
           lru_lambda, q_norm, k_norm, rel_bias, ab_w_out, ssm_A_re, ssm_A_im, ssm_B_re, ssm_B_im,
           ssm_C_re, ssm_C_im, ssm_D, ssm_log_dt, glu_w):
    p = dict(ffn1_norm=ffn1_norm, mix_norm=mix_norm, ffn2_norm=ffn2_norm, ab_w_in=ab_w_in,
             conv_w=conv_w, conv_b=conv_b, lru_wa=lru_wa, lru_ba=lru_ba, lru_wx=lru_wx, lru_bx=lru_bx,
             lru_lambda=lru_lambda, q_norm=q_norm, k_norm=k_norm, rel_bias=rel_bias, ab_w_out=ab_w_out,
             ssm_A_re=ssm_A_re, ssm_A_im=ssm_A_im, ssm_B_re=ssm_B_re, ssm_B_im=ssm_B_im,
             ssm_C_re=ssm_C_re, ssm_C_im=ssm_C_im, ssm_D=ssm_D, ssm_log_dt=ssm_log_dt, glu_w=glu_w)
    p['ffn1_w'] = (ffn1_w_in.astype(BF16), ffn1_w_out.astype(BF16))
    p['ffn2_w'] = (ffn2_w_in.astype(BF16), ffn2_w_out.astype(BF16))
    p['s5_prep'] = [_prep_s5(ssm_A_re[o], ssm_A_im[o], ssm_B_re[o], ssm_B_im[o], ssm_C_re[o], ssm_C_im[o],
                             ssm_log_dt[o]) for o in range(ssm_A_re.shape[0])]
    (y_prompt, p_conv, p_h, p_k, p_v, p_re, p_im), (y_sample, s_conv, s_h, s_k, s_v, s_re, s_im) = _trunks(
        x_prompt, x_sample, p, (state_rglru_conv, state_rglru_h, cache_band_k, cache_band_v,
                                state_ssm_re, state_ssm_im))
    return (y_prompt, y_sample, p_conv, p_h, p_k, p_v, p_re, p_im, s_conv, s_h, s_k, s_v, s_re, s_im)
```
